```python
import jax
import jax.numpy as jnp
from jax import lax
import numpy as np

D_MODEL = 4096
BATCH = 4
SEQ = 2048
DEPTH = 1

CHUNK = 64
LEFT_CHUNKS = 8
REL_CLIP = 128
N_REL = CHUNK + REL_CLIP
HEAD_DIM = 128
HEADS_A = 16
HEADS_B = 16
Q_LORA = 1024
KV_LORA = 512
NOPE_DIM = 128
ROPE_DIM = 64
V_DIM = 128
QK_B = NOPE_DIM + ROPE_DIM
ROPE_THETA = 10000.0
Q_BLOCK = 128
MEM_TOKENS = 256
MEM_HEADS = 4
MEM_HEAD_DIM = 128
N_EXPERTS = 64
N_GROUPS = 8
TOPK_GROUPS = 4
TOP_K = 8
EXPERT_FF = 512
SHARED_FF = 512
ROUTED_SCALE = 2.5
EXPERT_BLOCK = 256
EPS = 1e-6

WIDTH_A = HEADS_A * HEAD_DIM
IN_WIDTH = 3 * WIDTH_A + Q_LORA + KV_LORA + ROPE_DIM
MIX_WIDTH = WIDTH_A + HEADS_B * V_DIM

kernel_name = 'hybrid_chunked_mla_moe_block'


def rms_norm(x, g):
    xf = x.astype(jnp.float32)
    y = xf * lax.rsqrt(jnp.mean(xf * xf, axis=-1, keepdims=True) + EPS)
    return (y * g.astype(jnp.float32)).astype(x.dtype)


def rotate_half(x):
    x1, x2 = jnp.split(x, 2, axis=-1)
    return jnp.concatenate([-x2, x1], axis=-1)


def apply_rope(x, cos, sin):
    return x * cos + rotate_half(x) * sin


def rope_tables(positions, dtype):
    inv_freq = ROPE_THETA ** (-jnp.arange(0, ROPE_DIM, 2, dtype=jnp.float32) / ROPE_DIM)
    ang = positions.astype(jnp.float32)[..., None] * inv_freq
    ang = jnp.concatenate([ang, ang], axis=-1)[:, :, None, :]
    return jnp.cos(ang).astype(dtype), jnp.sin(ang).astype(dtype)


def masked_softmax(s, valid):
    return jax.nn.softmax(jnp.where(valid, s, -jnp.inf), axis=-1)


def chunk_band_attention(q, k, v, rel_bias):
    B, S, H, Dh = q.shape
    n_chunks = S // CHUNK
    pad = LEFT_CHUNKS * CHUNK
    band = pad + CHUNK
    kp = jnp.pad(k, ((0, 0), (pad, 0), (0, 0), (0, 0)))
    vp = jnp.pad(v, ((0, 0), (pad, 0), (0, 0), (0, 0)))
    i = jnp.arange(CHUNK)[:, None]
    j = jnp.arange(band)[None, :]
    dist = pad + i - j
    bias = rel_bias[:, jnp.clip(dist, -(CHUNK - 1), REL_CLIP) + CHUNK - 1].astype(jnp.float32)
    key_chunk_rel = j // CHUNK - LEFT_CHUNKS
    scale = Dh ** -0.5

    def one_chunk(c):
        qc = lax.dynamic_slice_in_dim(q, c * CHUNK, CHUNK, axis=1)
        kc = lax.dynamic_slice_in_dim(kp, c * CHUNK, band, axis=1)
        vc = lax.dynamic_slice_in_dim(vp, c * CHUNK, band, axis=1)
        s = jnp.einsum('bqhd,bkhd->bhqk', qc, kc).astype(jnp.float32) * scale + bias[None]
        valid = (c + key_chunk_rel >= 0)[None, None]
        p = masked_softmax(s, valid).astype(vc.dtype)
        return jnp.einsum('bhqk,bkhd->bqhd', p, vc)

    out = lax.map(one_chunk, jnp.arange(n_chunks))
    return out.transpose(1, 0, 2, 3, 4).reshape(B, S, H, Dh)


def mla_attention(c_q, c_kv, k_pe, cos, sin, g_cq, w_uq, g_ckv, w_ukv, g_qb, g_kb):
    B, S, _ = c_q.shape
    q = (rms_norm(c_q, g_cq) @ w_uq).reshape(B, S, HEADS_B, QK_B)
    kv = (rms_norm(c_kv, g_ckv) @ w_ukv).reshape(B, S, HEADS_B, NOPE_DIM + V_DIM)
    k_nope, v = kv[..., :NOPE_DIM], kv[..., NOPE_DIM:]
    q_nope = rms_norm(q[..., :NOPE_DIM], g_qb[:NOPE_DIM])
    q_pe = apply_rope(rms_norm(q[..., NOPE_DIM:], g_qb[NOPE_DIM:]), cos, sin)
    k_nope = rms_norm(k_nope, g_kb[:NOPE_DIM])
    k_pe = apply_rope(rms_norm(k_pe, g_kb[NOPE_DIM:])[:, :, None, :], cos, sin)
    q = jnp.concatenate([q_nope, q_pe], axis=-1)
    k = jnp.concatenate([k_nope, jnp.broadcast_to(k_pe, (B, S, HEADS_B, ROPE_DIM))], axis=-1)
    scale = QK_B ** -0.5
    outs = []
    for blk in range(S // Q_BLOCK):
        lo, hi = blk * Q_BLOCK, (blk + 1) * Q_BLOCK
        s = jnp.einsum('bqhd,bkhd->bhqk', q[:, lo:hi], k[:, :hi]).astype(jnp.float32) * scale
        q_chunk = jnp.arange(lo, hi) // CHUNK
        k_chunk = jnp.arange(hi) // CHUNK
        p = masked_softmax(s, (k_chunk[None, :] <= q_chunk[:, None])[None, None]).astype(v.dtype)
        outs.append(jnp.einsum('bhqk,bkhd->bqhd', p, v[:, :hi]))
    return jnp.concatenate(outs, axis=1)


def memory_cross_attention(h, m, w_xq, w_xkv, g_qx, g_kx, w_xo):
    B, S, _ = h.shape
    M = m.shape[1]
    q = rms_norm((h @ w_xq).reshape(B, S, MEM_HEADS, MEM_HEAD_DIM), g_qx)
    kv = (m @ w_xkv).reshape(B, M, 2, MEM_HEADS, MEM_HEAD_DIM)
    k = rms_norm(kv[:, :, 0], g_kx)
    v = kv[:, :, 1]
    s = jnp.einsum('bqhd,bkhd->bhqk', q, k).astype(jnp.float32) * MEM_HEAD_DIM ** -0.5
    p = jax.nn.softmax(s, axis=-1).astype(v.dtype)
    o = jnp.einsum('bhqk,bkhd->bqhd', p, v).reshape(B, S, MEM_HEADS * MEM_HEAD_DIM)
    return o @ w_xo


def route(t, w_router, router_bias):
    N = t.shape[0]
    scores = jax.nn.sigmoid((t @ w_router).astype(jnp.float32))
    choice = scores + router_bias.astype(jnp.float32)
    grp = choice.reshape(N, N_GROUPS, N_EXPERTS // N_GROUPS)
    grp_score = lax.top_k(grp, 2)[0].sum(-1)
    _, top_groups = lax.top_k(grp_score, TOPK_GROUPS)
    group_ok = jnp.any(top_groups[:, :, None] == jnp.arange(N_GROUPS)[None, None, :], axis=1)
    expert_ok = jnp.repeat(group_ok, N_EXPERTS // N_GROUPS, axis=1)
    _, eid = lax.top_k(jnp.where(expert_ok, choice, -jnp.inf), TOP_K)
    w = jnp.take_along_axis(scores, eid, axis=-1)
    w = w / (w.sum(-1, keepdims=True) + 1e-20) * ROUTED_SCALE
    return eid, w


def routed_experts(t, eid, gate, w_ex_gate, w_ex_up, w_ex_down):
    N, D = t.shape
    NK = N * TOP_K
    e_flat = eid.reshape(NK)
    tok_flat = jnp.repeat(jnp.arange(N, dtype=jnp.int32), TOP_K)
    order = jnp.argsort(e_flat)
    e_sorted = e_flat[order]
    counts = jnp.bincount(e_flat, length=N_EXPERTS)
    padded = (counts + EXPERT_BLOCK - 1) // EXPERT_BLOCK * EXPERT_BLOCK
    start = jnp.cumsum(counts) - counts
    pad_end = jnp.cumsum(padded)
    pad_start = pad_end - padded
    dest = pad_start[e_sorted] + jnp.arange(NK, dtype=jnp.int32) - start[e_sorted]
    P = -(-NK // EXPERT_BLOCK) * EXPERT_BLOCK + N_EXPERTS * EXPERT_BLOCK
    n_blocks = P // EXPERT_BLOCK
    tok_pad = jnp.full((P,), N, jnp.int32).at[dest].set(tok_flat[order])
    g_pad = jnp.zeros((P,), t.dtype).at[dest].set(gate.reshape(NK)[order].astype(t.dtype))
    block_e = jnp.minimum(jnp.searchsorted(pad_end, jnp.arange(n_blocks, dtype=jnp.int32) * EXPERT_BLOCK, side='right'), N_EXPERTS - 1)
    t_pad = jnp.concatenate([t, jnp.zeros((1, D), t.dtype)], axis=0)

    def block_step(y, blk):
        tok, g, e = blk
        xb = t_pad[tok]
        hb = jax.nn.silu(xb @ w_ex_gate[e]) * (xb @ w_ex_up[e])
        return y.at[tok].add((hb @ w_ex_down[e]) * g[:, None]), None

    y, _ = lax.scan(block_step, jnp.zeros((N + 1, D), t.dtype),
                    (tok_pad.reshape(n_blocks, EXPERT_BLOCK), g_pad.reshape(n_blocks, EXPERT_BLOCK), block_e))
    return y[:N]


def moe_ffn(h, w_router, router_bias, w_sh_gate, w_sh_up, w_sh_down, w_ex_gate, w_ex_up, w_ex_down):
    B, S, D = h.shape
    t = h.reshape(B * S, D)
    eid, gate = route(t, w_router, router_bias)
    shared = (jax.nn.silu(t @ w_sh_gate) * (t @ w_sh_up)) @ w_sh_down
    return (shared + routed_experts(t, eid, gate, w_ex_gate, w_ex_up, w_ex_down)).reshape(B, S, D)


def setup_inputs(seed: int = 0) -> dict:
    key = jax.random.key(seed)
    ks = jax.random.split(key, 32)
    f32 = jnp.float32

    def w(k, shape, fan_in):
        return jax.random.normal(k, shape, f32) * fan_in ** -0.5

    def gain(k, n):
        return 1.0 + 0.02 * jax.random.normal(k, (n,), f32)

    x = jax.random.normal(ks[0], (BATCH, SEQ, D_MODEL), f32)
    mem = jax.random.normal(ks[1], (BATCH, MEM_TOKENS, D_MODEL), f32)
    offset = jax.random.randint(ks[2], (BATCH, 1), 0, 4096, jnp.int32)
    positions = offset + jnp.arange(SEQ, dtype=jnp.int32)[None, :]
    return {
        'x': x, 'mem': mem, 'positions': positions,
        'g_mix': gain(ks[3], D_MODEL),
        'w_in': w(ks[4], (D_MODEL, IN_WIDTH), D_MODEL),
        'g_qa': gain(ks[5], HEAD_DIM),
        'g_ka': gain(ks[6], HEAD_DIM),
        'rel_bias': 0.1 * jax.random.normal(ks[7], (HEADS_A, N_REL), f32),
        'g_cq': gain(ks[8], Q_LORA),
        'w_uq': w(ks[9], (Q_LORA, HEADS_B * QK_B), Q_LORA),
        'g_ckv': gain(ks[10], KV_LORA),
        'w_ukv': w(ks[11], (KV_LORA, HEADS_B * (NOPE_DIM + V_DIM)), KV_LORA),
        'g_qb': gain(ks[12], QK_B),
        'g_kb': gain(ks[13], QK_B),
        'w_o': w(ks[14], (MIX_WIDTH, D_MODEL), MIX_WIDTH),
        'g_cross': gain(ks[15], D_MODEL),
        'g_mem': gain(ks[16], D_MODEL),
        'w_xq': w(ks[17], (D_MODEL, MEM_HEADS * MEM_HEAD_DIM), D_MODEL),
        'w_xkv': w(ks[18], (D_MODEL, 2 * MEM_HEADS * MEM_HEAD_DIM), D_MODEL),
        'g_qx': gain(ks[19], MEM_HEAD_DIM),
        'g_kx': gain(ks[20], MEM_HEAD_DIM),
        'w_xo': w(ks[21], (MEM_HEADS * MEM_HEAD_DIM, D_MODEL), MEM_HEADS * MEM_HEAD_DIM),
        'g_ffn': gain(ks[22], D_MODEL),
        'w_router': w(ks[23], (D_MODEL, N_EXPERTS), D_MODEL),
        'router_bias': 0.01 * jax.random.normal(ks[24], (N_EXPERTS,), f32),
        'w_sh_gate': w(ks[25], (D_MODEL, SHARED_FF), D_MODEL),
        'w_sh_up': w(ks[26], (D_MODEL, SHARED_FF), D_MODEL),
        'w_sh_down': w(ks[27], (SHARED_FF, D_MODEL), SHARED_FF),
        'w_ex_gate': w(ks[28], (N_EXPERTS, D_MODEL, EXPERT_FF), D_MODEL),
        'w_ex_up': w(ks[29], (N_EXPERTS, D_MODEL, EXPERT_FF), D_MODEL),
        'w_ex_down': w(ks[30], (N_EXPERTS, EXPERT_FF, D_MODEL), EXPERT_FF),
    }


def reference(x, mem, positions, g_mix, w_in, g_qa, g_ka, rel_bias, g_cq, w_uq, g_ckv, w_ukv, g_qb, g_kb, w_o,
              g_cross, g_mem, w_xq, w_xkv, g_qx, g_kx, w_xo, g_ffn, w_router, router_bias,
              w_sh_gate, w_sh_up, w_sh_down, w_ex_gate, w_ex_up, w_ex_down):
    B, S, D = x.shape
    cos, sin = rope_tables(positions, x.dtype)
    mem_n = rms_norm(mem, g_mem)
    split_at = [3 * WIDTH_A, 3 * WIDTH_A + Q_LORA, 3 * WIDTH_A + Q_LORA + KV_LORA]
    for _ in range(DEPTH):
        h = rms_norm(x, g_mix)
        qkv_a, c_q, c_kv, k_pe = jnp.split(h @ w_in, split_at, axis=-1)
        qkv_a = qkv_a.reshape(B, S, 3, HEADS_A, HEAD_DIM)
        o_a = chunk_band_attention(rms_norm(qkv_a[:, :, 0], g_qa), rms_norm(qkv_a[:, :, 1], g_ka),
                                   qkv_a[:, :, 2], rel_bias)
        o_b = mla_attention(c_q, c_kv, k_pe, cos, sin, g_cq, w_uq, g_ckv, w_ukv, g_qb, g_kb)
        mixed = jnp.concatenate([o_a.reshape(B, S, WIDTH_A), o_b.reshape(B, S, HEADS_B * V_DIM)], axis=-1)
        x = x + mixed @ w_o
        x = x + memory_cross_attention(rms_norm(x, g_cross), mem_n, w_xq, w_xkv, g_qx, g_kx, w_xo)
        x = x + moe_ffn(rms_norm(x, g_ffn), w_router, router_bias, w_sh_gate, w_sh_up, w_sh_down,
                        w_ex_gate, w_ex_up, w_ex_down)
    return x
```

```python
import functools

import jax
import jax.numpy as jnp
from jax import lax
from jax.experimental import pallas as pl
from jax.experimental.pallas import tpu as pltpu

F32 = jnp.float32
BF16 = jnp.bfloat16
I32 = jnp.int32
U32 = jnp.uint32

CHUNK = 64
LEFT_CHUNKS = 8
REL_CLIP = 128
HEAD_DIM = 128
HEADS_A = 16
HEADS_B = 16
Q_LORA = 1024
KV_LORA = 512
NOPE_DIM = 128
ROPE_DIM = 64
V_DIM = 128
QK_B = NOPE_DIM + ROPE_DIM
ROPE_THETA = 10000.0
MEM_HEADS = 4
MEM_HEAD_DIM = 128
N_EXPERTS = 64
N_GROUPS = 8
TOPK_GROUPS = 4
TOP_K = 8
ROUTED_SCALE = 2.5
EPS = 1e-6

LANES = 128
ATT_BLOCK = 256
EXPERT_ROWS = 256
NEG = -1e30
VMEM_LIMIT = 56 * 1024 * 1024
HI_MASK = 0xFFFF0000

_NT = (((1,), (1,)), ((), ()))


def _cp(sem, vmem=VMEM_LIMIT):
    return pltpu.CompilerParams(dimension_semantics=sem, vmem_limit_bytes=vmem)


def _rms(x, g):
    return x * lax.rsqrt(jnp.mean(x * x, axis=-1, keepdims=True) + EPS) * g


def _pack_pairs(x):
    k = x.shape[1] // 2
    bits = pltpu.bitcast(x.astype(BF16).astype(F32), U32)
    return bits[:, k:] | (bits[:, :k] >> 16)


def _unpack_pairs(u):
    lo = pltpu.bitcast(u << 16, F32)
    hi = pltpu.bitcast(u & jnp.uint32(HI_MASK), F32)
    return lo, hi


def _norm_kernel(x_ref, g_ref, o_ref):
    o_ref[...] = _rms(x_ref[...], g_ref[...]).astype(o_ref.dtype)


def _norm_bf16(x, g, tm=256):
    n, d = x.shape
    return pl.pallas_call(
        _norm_kernel,
        grid=(n // tm,),
        in_specs=[pl.BlockSpec((tm, d), lambda i: (i, 0)), pl.BlockSpec((1, d), lambda i: (0, 0))],
        out_specs=pl.BlockSpec((tm, d), lambda i: (i, 0)),
        out_shape=jax.ShapeDtypeStruct((n, d), BF16),
        compiler_params=_cp(("parallel",)),
        name="norm_mix",
    )(x, g.reshape(1, d))


def _in_proj_kernel(*refs, head_norm, tn):
    if head_norm:
        h_ref, w_ref, gs_ref, o_ref = refs
    else:
        h_ref, w_ref, o_ref = refs
    acc = jnp.dot(h_ref[...], w_ref[...].astype(BF16), preferred_element_type=F32)
    if head_norm:
        for c in range(tn // LANES):
            sl = slice(c * LANES, (c + 1) * LANES)
            o_ref[:, sl] = _rms(acc[:, sl], gs_ref[:, sl]).astype(o_ref.dtype)
    else:
        o_ref[...] = acc.astype(o_ref.dtype)


def _in_proj(h, w, col0, ncols, out_dtype, name, gs=None, tm=1024, tn=512):
    n, d = h.shape
    tn = min(tn, ncols)
    j0 = col0 // tn
    in_specs = [pl.BlockSpec((tm, d), lambda i, j: (i, 0)),
                pl.BlockSpec((d, tn), lambda i, j: (0, j0 + j))]
    args = [h, w]
    if gs is not None:
        in_specs.append(pl.BlockSpec((1, tn), lambda i, j: (0, j)))
        args.append(gs.reshape(1, ncols))
    return pl.pallas_call(
        functools.partial(_in_proj_kernel, head_norm=gs is not None, tn=tn),
        grid=(n // tm, ncols // tn),
        in_specs=in_specs,
        out_specs=pl.BlockSpec((tm, tn), lambda i, j: (i, j)),
        out_shape=jax.ShapeDtypeStruct((n, ncols), out_dtype),
        compiler_params=_cp(("parallel", "arbitrary")),
        name=name,
    )(*args)


def _attn_a_kernel(q_ref, k0_ref, k1_ref, k2_ref, v0_ref, v1_ref, v2_ref, b_ref, o_ref, *, heads):
    i = pl.program_id(2)
    k_refs = (k0_ref, k1_ref, k2_ref)
    v_refs = (v0_ref, v1_ref, v2_ref)
    for hh in range(heads):
        sl = slice(hh * HEAD_DIM, (hh + 1) * HEAD_DIM)
        q = q_ref[:, sl]
        s = []
        for d in range(3):
            sd = lax.dot_general(q, k_refs[d][:, sl], _NT, preferred_element_type=F32)
            sd = sd + b_ref[hh, :, d * ATT_BLOCK:(d + 1) * ATT_BLOCK]
            if d > 0:
                sd = jnp.where(i >= d, sd, NEG)
            s.append(sd)
        m = jnp.maximum(jnp.maximum(s[0].max(-1, keepdims=True), s[1].max(-1, keepdims=True)),
                        s[2].max(-1, keepdims=True))
        l = jnp.zeros_like(m)
        o = jnp.zeros((q.shape[0], HEAD_DIM), F32)
        for d in range(3):
            p = jnp.exp(s[d] - m)
            l = l + p.sum(-1, keepdims=True)
            o = o + jnp.dot(p.astype(BF16), v_refs[d][:, sl], preferred_element_type=F32)
        o_ref[:, sl] = (o / l).astype(o_ref.dtype)


def _band_bias(rel_bias):
    r = jnp.arange(ATT_BLOCK)[:, None]
    c = jnp.arange(ATT_BLOCK)[None, :]
    per_blk = ATT_BLOCK // CHUNK
    tiles = []
    for d in range(3):
        dist = ATT_BLOCK * d + r - c
        b = rel_bias[:, jnp.clip(dist, -(CHUNK - 1), REL_CLIP) + CHUNK - 1].astype(F32)
        cdiff = per_blk * d + r // CHUNK - c // CHUNK
        valid = (cdiff >= 0) & (cdiff <= LEFT_CHUNKS)
        tiles.append(jnp.where(valid[None], b, NEG))
    return jnp.concatenate(tiles, axis=-1)


def _attn_a(qk, v, bias, batch, seq, heads_per_step=4):
    n = qk.shape[0]
    nq = seq // ATT_BLOCK
    hw = heads_per_step * HEAD_DIM
    groups = HEADS_A // heads_per_step
    kcol0 = HEADS_A * HEAD_DIM // hw

    def kspec(d, col0):
        return pl.BlockSpec((ATT_BLOCK, hw), lambda g, b, i: (b * nq + jnp.maximum(i - d, 0), col0 + g))

    return pl.pallas_call(
        functools.partial(_attn_a_kernel, heads=heads_per_step),
        grid=(groups, batch, nq),
        in_specs=[pl.BlockSpec((ATT_BLOCK, hw), lambda g, b, i: (b * nq + i, g)),
                  kspec(0, kcol0), kspec(1, kcol0), kspec(2, kcol0),
                  kspec(0, 0), kspec(1, 0), kspec(2, 0),
                  pl.BlockSpec((heads_per_step, ATT_BLOCK, 3 * ATT_BLOCK), lambda g, b, i: (g, 0, 0))],
        out_specs=pl.BlockSpec((ATT_BLOCK, hw), lambda g, b, i: (b * nq + i, g)),
        out_shape=jax.ShapeDtypeStruct((n, HEADS_A * HEAD_DIM), BF16),
        compiler_params=_cp(("parallel", "parallel", "arbitrary")),
        name="attn_band",
    )(qk, qk, qk, qk, v, v, v, bias)


def _rope_pe(pe, g, cos_ref, sa_ref, sb_ref):
    ss = jnp.sum(pe * pe, axis=-1, keepdims=True) * (1.0 / ROPE_DIM)
    pn = pe * lax.rsqrt(ss + EPS) * g
    half = ROPE_DIM // 2
    return (pn * cos_ref[...] + pltpu.roll(pn, LANES - half, 1) * sa_ref[...]
            + pltpu.roll(pn, half, 1) * sb_ref[...])


def _mla_q_kernel(c_ref, gc_ref, w_ref, gq_ref, cos_ref, sa_ref, sb_ref, o_ref, xn_ref, *, heads, scale):
    @pl.when(pl.program_id(1) == 0)
    def _():
        xn_ref[...] = _rms(c_ref[...], gc_ref[...]).astype(BF16)

    acc = jnp.dot(xn_ref[...], w_ref[...].astype(BF16), preferred_element_type=F32)
    for hh in range(heads):
        base = hh * 2 * LANES
        nope = acc[:, base:base + LANES]
        pe = acc[:, base + LANES:base + 2 * LANES]
        o_ref[:, base:base + LANES] = (_rms(nope, gq_ref[:, :LANES]) * scale).astype(o_ref.dtype)
        o_ref[:, base + LANES:base + 2 * LANES] = (
            _rope_pe(pe, gq_ref[:, LANES:], cos_ref, sa_ref, sb_ref) * scale).astype(o_ref.dtype)


def _mla_q(c, g_cq, w_uq_pad, gq, cos_t, sin_a, sin_b, tm=512, heads_per_step=2):
    n = c.shape[0]
    tn = heads_per_step * 2 * LANES
    ncols = w_uq_pad.shape[1]
    row = lambda i, j: (i, 0)
    return pl.pallas_call(
        functools.partial(_mla_q_kernel, heads=heads_per_step, scale=QK_B ** -0.5),
        grid=(n // tm, ncols // tn),
        in_specs=[pl.BlockSpec((tm, Q_LORA), row),
                  pl.BlockSpec((1, Q_LORA), lambda i, j: (0, 0)),
                  pl.BlockSpec((Q_LORA, tn), lambda i, j: (0, j)),
                  pl.BlockSpec((1, 2 * LANES), lambda i, j: (0, 0)),
                  pl.BlockSpec((tm, LANES), row), pl.BlockSpec((tm, LANES), row), pl.BlockSpec((tm, LANES), row)],
        out_specs=pl.BlockSpec((tm, tn), lambda i, j: (i, j)),
        out_shape=jax.ShapeDtypeStruct((n, ncols), BF16),
        scratch_shapes=[pltpu.VMEM((tm, Q_LORA), BF16)],
        compiler_params=_cp(("parallel", "arbitrary")),
        name="mla_q_proj",
    )(c, g_cq.reshape(1, Q_LORA), w_uq_pad, gq, cos_t, sin_a, sin_b)


def _mla_kv_kernel(c_ref, gc_ref, kpe_ref, w_ref, gk_ref, cos_ref, sa_ref, sb_ref, k_ref, v_ref,
                   xn_ref, pe_ref, *, heads):
    @pl.when(pl.program_id(1) == 0)
    def _():
        xn_ref[...] = _rms(c_ref[...], gc_ref[...]).astype(BF16)
        pe_ref[...] = _rope_pe(kpe_ref[...], gk_ref[:, LANES:], cos_ref, sa_ref, sb_ref).astype(BF16)

    acc = jnp.dot(xn_ref[...], w_ref[...].astype(BF16), preferred_element_type=F32)
    for hh in range(heads):
        base = hh * 2 * LANES
        k_ref[:, base:base + LANES] = _rms(acc[:, base:base + LANES], gk_ref[:, :LANES]).astype(k_ref.dtype)
        k_ref[:, base + LANES:base + 2 * LANES] = pe_ref[...]
        v_ref[:, hh * LANES:(hh + 1) * LANES] = acc[:, base + LANES:base + 2 * LANES].astype(v_ref.dtype)


def _mla_kv(c, g_ckv, kpe, w_ukv, gk, cos_t, sin_a, sin_b, tm=512, heads_per_step=2):
    n = c.shape[0]
    tn = heads_per_step * 2 * LANES
    ncols = w_ukv.shape[1]
    cblk = Q_LORA // KV_LORA
    row = lambda i, j: (i, 0)
    return pl.pallas_call(
        functools.partial(_mla_kv_kernel, heads=heads_per_step),
        grid=(n // tm, ncols // tn),
        in_specs=[pl.BlockSpec((tm, KV_LORA), lambda i, j: (i, cblk)),
                  pl.BlockSpec((1, KV_LORA), lambda i, j: (0, 0)),
                  pl.BlockSpec((tm, LANES), row),
                  pl.BlockSpec((KV_LORA, tn), lambda i, j: (0, j)),
                  pl.BlockSpec((1, 2 * LANES), lambda i, j: (0, 0)),
                  pl.BlockSpec((tm, LANES), row), pl.BlockSpec((tm, LANES), row), pl.BlockSpec((tm, LANES), row)],
        out_specs=[pl.BlockSpec((tm, tn), lambda i, j: (i, j)),
                   pl.BlockSpec((tm, tn // 2), lambda i, j: (i, j))],
        out_shape=[jax.ShapeDtypeStruct((n, ncols), BF16),
                   jax.ShapeDtypeStruct((n, ncols // 2), BF16)],
        scratch_shapes=[pltpu.VMEM((tm, KV_LORA), BF16), pltpu.VMEM((tm, LANES), BF16)],
        compiler_params=_cp(("parallel", "arbitrary")),
        name="mla_kv_proj",
    )(c, g_ckv.reshape(1, KV_LORA), kpe, w_ukv, gk, cos_t, sin_a, sin_b)


def _attn_b_kernel(q_ref, k_ref, v_ref, o_ref):
    i = pl.program_id(2)
    q = q_ref[...]
    tq = q.shape[0]

    def step(kb, carry, masked):
        m, l, acc = carry
        start = pl.multiple_of(kb * ATT_BLOCK, ATT_BLOCK)
        s = lax.dot_general(q, k_ref[pl.ds(start, ATT_BLOCK), :], _NT, preferred_element_type=F32)
        if masked:
            r = lax.broadcasted_iota(I32, s.shape, 0) // CHUNK
            c = lax.broadcasted_iota(I32, s.shape, 1) // CHUNK
            s = jnp.where(c <= r, s, NEG)
        m_new = jnp.maximum(m, s.max(-1, keepdims=True))
        alpha = jnp.exp(m - m_new)
        p = jnp.exp(s - m_new)
        l = alpha * l + p.sum(-1, keepdims=True)
        acc = alpha * acc + jnp.dot(p.astype(BF16), v_ref[pl.ds(start, ATT_BLOCK), :],
                                    preferred_element_type=F32)
        return m_new, l, acc

    init = (jnp.full((tq, 1), NEG, F32), jnp.zeros((tq, 1), F32), jnp.zeros((tq, V_DIM), F32))
    carry = lax.fori_loop(0, i, lambda kb, c: step(kb, c, False), init)
    _, l, acc = step(i, carry, True)
    o_ref[...] = (acc / l).astype(o_ref.dtype)


def _attn_b(qf, kf, vb, batch, seq):
    n = qf.shape[0]
    nq = seq // ATT_BLOCK
    return pl.pallas_call(
        _attn_b_kernel,
        grid=(batch, HEADS_B, nq),
        in_specs=[pl.BlockSpec((ATT_BLOCK, 2 * LANES), lambda b, h, i: (b * nq + i, h)),
                  pl.BlockSpec((seq, 2 * LANES), lambda b, h, i: (b, h)),
                  pl.BlockSpec((seq, V_DIM), lambda b, h, i: (b, h))],
        out_specs=pl.BlockSpec((ATT_BLOCK, V_DIM), lambda b, h, i: (b * nq + i, h)),
        out_shape=jax.ShapeDtypeStruct((n, HEADS_B * V_DIM), BF16),
        compiler_params=_cp(("parallel", "parallel", "arbitrary")),
        name="attn_latent",
    )(qf, kf, vb)


def _out_proj_kernel(oa_ref, ob_ref, w_ref, x_ref, o_ref):
    ka = oa_ref.shape[1]
    acc = jnp.dot(oa_ref[...], w_ref[:ka, :].astype(BF16), preferred_element_type=F32)
    acc = acc + jnp.dot(ob_ref[...], w_ref[ka:, :].astype(BF16), preferred_element_type=F32)
    o_ref[...] = x_ref[...] + acc


def _out_proj(oa, ob, w_o, x, tm=512, tn=512):
    n, ka = oa.shape
    kb = ob.shape[1]
    d = w_o.shape[1]
    return pl.pallas_call(
        _out_proj_kernel,
        grid=(d // tn, n // tm),
        in_specs=[pl.BlockSpec((tm, ka), lambda j, i: (i, 0)),
                  pl.BlockSpec((tm, kb), lambda j, i: (i, 0)),
                  pl.BlockSpec((ka + kb, tn), lambda j, i: (0, j)),
                  pl.BlockSpec((tm, tn), lambda j, i: (i, j))],
        out_specs=pl.BlockSpec((tm, tn), lambda j, i: (i, j)),
        out_shape=jax.ShapeDtypeStruct((n, d), F32),
        compiler_params=_cp(("parallel", "arbitrary")),
        name="out_proj",
    )(oa, ob, w_o, x)


def _normed_proj_kernel(x_ref, g_ref, w_ref, gs_ref, o_ref, *, norm_cols):
    h = _rms(x_ref[...], g_ref[...]).astype(BF16)
    acc = jnp.dot(h, w_ref[...].astype(BF16), preferred_element_type=F32)
    for c in range(acc.shape[1] // LANES):
        sl = slice(c * LANES, (c + 1) * LANES)
        if c * LANES < norm_cols:
            o_ref[:, sl] = _rms(acc[:, sl], gs_ref[:, sl]).astype(o_ref.dtype)
        else:
            o_ref[:, sl] = acc[:, sl].astype(o_ref.dtype)


def _normed_proj(x, g, w, gs, norm_cols, name, tm=256):
    n, d = x.shape
    nc = w.shape[1]
    return pl.pallas_call(
        functools.partial(_normed_proj_kernel, norm_cols=norm_cols),
        grid=(n // tm,),
        in_specs=[pl.BlockSpec((tm, d), lambda i: (i, 0)),
                  pl.BlockSpec((1, d), lambda i: (0, 0)),
                  pl.BlockSpec((d, nc), lambda i: (0, 0)),
                  pl.BlockSpec((1, nc), lambda i: (0, 0))],
        out_specs=pl.BlockSpec((tm, nc), lambda i: (i, 0)),
        out_shape=jax.ShapeDtypeStruct((n, nc), BF16),
        compiler_params=_cp(("parallel",)),
        name=name,
    )(x, g.reshape(1, d), w, gs.reshape(1, nc))


def _cross_out_kernel(q_ref, k_ref, v_ref, w_ref, x_ref, o_ref):
    outs = []
    for hh in range(MEM_HEADS):
        sl = slice(hh * MEM_HEAD_DIM, (hh + 1) * MEM_HEAD_DIM)
        s = lax.dot_general(q_ref[:, sl], k_ref[:, sl], _NT, preferred_element_type=F32)
        p = jnp.exp(s - s.max(-1, keepdims=True))
        o = jnp.dot(p.astype(BF16), v_ref[:, sl], preferred_element_type=F32)
        outs.append((o / p.sum(-1, keepdims=True)).astype(BF16))
    acc = jnp.zeros(x_ref.shape, F32)
    for hh in range(MEM_HEADS):
        sl = slice(hh * MEM_HEAD_DIM, (hh + 1) * MEM_HEAD_DIM)
        acc = acc + jnp.dot(outs[hh], w_ref[sl, :].astype(BF16), preferred_element_type=F32)
    o_ref[...] = x_ref[...] + acc


def _cross_out(qx, kx, vx, w_xo, x, batch, seq, mem_tokens, tm=256):
    n, d = x.shape
    per_b = seq // tm
    hw = MEM_HEADS * MEM_HEAD_DIM
    return pl.pallas_call(
        _cross_out_kernel,
        grid=(batch, per_b),
        in_specs=[pl.BlockSpec((tm, hw), lambda b, i: (b * per_b + i, 0)),
                  pl.BlockSpec((mem_tokens, hw), lambda b, i: (b, 0)),
                  pl.BlockSpec((mem_tokens, hw), lambda b, i: (b, 1)),
                  pl.BlockSpec((hw, d), lambda b, i: (0, 0)),
                  pl.BlockSpec((tm, d), lambda b, i: (b * per_b + i, 0))],
        out_specs=pl.BlockSpec((tm, d), lambda b, i: (b * per_b + i, 0)),
        out_shape=jax.ShapeDtypeStruct((n, d), F32),
        compiler_params=_cp(("parallel", "arbitrary")),
        name="cross_attn_out",
    )(qx, kx, vx, w_xo, x)


def _router_kernel(x_ref, g_ref, wr_ref, b_ref, tri_ref, hp_ref, eid_ref, gate_ref, rank_ref, cnt_ref,
                   run_ref):
    t = pl.program_id(0)

    @pl.when(t == 0)
    def _():
        run_ref[...] = jnp.zeros_like(run_ref)

    h = _rms(x_ref[...], g_ref[...])
    hp_ref[...] = _pack_pairs(h)
    tm = h.shape[0]
    per_g = N_EXPERTS // N_GROUPS

    logits = lax.dot_general(wr_ref[...], h, _NT, precision=lax.Precision.HIGHEST,
                             preferred_element_type=F32)
    scores = 1.0 / (1.0 + jnp.exp(-logits))
    choice = scores + b_ref[...]

    sub = lax.broadcasted_iota(I32, (per_g, tm), 0).astype(F32)
    rows = []
    for g in range(N_GROUPS):
        c = choice[g * per_g:(g + 1) * per_g, :]
        m1 = c.max(0, keepdims=True)
        first = jnp.where(c == m1, sub, float(per_g)).min(0, keepdims=True)
        m2 = jnp.where(sub == first, -jnp.inf, c).max(0, keepdims=True)
        rows.append(m1 + m2)
    gs = jnp.concatenate(rows, axis=0)

    gsub = lax.broadcasted_iota(I32, (N_GROUPS, tm), 0).astype(F32)
    beaten = jnp.zeros((N_GROUPS, tm), F32)
    for g2 in range(N_GROUPS):
        row = gs[g2:g2 + 1, :]
        wins = (row > gs) | ((row == gs) & (gsub > float(g2)))
        beaten = beaten + jnp.where(wins, 1.0, 0.0)
    g_ok = jnp.where(beaten < TOPK_GROUPS, 1.0, 0.0)
    e_ok = jnp.concatenate(
        [jnp.broadcast_to(g_ok[g:g + 1, :], (per_g, tm)) for g in range(N_GROUPS)], axis=0)

    eiota = lax.broadcasted_iota(I32, (N_EXPERTS, tm), 0).astype(F32)
    masked = jnp.where(e_ok > 0.5, choice, -jnp.inf)
    chosen = jnp.zeros((N_EXPERTS, tm), F32)
    eids, ws = [], []
    for _ in range(TOP_K):
        m = masked.max(0, keepdims=True)
        idx = jnp.where(masked == m, eiota, float(N_EXPERTS)).min(0, keepdims=True)
        sel = eiota == idx
        ws.append(jnp.where(sel, scores, 0.0).sum(0, keepdims=True))
        masked = jnp.where(sel, -jnp.inf, masked)
        chosen = jnp.where(sel, 1.0, chosen)
        eids.append(idx)
    wsum = ws[0]
    for w in ws[1:]:
        wsum = wsum + w
    denom = wsum + 1e-20

    pos = jnp.dot(chosen.astype(BF16), tri_ref[...], preferred_element_type=F32) + run_ref[:, 0:1]
    run_ref[...] = run_ref[...] + chosen.sum(1, keepdims=True)
    cnt_ref[...] = run_ref[...].astype(I32)

    for r in range(TOP_K):
        eid_ref[r:r + 1, :] = eids[r].astype(I32)
        gate_ref[r:r + 1, :] = ws[r] / denom * ROUTED_SCALE
        rank_ref[r:r + 1, :] = jnp.where(eiota == eids[r], pos, 0.0).sum(0, keepdims=True).astype(I32)


def _router(x, g, w_router, router_bias, tm=512):
    n, d = x.shape
    ne = w_router.shape[1]
    tri = (jnp.arange(tm)[:, None] < jnp.arange(tm)[None, :]).astype(BF16)
    row8 = lambda i: (0, i)
    return pl.pallas_call(
        _router_kernel,
        grid=(n // tm,),
        in_specs=[pl.BlockSpec((tm, d), lambda i: (i, 0)),
                  pl.BlockSpec((1, d), lambda i: (0, 0)),
                  pl.BlockSpec((ne, d), lambda i: (0, 0)),
                  pl.BlockSpec((ne, 1), lambda i: (0, 0)),
                  pl.BlockSpec((tm, tm), lambda i: (0, 0))],
        out_specs=[pl.BlockSpec((tm, d // 2), lambda i: (i, 0)),
                   pl.BlockSpec((TOP_K, tm), row8), pl.BlockSpec((TOP_K, tm), row8),
                   pl.BlockSpec((TOP_K, tm), row8),
                   pl.BlockSpec((ne, LANES), lambda i: (0, 0))],
        out_shape=[jax.ShapeDtypeStruct((n, d // 2), U32),
                   jax.ShapeDtypeStruct((TOP_K, n), I32), jax.ShapeDtypeStruct((TOP_K, n), F32),
                   jax.ShapeDtypeStruct((TOP_K, n), I32),
                   jax.ShapeDtypeStruct((ne, LANES), I32)],
        scratch_shapes=[pltpu.VMEM((ne, LANES), F32)],
        compiler_params=_cp(("arbitrary",)),
        name="router",
    )(x, g.reshape(1, d), w_router.T, router_bias.reshape(ne, 1), tri)


def _dispatch_kernel(cnt_ref, start_ref, nblk_ref, dest_ref, hp_ref, xs_ref, sem, *, tm):
    t = pl.program_id(0)
    pairs = TOP_K * tm

    def row_copy(r, slot):
        return pltpu.make_async_copy(hp_ref.at[pl.ds(r, 1), :], xs_ref.at[pl.ds(slot, 1), :], sem)

    def issue(p, c):
        row_copy(p & (tm - 1), dest_ref[0, 0, p]).start()
        return c

    lax.fori_loop(0, pairs, issue, 0)

    def drain(p, c):
        row_copy(0, 0).wait()
        return c

    lax.fori_loop(0, pairs, drain, 0)

    @pl.when(t == pl.num_programs(0) - 1)
    def _():
        def per_expert(e, c):
            used = cnt_ref[e]
            first = start_ref[e] * EXPERT_ROWS + used
            npad = nblk_ref[e] * EXPERT_ROWS - used

            def pad_issue(p, c2):
                row_copy(0, first + p).start()
                return c2

            def pad_drain(p, c2):
                row_copy(0, 0).wait()
                return c2

            lax.fori_loop(0, npad, pad_issue, 0)
            lax.fori_loop(0, npad, pad_drain, 0)
            return c

        lax.fori_loop(0, N_EXPERTS, per_expert, 0)


def _dispatch(hp, dest, counts, blk_start, nblk, p_rows, tm=256):
    n, w = hp.shape
    tiles = n // tm
    dest_t = dest.reshape(TOP_K, tiles, tm).transpose(1, 0, 2).reshape(tiles, 1, TOP_K * tm)
    grid_spec = pltpu.PrefetchScalarGridSpec(
        num_scalar_prefetch=3,
        grid=(tiles,),
        in_specs=[pl.BlockSpec((1, 1, TOP_K * tm), lambda i, *_: (i, 0, 0), memory_space=pltpu.SMEM),
                  pl.BlockSpec((tm, w), lambda i, *_: (i, 0))],
        out_specs=pl.BlockSpec(memory_space=pl.ANY),
        scratch_shapes=[pltpu.SemaphoreType.DMA(())],
    )
    return pl.pallas_call(
        functools.partial(_dispatch_kernel, tm=tm),
        grid_spec=grid_spec,
        out_shape=jax.ShapeDtypeStruct((p_rows, w), U32),
        compiler_params=_cp(("arbitrary",)),
        name="moe_dispatch",
    )(counts, blk_start, nblk, dest_t, hp)


def _silu_mul(g, u):
    return g / (1.0 + jnp.exp(-g)) * u


def _expert_up_kernel(be_ref, nb_ref, xs_ref, wg_ref, wu_ref, o_ref):
    @pl.when(pl.program_id(0) < nb_ref[0])
    def _():
        lo, hi = _unpack_pairs(xs_ref[...])
        half = lo.shape[1]

        def proj(w_ref):
            return (jnp.dot(lo, w_ref[0, :half, :], preferred_element_type=F32)
                    + jnp.dot(hi, w_ref[0, half:, :], preferred_element_type=F32))

        o_ref[...] = _silu_mul(proj(wg_ref), proj(wu_ref)).astype(o_ref.dtype)


def _expert_up(xs, w_gate, w_up, block_e, n_used, n_blocks):
    p_rows, w = xs.shape
    _, d, ff = w_gate.shape
    blk = lambda b, be, nb: (jnp.minimum(b, nb[0] - 1), 0)
    wsel = lambda b, be, nb: (be[jnp.minimum(b, nb[0] - 1)], 0, 0)
    grid_spec = pltpu.PrefetchScalarGridSpec(
        num_scalar_prefetch=2,
        grid=(n_blocks,),
        in_specs=[pl.BlockSpec((EXPERT_ROWS, w), blk),
                  pl.BlockSpec((1, d, ff), wsel), pl.BlockSpec((1, d, ff), wsel)],
        out_specs=pl.BlockSpec((EXPERT_ROWS, ff), blk),
    )
    return pl.pallas_call(
        _expert_up_kernel,
        grid_spec=grid_spec,
        out_shape=jax.ShapeDtypeStruct((p_rows, ff), BF16),
        compiler_params=_cp(("arbitrary",)),
        name="expert_up",
    )(block_e, n_used, xs, w_gate, w_up)


def _expert_down_kernel(be_ref, nb_ref, h_ref, wd_ref, o_ref):
    @pl.when(pl.program_id(0) < nb_ref[0])
    def _():
        y = jnp.dot(h_ref[...], wd_ref[0].astype(BF16), preferred_element_type=F32)
        o_ref[...] = _pack_pairs(y)


def _expert_down(hs, w_down, block_e, n_used, n_blocks):
    p_rows, ff = hs.shape
    d = w_down.shape[2]
    blk = lambda b, be, nb: (jnp.minimum(b, nb[0] - 1), 0)
    wsel = lambda b, be, nb: (be[jnp.minimum(b, nb[0] - 1)], 0, 0)
    grid_spec = pltpu.PrefetchScalarGridSpec(
        num_scalar_prefetch=2,
        grid=(n_blocks,),
        in_specs=[pl.BlockSpec((EXPERT_ROWS, ff), blk), pl.BlockSpec((1, ff, d), wsel)],
        out_specs=pl.BlockSpec((EXPERT_ROWS, d // 2), blk),
    )
    return pl.pallas_call(
        _expert_down_kernel,
        grid_spec=grid_spec,
        out_shape=jax.ShapeDtypeStruct((p_rows, d // 2), U32),
        compiler_params=_cp(("arbitrary",)),
        name="expert_down",
    )(block_e, n_used, hs, w_down)


def _shared_up_kernel(hp_ref, wg_ref, wu_ref, o_ref):
    lo, hi = _unpack_pairs(hp_ref[...])
    half = lo.shape[1]

    def proj(w_ref):
        return (jnp.dot(lo, w_ref[:half, :], preferred_element_type=F32)
                + jnp.dot(hi, w_ref[half:, :], preferred_element_type=F32))

    o_ref[...] = _silu_mul(proj(wg_ref), proj(wu_ref)).astype(o_ref.dtype)


def _shared_up(hp, w_gate, w_up, tm=512):
    n, w = hp.shape
    d, ff = w_gate.shape
    return pl.pallas_call(
        _shared_up_kernel,
        grid=(n // tm,),
        in_specs=[pl.BlockSpec((tm, w), lambda i: (i, 0)),
                  pl.BlockSpec((d, ff), lambda i: (0, 0)), pl.BlockSpec((d, ff), lambda i: (0, 0))],
        out_specs=pl.BlockSpec((tm, ff), lambda i: (i, 0)),
        out_shape=jax.ShapeDtypeStruct((n, ff), BF16),
        compiler_params=_cp(("parallel",)),
        name="shared_up",
    )(hp, w_gate, w_up)


def _shared_down_kernel(t_ref, w_ref, x_ref, o_ref):
    o_ref[...] = x_ref[...] + jnp.dot(t_ref[...], w_ref[...].astype(BF16), preferred_element_type=F32)


def _shared_down(t, w_down, x, tm=256):
    n, ff = t.shape
    d = w_down.shape[1]
    return pl.pallas_call(
        _shared_down_kernel,
        grid=(n // tm,),
        in_specs=[pl.BlockSpec((tm, ff), lambda i: (i, 0)),
                  pl.BlockSpec((ff, d), lambda i: (0, 0)),
                  pl.BlockSpec((tm, d), lambda i: (i, 0))],
        out_specs=pl.BlockSpec((tm, d), lambda i: (i, 0)),
        out_shape=jax.ShapeDtypeStruct((n, d), F32),
        compiler_params=_cp(("parallel",)),
        name="shared_down",
    )(t, w_down, x)


def _combine_kernel(dest_ref, gate_ref, base_ref, ys_ref, o_ref, buf_ref, sem, *, tm):
    pairs = TOP_K * tm

    def row_copy(slot, k, r):
        return pltpu.make_async_copy(ys_ref.at[pl.ds(slot, 1), :], buf_ref.at[k, pl.ds(r, 1), :], sem)

    def issue(p, c):
        row_copy(dest_ref[0, 0, p], p >> (tm.bit_length() - 1), p & (tm - 1)).start()
        return c

    lax.fori_loop(0, pairs, issue, 0)

    def drain(p, c):
        row_copy(0, 0, 0).wait()
        return c

    lax.fori_loop(0, pairs, drain, 0)

    half = buf_ref.shape[2]
    acc_lo = base_ref[:, :half]
    acc_hi = base_ref[:, half:]
    for k in range(TOP_K):
        lo, hi = _unpack_pairs(buf_ref[k])
        g = gate_ref[:, k:k + 1]
        acc_lo = acc_lo + g * lo
        acc_hi = acc_hi + g * hi
    o_ref[:, :half] = acc_lo
    o_ref[:, half:] = acc_hi


def _combine(dest, gate_rows, base, ys, tm=128):
    n, d = base.shape
    tiles = n // tm
    dest_t = dest.reshape(TOP_K, tiles, tm).transpose(1, 0, 2).reshape(tiles, 1, TOP_K * tm)
    return pl.pallas_call(
        functools.partial(_combine_kernel, tm=tm),
        grid=(tiles,),
        in_specs=[pl.BlockSpec((1, 1, TOP_K * tm), lambda i: (i, 0, 0), memory_space=pltpu.SMEM),
                  pl.BlockSpec((tm, TOP_K), lambda i: (i, 0)),
                  pl.BlockSpec((tm, d), lambda i: (i, 0)),
                  pl.BlockSpec(memory_space=pl.ANY)],
        out_specs=pl.BlockSpec((tm, d), lambda i: (i, 0)),
        out_shape=jax.ShapeDtypeStruct((n, d), F32),
        scratch_shapes=[pltpu.VMEM((TOP_K, tm, d // 2), U32), pltpu.SemaphoreType.DMA(())],
        compiler_params=_cp(("arbitrary",)),
        name="moe_combine",
    )(dest_t, gate_rows, base, ys)


def _rope_tables(positions):
    half = ROPE_DIM // 2
    inv_freq = ROPE_THETA ** (-jnp.arange(0, ROPE_DIM, 2, dtype=F32) / ROPE_DIM)
    ang = positions.reshape(-1).astype(F32)[:, None] * inv_freq
    cos, sin = jnp.cos(ang), jnp.sin(ang)
    n = ang.shape[0]
    z = lambda w: jnp.zeros((n, w), F32)
    cos_t = jnp.concatenate([cos, cos, z(LANES - ROPE_DIM)], axis=1)
    sin_a = jnp.concatenate([-sin, z(LANES - half)], axis=1)
    sin_b = jnp.concatenate([z(half), sin, z(LANES - ROPE_DIM)], axis=1)
    return cos_t, sin_a, sin_b


def _pad_lanes(v, width):
    return jnp.concatenate([v, jnp.zeros((width - v.shape[0],), v.dtype)])


def kernel(x, mem, positions, g_mix, w_in, g_qa, g_ka, rel_bias, g_cq, w_uq, g_ckv, w_ukv, g_qb, g_kb, w_o, g_cross, g_mem, w_xq, w_xkv, g_qx, g_kx, w_xo, g_ffn, w_router, router_bias, w_sh_gate, w_sh_up, w_sh_down, w_ex_gate, w_ex_up, w_ex_down):
    batch, seq, d = x.shape
    n = batch * seq
    mem_tokens = mem.shape[1]
    width_a = HEADS_A * HEAD_DIM
    x2d = x.reshape(n, d)

    h = _norm_bf16(x2d, g_mix)
    gs_qk = jnp.concatenate([jnp.tile(g_qa * HEAD_DIM ** -0.5, HEADS_A), jnp.tile(g_ka, HEADS_A)])
    qk = _in_proj(h, w_in, 0, 2 * width_a, BF16, "in_proj_qk", gs=gs_qk)
    v_a = _in_proj(h, w_in, 2 * width_a, width_a, BF16, "in_proj_v")
    c = _in_proj(h, w_in, 3 * width_a, Q_LORA + KV_LORA, F32, "in_proj_lora")
    w_kpe = jnp.pad(w_in[:, 3 * width_a + Q_LORA + KV_LORA:], ((0, 0), (0, LANES - ROPE_DIM)))
    kpe = _in_proj(h, w_kpe, 0, LANES, F32, "in_proj_kpe")

    o_a = _attn_a(qk, v_a, _band_bias(rel_bias), batch, seq)

    cos_t, sin_a, sin_b = _rope_tables(positions)
    w_uq_pad = jnp.pad(w_uq.reshape(Q_LORA, HEADS_B, QK_B),
                       ((0, 0), (0, 0), (0, 2 * LANES - QK_B))).reshape(Q_LORA, HEADS_B * 2 * LANES)
    gq = _pad_lanes(g_qb, 2 * LANES).reshape(1, 2 * LANES)
    gk = _pad_lanes(g_kb, 2 * LANES).reshape(1, 2 * LANES)
    qf = _mla_q(c, g_cq, w_uq_pad, gq, cos_t, sin_a, sin_b)
    kf, v_b = _mla_kv(c, g_ckv, kpe, w_ukv, gk, cos_t, sin_a, sin_b)
    o_b = _attn_b(qf, kf, v_b, batch, seq)

    x1 = _out_proj(o_a, o_b, w_o, x2d)

    hw = MEM_HEADS * MEM_HEAD_DIM
    qx = _normed_proj(x1, g_cross, w_xq, jnp.tile(g_qx * MEM_HEAD_DIM ** -0.5, MEM_HEADS), hw, "cross_q")
    kvx = _normed_proj(mem.reshape(batch * mem_tokens, d), g_mem, w_xkv,
                       jnp.concatenate([jnp.tile(g_kx, MEM_HEADS), jnp.ones((hw,), F32)]), hw, "cross_kv")
    x2 = _cross_out(qx, kvx, kvx, w_xo, x1, batch, seq, mem_tokens)

    hp, eid, gate, rank, cnt = _router(x2, g_ffn, w_router, router_bias)
    counts = cnt[:, 0]
    nblk = (counts + EXPERT_ROWS - 1) // EXPERT_ROWS
    blk_end = jnp.cumsum(nblk)
    blk_start = blk_end - nblk
    n_blocks = n * TOP_K // EXPERT_ROWS + N_EXPERTS
    dest = blk_start[eid] * EXPERT_ROWS + rank
    block_e = jnp.minimum(jnp.searchsorted(blk_end, jnp.arange(n_blocks, dtype=I32), side="right"),
                          N_EXPERTS - 1).astype(I32)
    n_used = blk_end[-1:].astype(I32)

    xs = _dispatch(hp, dest, counts, blk_start.astype(I32), nblk.astype(I32), n_blocks * EXPERT_ROWS)
    hs = _expert_up(xs, w_ex_gate, w_ex_up, block_e, n_used, n_blocks)
    ys = _expert_down(hs, w_ex_down, block_e, n_used, n_blocks)

    base = _shared_down(_shared_up(hp, w_sh_gate, w_sh_up), w_sh_down, x2)
    out = _combine(dest, gate.T, base, ys)
    return out.reshape(batch, seq, d)
```

```python
import functools

import jax
import jax.numpy as jnp
import numpy as np
from jax import lax
from jax.experimental import pallas as pl
from jax.experimental.pallas import tpu as pltpu

F32 = jnp.float32
BF16 = jnp.bfloat16
I32 = jnp.int32
U32 = jnp.uint32

CHUNK = 64
LEFT_CHUNKS = 8
REL_CLIP = 128
HEAD_DIM = 128
HEADS_A = 16
HEADS_B = 16
Q_LORA = 1024
KV_LORA = 512
NOPE_DIM = 128
ROPE_DIM = 64
V_DIM = 128
QK_B = NOPE_DIM + ROPE_DIM
ROPE_THETA = 10000.0
MEM_HEADS = 4
MEM_HEAD_DIM = 128
N_EXPERTS = 64
N_GROUPS = 8
TOPK_GROUPS = 4
TOP_K = 8
ROUTED_SCALE = 2.5
EPS = 1e-6

LANES = 128
ATT_BLOCK = 256
EXPERT_ROWS = 256
STEP_ROWS = 2 * EXPERT_ROWS
DMA_UNROLL = 8
COMBINE_GROUP = 16
NEG = -1e30
VMEM_LIMIT = 56 * 1024 * 1024

_NT = (((1,), (1,)), ((), ()))


def _cp(sem, vmem=VMEM_LIMIT):
    return pltpu.CompilerParams(dimension_semantics=sem, vmem_limit_bytes=vmem)


def _rms(x, g):
    return x * lax.rsqrt(jnp.mean(x * x, axis=-1, keepdims=True) + EPS) * g


def _pack_store(ref, first_token, x, scr):
    m, w = x.shape
    pitch = w // (2 * LANES)
    for s in range(pitch):
        base = s * 2 * m
        scr[pl.ds(base, m, stride=2), :] = x[:, (2 * s) * LANES:(2 * s + 1) * LANES]
        scr[pl.ds(base + 1, m, stride=2), :] = x[:, (2 * s + 1) * LANES:(2 * s + 2) * LANES]
        z = scr[pl.ds(base, 2 * m), :].astype(BF16)
        ref[pl.ds(first_token * pitch + s, m, stride=pitch), :] = pltpu.bitcast(z, U32)


def _load_pairs(ref, first_token, m, pitch, s):
    w = ref[pl.ds(first_token * pitch + s, m, stride=pitch), :]
    return pltpu.bitcast(w, BF16).astype(F32)


def _split_pairs(z, scr, region):
    m = z.shape[0] // 2
    base = region * 2 * m
    scr[pl.ds(base, 2 * m), :] = z
    return scr[pl.ds(base, m, stride=2), :], scr[pl.ds(base + 1, m, stride=2), :]


def _load_unpacked(ref, first_token, m, pitch, scr):
    cols = []
    for s in range(pitch):
        cols.extend(_split_pairs(_load_pairs(ref, first_token, m, pitch, s), scr, s))
    return jnp.concatenate(cols, axis=1)


def _norm_kernel(x_ref, g_ref, o_ref):
    o_ref[...] = _rms(x_ref[...], g_ref[...]).astype(o_ref.dtype)


def _norm_bf16(x, g, tm=256):
    n, d = x.shape
    return pl.pallas_call(
        _norm_kernel,
        grid=(n // tm,),
        in_specs=[pl.BlockSpec((tm, d), lambda i: (i, 0)), pl.BlockSpec((1, d), lambda i: (0, 0))],
        out_specs=pl.BlockSpec((tm, d), lambda i: (i, 0)),
        out_shape=jax.ShapeDtypeStruct((n, d), BF16),
        compiler_params=_cp(("parallel",)),
        name="norm_mix",
    )(x, g.reshape(1, d))


def _in_proj_kernel(*refs, head_norm, tn):
    if head_norm:
        h_ref, w_ref, gs_ref, o_ref = refs
    else:
        h_ref, w_ref, o_ref = refs
    acc = lax.dot_general(h_ref[...], w_ref[...].astype(BF16), _NT, preferred_element_type=F32)
    if head_norm:
        for c in range(tn // LANES):
            sl = slice(c * LANES, (c + 1) * LANES)
            o_ref[:, sl] = _rms(acc[:, sl], gs_ref[:, sl]).astype(o_ref.dtype)
    else:
        o_ref[...] = acc.astype(o_ref.dtype)


def _in_proj(h, w_t, col0, ncols, out_dtype, name, gs=None, tm=1024, tn=512):
    n, d = h.shape
    tn = min(tn, ncols)
    j0 = col0 // tn
    in_specs = [pl.BlockSpec((tm, d), lambda i, j: (i, 0)),
                pl.BlockSpec((tn, d), lambda i, j: (j0 + j, 0))]
    args = [h, w_t]
    if gs is not None:
        in_specs.append(pl.BlockSpec((1, tn), lambda i, j: (0, j)))
        args.append(gs.reshape(1, ncols))
    return pl.pallas_call(
        functools.partial(_in_proj_kernel, head_norm=gs is not None, tn=tn),
        grid=(n // tm, ncols // tn),
        in_specs=in_specs,
        out_specs=pl.BlockSpec((tm, tn), lambda i, j: (i, j)),
        out_shape=jax.ShapeDtypeStruct((n, ncols), out_dtype),
        compiler_params=_cp(("parallel", "arbitrary")),
        name=name,
    )(*args)


def _attn_a_kernel(q_ref, k0_ref, k1_ref, k2_ref, v0_ref, v1_ref, v2_ref, b_ref, o_ref, *, heads):
    i = pl.program_id(2)
    k_refs = (k0_ref, k1_ref, k2_ref)
    v_refs = (v0_ref, v1_ref, v2_ref)
    for hh in range(heads):
        sl = slice(hh * HEAD_DIM, (hh + 1) * HEAD_DIM)
        q = q_ref[:, sl]
        s = []
        for d in range(3):
            sd = lax.dot_general(q, k_refs[d][:, sl], _NT, preferred_element_type=F32)
            sd = sd + b_ref[hh, :, d * ATT_BLOCK:(d + 1) * ATT_BLOCK]
            if d > 0:
                sd = jnp.where(i >= d, sd, NEG)
            s.append(sd)
        m = jnp.maximum(jnp.maximum(s[0].max(-1, keepdims=True), s[1].max(-1, keepdims=True)),
                        s[2].max(-1, keepdims=True))
        l = jnp.zeros_like(m)
        o = jnp.zeros((q.shape[0], HEAD_DIM), F32)
        for d in range(3):
            p = jnp.exp(s[d] - m)
            l = l + p.sum(-1, keepdims=True)
            o = o + jnp.dot(p.astype(BF16), v_refs[d][:, sl], preferred_element_type=F32)
        o_ref[:, sl] = (o / l).astype(o_ref.dtype)


def _band_bias(rel_bias):
    blk = ATT_BLOCK
    period = 2 * blk
    heads = rel_bias.shape[0]
    r = np.arange(blk)[:, None]
    c = np.arange(blk)[None, :]
    per_blk = blk // CHUNK
    k = np.arange(period)
    delta = np.where(k < blk, k, k - period)
    tiles = []
    for d in range(3):
        idx = np.clip(blk * d - delta, -(CHUNK - 1), REL_CLIP) + CHUNK - 1
        w = rel_bias[:, idx].astype(F32)
        b = jnp.tile(w, (1, blk))[:, :blk * (period - 1)].reshape(heads, blk, period - 1)[:, :, :blk]
        cdiff = per_blk * d + r // CHUNK - c // CHUNK
        valid = (cdiff >= 0) & (cdiff <= LEFT_CHUNKS)
        tiles.append(jnp.where(valid[None], b, NEG))
    return jnp.concatenate(tiles, axis=-1)


def _attn_a(qk, v, bias, batch, seq, heads_per_step=4):
    n = qk.shape[0]
    nq = seq // ATT_BLOCK
    hw = heads_per_step * HEAD_DIM
    groups = HEADS_A // heads_per_step
    kcol0 = HEADS_A * HEAD_DIM // hw

    def kspec(d, col0):
        return pl.BlockSpec((ATT_BLOCK, hw), lambda g, b, i: (b * nq + jnp.maximum(i - d, 0), col0 + g))

    return pl.pallas_call(
        functools.partial(_attn_a_kernel, heads=heads_per_step),
        grid=(groups, batch, nq),
        in_specs=[pl.BlockSpec((ATT_BLOCK, hw), lambda g, b, i: (b * nq + i, g)),
                  kspec(0, kcol0), kspec(1, kcol0), kspec(2, kcol0),
                  kspec(0, 0), kspec(1, 0), kspec(2, 0),
                  pl.BlockSpec((heads_per_step, ATT_BLOCK, 3 * ATT_BLOCK), lambda g, b, i: (g, 0, 0))],
        out_specs=pl.BlockSpec((ATT_BLOCK, hw), lambda g, b, i: (b * nq + i, g)),
        out_shape=jax.ShapeDtypeStruct((n, HEADS_A * HEAD_DIM), BF16),
        compiler_params=_cp(("parallel", "parallel", "arbitrary")),
        name="attn_band",
    )(qk, qk, qk, qk, v, v, v, bias)


def _rope_pe(pe, g, cos_ref, sa_ref, sb_ref):
    ss = jnp.sum(pe * pe, axis=-1, keepdims=True) * (1.0 / ROPE_DIM)
    pn = pe * lax.rsqrt(ss + EPS) * g
    half = ROPE_DIM // 2
    return (pn * cos_ref[...] + pltpu.roll(pn, LANES - half, 1) * sa_ref[...]
            + pltpu.roll(pn, half, 1) * sb_ref[...])


def _mla_q_kernel(c_ref, gc_ref, w_ref, gq_ref, cos_ref, sa_ref, sb_ref, o_ref, xn_ref, *, heads, scale):
    @pl.when(pl.program_id(1) == 0)
    def _():
        xn_ref[...] = _rms(c_ref[...], gc_ref[...]).astype(BF16)

    acc = jnp.dot(xn_ref[...], w_ref[...].astype(BF16), preferred_element_type=F32)
    for hh in range(heads):
        base = hh * 2 * LANES
        nope = acc[:, base:base + LANES]
        pe = acc[:, base + LANES:base + 2 * LANES]
        o_ref[:, base:base + LANES] = (_rms(nope, gq_ref[:, :LANES]) * scale).astype(o_ref.dtype)
        o_ref[:, base + LANES:base + 2 * LANES] = (
            _rope_pe(pe, gq_ref[:, LANES:], cos_ref, sa_ref, sb_ref) * scale).astype(o_ref.dtype)


def _mla_q(c, g_cq, w_uq_pad, gq, cos_t, sin_a, sin_b, tm=512, heads_per_step=2):
    n = c.shape[0]
    tn = heads_per_step * 2 * LANES
    ncols = w_uq_pad.shape[1]
    row = lambda i, j: (i, 0)
    return pl.pallas_call(
        functools.partial(_mla_q_kernel, heads=heads_per_step, scale=QK_B ** -0.5),
        grid=(n // tm, ncols // tn),
        in_specs=[pl.BlockSpec((tm, Q_LORA), row),
                  pl.BlockSpec((1, Q_LORA), lambda i, j: (0, 0)),
                  pl.BlockSpec((Q_LORA, tn), lambda i, j: (0, j)),
                  pl.BlockSpec((1, 2 * LANES), lambda i, j: (0, 0)),
                  pl.BlockSpec((tm, LANES), row), pl.BlockSpec((tm, LANES), row), pl.BlockSpec((tm, LANES), row)],
        out_specs=pl.BlockSpec((tm, tn), lambda i, j: (i, j)),
        out_shape=jax.ShapeDtypeStruct((n, ncols), BF16),
        scratch_shapes=[pltpu.VMEM((tm, Q_LORA), BF16)],
        compiler_params=_cp(("parallel", "arbitrary")),
        name="mla_q_proj",
    )(c, g_cq.reshape(1, Q_LORA), w_uq_pad, gq, cos_t, sin_a, sin_b)


def _mla_kv_kernel(c_ref, gc_ref, kpe_ref, w_ref, gk_ref, cos_ref, sa_ref, sb_ref, k_ref, v_ref,
                   xn_ref, pe_ref, *, heads):
    @pl.when(pl.program_id(1) == 0)
    def _():
        xn_ref[...] = _rms(c_ref[...], gc_ref[...]).astype(BF16)
        pe_ref[...] = _rope_pe(kpe_ref[...], gk_ref[:, LANES:], cos_ref, sa_ref, sb_ref).astype(BF16)

    acc = jnp.dot(xn_ref[...], w_ref[...].astype(BF16), preferred_element_type=F32)
    for hh in range(heads):
        base = hh * 2 * LANES
        k_ref[:, base:base + LANES] = _rms(acc[:, base:base + LANES], gk_ref[:, :LANES]).astype(k_ref.dtype)
        k_ref[:, base + LANES:base + 2 * LANES] = pe_ref[...]
        v_ref[:, hh * LANES:(hh + 1) * LANES] = acc[:, base + LANES:base + 2 * LANES].astype(v_ref.dtype)


def _mla_kv(c, g_ckv, kpe, w_ukv, gk, cos_t, sin_a, sin_b, tm=512, heads_per_step=2):
    n = c.shape[0]
    tn = heads_per_step * 2 * LANES
    ncols = w_ukv.shape[1]
    cblk = Q_LORA // KV_LORA
    row = lambda i, j: (i, 0)
    return pl.pallas_call(
        functools.partial(_mla_kv_kernel, heads=heads_per_step),
        grid=(n // tm, ncols // tn),
        in_specs=[pl.BlockSpec((tm, KV_LORA), lambda i, j: (i, cblk)),
                  pl.BlockSpec((1, KV_LORA), lambda i, j: (0, 0)),
                  pl.BlockSpec((tm, LANES), row),
                  pl.BlockSpec((KV_LORA, tn), lambda i, j: (0, j)),
                  pl.BlockSpec((1, 2 * LANES), lambda i, j: (0, 0)),
                  pl.BlockSpec((tm, LANES), row), pl.BlockSpec((tm, LANES), row), pl.BlockSpec((tm, LANES), row)],
        out_specs=[pl.BlockSpec((tm, tn), lambda i, j: (i, j)),
                   pl.BlockSpec((tm, tn // 2), lambda i, j: (i, j))],
        out_shape=[jax.ShapeDtypeStruct((n, ncols), BF16),
                   jax.ShapeDtypeStruct((n, ncols // 2), BF16)],
        scratch_shapes=[pltpu.VMEM((tm, KV_LORA), BF16), pltpu.VMEM((tm, LANES), BF16)],
        compiler_params=_cp(("parallel", "arbitrary")),
        name="mla_kv_proj",
    )(c, g_ckv.reshape(1, KV_LORA), kpe, w_ukv, gk, cos_t, sin_a, sin_b)


def _attn_b_kernel(q_ref, k_ref, v_ref, o_ref, *, heads):
    i = pl.program_id(2)
    tq = q_ref.shape[0]
    qw = 2 * LANES
    qs = [q_ref[:, h * qw:(h + 1) * qw] for h in range(heads)]

    def step(kb, carry, masked):
        start = pl.multiple_of(kb * ATT_BLOCK, ATT_BLOCK)
        if masked:
            r = lax.broadcasted_iota(I32, (tq, ATT_BLOCK), 0) // CHUNK
            c = lax.broadcasted_iota(I32, (tq, ATT_BLOCK), 1) // CHUNK
            keep = c <= r
        out = []
        for h in range(heads):
            m, l, acc = carry[h]
            s = lax.dot_general(qs[h], k_ref[pl.ds(start, ATT_BLOCK), h * qw:(h + 1) * qw], _NT,
                                preferred_element_type=F32)
            if masked:
                s = jnp.where(keep, s, NEG)
            m_new = jnp.maximum(m, s.max(-1, keepdims=True))
            alpha = jnp.exp(m - m_new)
            p = jnp.exp(s - m_new)
            l = alpha * l + p.sum(-1, keepdims=True)
            acc = alpha * acc + jnp.dot(p.astype(BF16),
                                        v_ref[pl.ds(start, ATT_BLOCK), h * V_DIM:(h + 1) * V_DIM],
                                        preferred_element_type=F32)
            out.append((m_new, l, acc))
        return tuple(out)

    init = tuple((jnp.full((tq, 1), NEG, F32), jnp.zeros((tq, 1), F32), jnp.zeros((tq, V_DIM), F32))
                 for _ in range(heads))
    carry = lax.fori_loop(0, i, lambda kb, c: step(kb, c, False), init)
    final = step(i, carry, True)
    for h in range(heads):
        _, l, acc = final[h]
        o_ref[:, h * V_DIM:(h + 1) * V_DIM] = (acc / l).astype(o_ref.dtype)


def _attn_b(qf, kf, vb, batch, seq, heads_per_step=4):
    n = qf.shape[0]
    nq = seq // ATT_BLOCK
    qw = heads_per_step * 2 * LANES
    vw = heads_per_step * V_DIM
    return pl.pallas_call(
        functools.partial(_attn_b_kernel, heads=heads_per_step),
        grid=(batch, HEADS_B // heads_per_step, nq),
        in_specs=[pl.BlockSpec((ATT_BLOCK, qw), lambda b, g, i: (b * nq + i, g)),
                  pl.BlockSpec((seq, qw), lambda b, g, i: (b, g)),
                  pl.BlockSpec((seq, vw), lambda b, g, i: (b, g))],
        out_specs=pl.BlockSpec((ATT_BLOCK, vw), lambda b, g, i: (b * nq + i, g)),
        out_shape=jax.ShapeDtypeStruct((n, HEADS_B * V_DIM), BF16),
        compiler_params=_cp(("parallel", "parallel", "arbitrary")),
        name="attn_latent",
    )(qf, kf, vb)


def _out_proj_kernel(oa_ref, ob_ref, w_ref, x_ref, o_ref):
    ka = oa_ref.shape[1]
    acc = jnp.dot(oa_ref[...], w_ref[:ka, :].astype(BF16), preferred_element_type=F32)
    acc = acc + jnp.dot(ob_ref[...], w_ref[ka:, :].astype(BF16), preferred_element_type=F32)
    o_ref[...] = x_ref[...] + acc


def _out_proj(oa, ob, w_o, x, tm=512, tn=512):
    n, ka = oa.shape
    kb = ob.shape[1]
    d = w_o.shape[1]
    return pl.pallas_call(
        _out_proj_kernel,
        grid=(d // tn, n // tm),
        in_specs=[pl.BlockSpec((tm, ka), lambda j, i: (i, 0)),
                  pl.BlockSpec((tm, kb), lambda j, i: (i, 0)),
                  pl.BlockSpec((ka + kb, tn), lambda j, i: (0, j)),
                  pl.BlockSpec((tm, tn), lambda j, i: (i, j))],
        out_specs=pl.BlockSpec((tm, tn), lambda j, i: (i, j)),
        out_shape=jax.ShapeDtypeStruct((n, d), F32),
        compiler_params=_cp(("parallel", "arbitrary")),
        name="out_proj",
    )(oa, ob, w_o, x)


def _normed_proj_kernel(x_ref, g_ref, w_ref, gs_ref, o_ref, *, norm_cols):
    h = _rms(x_ref[...], g_ref[...]).astype(BF16)
    acc = jnp.dot(h, w_ref[...].astype(BF16), preferred_element_type=F32)
    for c in range(acc.shape[1] // LANES):
        sl = slice(c * LANES, (c + 1) * LANES)
        if c * LANES < norm_cols:
            o_ref[:, sl] = _rms(acc[:, sl], gs_ref[:, sl]).astype(o_ref.dtype)
        else:
            o_ref[:, sl] = acc[:, sl].astype(o_ref.dtype)


def _normed_proj(x, g, w, gs, norm_cols, name, tm=256):
    n, d = x.shape
    nc = w.shape[1]
    return pl.pallas_call(
        functools.partial(_normed_proj_kernel, norm_cols=norm_cols),
        grid=(n // tm,),
        in_specs=[pl.BlockSpec((tm, d), lambda i: (i, 0)),
                  pl.BlockSpec((1, d), lambda i: (0, 0)),
                  pl.BlockSpec((d, nc), lambda i: (0, 0)),
                  pl.BlockSpec((1, nc), lambda i: (0, 0))],
        out_specs=pl.BlockSpec((tm, nc), lambda i: (i, 0)),
        out_shape=jax.ShapeDtypeStruct((n, nc), BF16),
        compiler_params=_cp(("parallel",)),
        name=name,
    )(x, g.reshape(1, d), w, gs.reshape(1, nc))


def _cross_out_kernel(q_ref, k_ref, v_ref, w_ref, x_ref, o_ref):
    outs = []
    for hh in range(MEM_HEADS):
        sl = slice(hh * MEM_HEAD_DIM, (hh + 1) * MEM_HEAD_DIM)
        s = lax.dot_general(q_ref[:, sl], k_ref[:, sl], _NT, preferred_element_type=F32)
        p = jnp.exp(s - s.max(-1, keepdims=True))
        o = jnp.dot(p.astype(BF16), v_ref[:, sl], preferred_element_type=F32)
        outs.append((o / p.sum(-1, keepdims=True)).astype(BF16))
    acc = jnp.zeros(x_ref.shape, F32)
    for hh in range(MEM_HEADS):
        sl = slice(hh * MEM_HEAD_DIM, (hh + 1) * MEM_HEAD_DIM)
        acc = acc + jnp.dot(outs[hh], w_ref[sl, :].astype(BF16), preferred_element_type=F32)
    o_ref[...] = x_ref[...] + acc


def _cross_out(qx, kx, vx, w_xo, x, batch, seq, mem_tokens, tm=256):
    n, d = x.shape
    per_b = seq // tm
    hw = MEM_HEADS * MEM_HEAD_DIM
    return pl.pallas_call(
        _cross_out_kernel,
        grid=(batch, per_b),
        in_specs=[pl.BlockSpec((tm, hw), lambda b, i: (b * per_b + i, 0)),
                  pl.BlockSpec((mem_tokens, hw), lambda b, i: (b, 0)),
                  pl.BlockSpec((mem_tokens, hw), lambda b, i: (b, 1)),
                  pl.BlockSpec((hw, d), lambda b, i: (0, 0)),
                  pl.BlockSpec((tm, d), lambda b, i: (b * per_b + i, 0))],
        out_specs=pl.BlockSpec((tm, d), lambda b, i: (b * per_b + i, 0)),
        out_shape=jax.ShapeDtypeStruct((n, d), F32),
        compiler_params=_cp(("parallel", "arbitrary")),
        name="cross_attn_out",
    )(qx, kx, vx, w_xo, x)


def _router_kernel(x_ref, g_ref, wr_ref, b_ref, tri_ref, hp_ref, eid_ref, gate_ref, rank_ref, cnt_ref,
                   run_ref, scr_ref):
    t = pl.program_id(0)

    @pl.when(t == 0)
    def _():
        run_ref[...] = jnp.zeros_like(run_ref)

    h = _rms(x_ref[...], g_ref[...])
    _pack_store(hp_ref, 0, h, scr_ref)
    tm = h.shape[0]
    per_g = N_EXPERTS // N_GROUPS

    logits = lax.dot_general(wr_ref[...], h, _NT, precision=lax.Precision.HIGHEST,
                             preferred_element_type=F32)
    scores = 1.0 / (1.0 + jnp.exp(-logits))
    choice = scores + b_ref[...]

    sub = lax.broadcasted_iota(I32, (per_g, tm), 0).astype(F32)
    rows = []
    for g in range(N_GROUPS):
        c = choice[g * per_g:(g + 1) * per_g, :]
        m1 = c.max(0, keepdims=True)
        first = jnp.where(c == m1, sub, float(per_g)).min(0, keepdims=True)
        m2 = jnp.where(sub == first, -jnp.inf, c).max(0, keepdims=True)
        rows.append(m1 + m2)
    gs = jnp.concatenate(rows, axis=0)

    gsub = lax.broadcasted_iota(I32, (N_GROUPS, tm), 0).astype(F32)
    beaten = jnp.zeros((N_GROUPS, tm), F32)
    for g2 in range(N_GROUPS):
        row = gs[g2:g2 + 1, :]
        wins = (row > gs) | ((row == gs) & (gsub > float(g2)))
        beaten = beaten + jnp.where(wins, 1.0, 0.0)
    g_ok = jnp.where(beaten < TOPK_GROUPS, 1.0, 0.0)
    e_ok = jnp.concatenate(
        [jnp.broadcast_to(g_ok[g:g + 1, :], (per_g, tm)) for g in range(N_GROUPS)], axis=0)

    eiota = lax.broadcasted_iota(I32, (N_EXPERTS, tm), 0).astype(F32)
    masked = jnp.where(e_ok > 0.5, choice, -jnp.inf)
    chosen = jnp.zeros((N_EXPERTS, tm), F32)
    eids, ws = [], []
    for _ in range(TOP_K):
        m = masked.max(0, keepdims=True)
        idx = jnp.where(masked == m, eiota, float(N_EXPERTS)).min(0, keepdims=True)
        sel = eiota == idx
        ws.append(jnp.where(sel, scores, 0.0).sum(0, keepdims=True))
        masked = jnp.where(sel, -jnp.inf, masked)
        chosen = jnp.where(sel, 1.0, chosen)
        eids.append(idx)
    wsum = ws[0]
    for w in ws[1:]:
        wsum = wsum + w
    denom = wsum + 1e-20

    pos = jnp.dot(chosen.astype(BF16), tri_ref[...], preferred_element_type=F32) + run_ref[:, 0:1]
    run_ref[...] = run_ref[...] + chosen.sum(1, keepdims=True)
    cnt_ref[...] = run_ref[...].astype(I32)

    for r in range(TOP_K):
        eid_ref[r:r + 1, :] = eids[r].astype(I32)
        gate_ref[r:r + 1, :] = ws[r] / denom * ROUTED_SCALE
        rank_ref[r:r + 1, :] = jnp.where(eiota == eids[r], pos, 0.0).sum(0, keepdims=True).astype(I32)


def _router(x, g, w_router, router_bias, tm=512):
    n, d = x.shape
    ne = w_router.shape[1]
    tri = (jnp.arange(tm)[:, None] < jnp.arange(tm)[None, :]).astype(BF16)
    row8 = lambda i: (0, i)
    pitch = d // 2 // LANES
    return pl.pallas_call(
        _router_kernel,
        grid=(n // tm,),
        in_specs=[pl.BlockSpec((tm, d), lambda i: (i, 0)),
                  pl.BlockSpec((1, d), lambda i: (0, 0)),
                  pl.BlockSpec((ne, d), lambda i: (0, 0)),
                  pl.BlockSpec((ne, 1), lambda i: (0, 0)),
                  pl.BlockSpec((tm, tm), lambda i: (0, 0))],
        out_specs=[pl.BlockSpec((tm * pitch, LANES), lambda i: (i, 0)),
                   pl.BlockSpec((TOP_K, tm), row8), pl.BlockSpec((TOP_K, tm), row8),
                   pl.BlockSpec((TOP_K, tm), row8),
                   pl.BlockSpec((ne, LANES), lambda i: (0, 0))],
        out_shape=[jax.ShapeDtypeStruct((n * pitch, LANES), U32),
                   jax.ShapeDtypeStruct((TOP_K, n), I32), jax.ShapeDtypeStruct((TOP_K, n), F32),
                   jax.ShapeDtypeStruct((TOP_K, n), I32),
                   jax.ShapeDtypeStruct((ne, LANES), I32)],
        scratch_shapes=[pltpu.VMEM((ne, LANES), F32), pltpu.VMEM((2 * tm * pitch, LANES), F32)],
        compiler_params=_cp(("arbitrary",)),
        name="router",
    )(x, g.reshape(1, d), w_router.T, router_bias.reshape(ne, 1), tri)


def _dispatch_kernel(cnt_ref, start_ref, dest_ref, hp_ref, xs_ref, sem, *, tm, pitch):
    t = pl.program_id(0)
    pairs = TOP_K * tm

    def row_copy(r, slot):
        src = hp_ref.at[pl.ds(pl.multiple_of(r * pitch, pitch), pitch), :]
        dst = xs_ref.at[pl.ds(pl.multiple_of(slot * pitch, pitch), pitch), :]
        return pltpu.make_async_copy(src, dst, sem)

    def issue(g, c):
        for u in range(DMA_UNROLL):
            p = g * DMA_UNROLL + u
            row_copy(p & (tm - 1), dest_ref[0, 0, p]).start(priority=u % 2)
        return c

    lax.fori_loop(0, pairs // DMA_UNROLL, issue, 0)
    for _ in range(TOP_K):
        pltpu.make_async_copy(hp_ref, xs_ref.at[pl.ds(0, tm * pitch), :], sem).wait()

    @pl.when(t == pl.num_programs(0) - 1)
    def _():
        def per_expert(e, c):
            used = cnt_ref[e]
            first = start_ref[e] * STEP_ROWS + used
            npad = (-used) & (EXPERT_ROWS - 1)

            def pad_issue(p, c2):
                row_copy(0, first + p).start()
                return c2

            def pad_drain(p, c2):
                row_copy(0, 0).wait()
                return c2

            lax.fori_loop(0, npad, pad_issue, 0)
            lax.fori_loop(0, npad, pad_drain, 0)
            return c

        lax.fori_loop(0, N_EXPERTS, per_expert, 0)


def _dispatch(hp, dest, counts, step_start, p_rows, pitch, tm=256):
    n = hp.shape[0] // pitch
    tiles = n // tm
    dest_t = dest.reshape(TOP_K, tiles, tm).transpose(1, 0, 2).reshape(tiles, 1, TOP_K * tm)
    grid_spec = pltpu.PrefetchScalarGridSpec(
        num_scalar_prefetch=2,
        grid=(tiles,),
        in_specs=[pl.BlockSpec((1, 1, TOP_K * tm), lambda i, *_: (i, 0, 0), memory_space=pltpu.SMEM),
                  pl.BlockSpec((tm * pitch, LANES), lambda i, *_: (i, 0))],
        out_specs=pl.BlockSpec(memory_space=pl.ANY),
        scratch_shapes=[pltpu.SemaphoreType.DMA(())],
    )
    return pl.pallas_call(
        functools.partial(_dispatch_kernel, tm=tm, pitch=pitch),
        grid_spec=grid_spec,
        out_shape=jax.ShapeDtypeStruct((p_rows * pitch, LANES), U32),
        compiler_params=_cp(("arbitrary",)),
        name="moe_dispatch",
    )(counts, step_start, dest_t, hp)


def _silu_mul(g, u):
    return g / (1.0 + jnp.exp(-g)) * u


def _expert_up_kernel(se_ref, nu_ref, nsub_ref, xs_ref, wg_ref, wu_ref, o_ref, scr_ref):
    b = pl.program_id(0)
    pitch = xs_ref.shape[0] // STEP_ROWS

    def sub_block(j):
        x = _load_unpacked(xs_ref, j * EXPERT_ROWS, EXPERT_ROWS, pitch, scr_ref)
        g = jnp.dot(x, wg_ref[0], preferred_element_type=F32)
        u = jnp.dot(x, wu_ref[0], preferred_element_type=F32)
        o_ref[j * EXPERT_ROWS:(j + 1) * EXPERT_ROWS, :] = _silu_mul(g, u).astype(o_ref.dtype)

    for j in range(STEP_ROWS // EXPERT_ROWS):
        pl.when(nsub_ref[b] > j)(functools.partial(sub_block, j))


def _expert_specs():
    blk = lambda b, se, nu, ns: (jnp.minimum(b, nu[0] - 1), 0)
    wsel = lambda b, se, nu, ns: (se[jnp.minimum(b, nu[0] - 1)], 0, 0)
    return blk, wsel


def _expert_up(xs, w_gate, w_up, step_e, n_used, nsub, n_steps, pitch):
    _, d, ff = w_gate.shape
    blk, wsel = _expert_specs()
    grid_spec = pltpu.PrefetchScalarGridSpec(
        num_scalar_prefetch=3,
        grid=(n_steps,),
        in_specs=[pl.BlockSpec((STEP_ROWS * pitch, LANES), blk),
                  pl.BlockSpec((1, d, ff), wsel), pl.BlockSpec((1, d, ff), wsel)],
        out_specs=pl.BlockSpec((STEP_ROWS, ff), blk),
        scratch_shapes=[pltpu.VMEM((2 * EXPERT_ROWS * pitch, LANES), F32)],
    )
    return pl.pallas_call(
        _expert_up_kernel,
        grid_spec=grid_spec,
        out_shape=jax.ShapeDtypeStruct((n_steps * STEP_ROWS, ff), BF16),
        compiler_params=_cp(("arbitrary",)),
        name="expert_up",
    )(step_e, n_used, nsub, xs, w_gate, w_up)


def _expert_down_kernel(se_ref, nu_ref, nsub_ref, h_ref, wd_ref, o_ref, scr_ref):
    b = pl.program_id(0)

    def sub_block(j):
        y = jnp.dot(h_ref[j * EXPERT_ROWS:(j + 1) * EXPERT_ROWS, :], wd_ref[0].astype(BF16),
                    preferred_element_type=F32)
        _pack_store(o_ref, j * EXPERT_ROWS, y, scr_ref)

    for j in range(STEP_ROWS // EXPERT_ROWS):
        pl.when(nsub_ref[b] > j)(functools.partial(sub_block, j))


def _expert_down(hs, w_down, step_e, n_used, nsub, n_steps, pitch):
    p_rows, ff = hs.shape
    d = w_down.shape[2]
    blk, wsel = _expert_specs()
    grid_spec = pltpu.PrefetchScalarGridSpec(
        num_scalar_prefetch=3,
        grid=(n_steps,),
        in_specs=[pl.BlockSpec((STEP_ROWS, ff), blk), pl.BlockSpec((1, ff, d), wsel)],
        out_specs=pl.BlockSpec((STEP_ROWS * pitch, LANES), blk),
        scratch_shapes=[pltpu.VMEM((2 * EXPERT_ROWS * pitch, LANES), F32)],
    )
    return pl.pallas_call(
        _expert_down_kernel,
        grid_spec=grid_spec,
        out_shape=jax.ShapeDtypeStruct((p_rows * pitch, LANES), U32),
        compiler_params=_cp(("arbitrary",)),
        name="expert_down",
    )(step_e, n_used, nsub, hs, w_down)


def _shared_up_kernel(hp_ref, wg_ref, wu_ref, o_ref, scr_ref):
    tm = o_ref.shape[0]
    x = _load_unpacked(hp_ref, 0, tm, hp_ref.shape[0] // tm, scr_ref)
    g = jnp.dot(x, wg_ref[...], preferred_element_type=F32)
    u = jnp.dot(x, wu_ref[...], preferred_element_type=F32)
    o_ref[...] = _silu_mul(g, u).astype(o_ref.dtype)


def _shared_up(hp, w_gate, w_up, pitch, tm=256):
    n = hp.shape[0] // pitch
    d, ff = w_gate.shape
    return pl.pallas_call(
        _shared_up_kernel,
        grid=(n // tm,),
        in_specs=[pl.BlockSpec((tm * pitch, LANES), lambda i: (i, 0)),
                  pl.BlockSpec((d, ff), lambda i: (0, 0)), pl.BlockSpec((d, ff), lambda i: (0, 0))],
        out_specs=pl.BlockSpec((tm, ff), lambda i: (i, 0)),
        out_shape=jax.ShapeDtypeStruct((n, ff), BF16),
        scratch_shapes=[pltpu.VMEM((2 * tm * pitch, LANES), F32)],
        compiler_params=_cp(("parallel",)),
        name="shared_up",
    )(hp, w_gate, w_up)


def _shared_down_kernel(t_ref, w_ref, x_ref, o_ref):
    o_ref[...] = x_ref[...] + jnp.dot(t_ref[...], w_ref[...].astype(BF16), preferred_element_type=F32)


def _shared_down(t, w_down, x, tm=256):
    n, ff = t.shape
    d = w_down.shape[1]
    return pl.pallas_call(
        _shared_down_kernel,
        grid=(n // tm,),
        in_specs=[pl.BlockSpec((tm, ff), lambda i: (i, 0)),
                  pl.BlockSpec((ff, d), lambda i: (0, 0)),
                  pl.BlockSpec((tm, d), lambda i: (i, 0))],
        out_specs=pl.BlockSpec((tm, d), lambda i: (i, 0)),
        out_shape=jax.ShapeDtypeStruct((n, d), F32),
        compiler_params=_cp(("parallel",)),
        name="shared_down",
    )(t, w_down, x)


def _combine_kernel(dfirst_ref, dnext_ref, gate_ref, base_ref, ys_ref, o_ref, buf_ref, scr_ref, sem, *,
                    tm, pitch):
    t = pl.program_id(0)
    pairs = TOP_K * tm
    slot_rows = pairs * pitch

    def issue_tile(dref, slot):
        def row_copy(p):
            src = ys_ref.at[pl.ds(pl.multiple_of(dref[0, 0, p] * pitch, pitch), pitch), :]
            dst = buf_ref.at[pl.ds(pl.multiple_of(slot * slot_rows + p * pitch, pitch), pitch), :]
            return pltpu.make_async_copy(src, dst, sem.at[slot])

        def body(g, c):
            for u in range(DMA_UNROLL):
                row_copy(g * DMA_UNROLL + u).start(priority=u % 2)
            return c

        lax.fori_loop(0, pairs // DMA_UNROLL, body, 0)

    @pl.when(t == 0)
    def _():
        issue_tile(dfirst_ref, 0)

    @pl.when(t + 1 < pl.num_programs(0))
    def _():
        issue_tile(dnext_ref, (t + 1) % 2)

    cur = t % 2
    for _ in range(TOP_K):
        pltpu.make_async_copy(ys_ref.at[pl.ds(0, tm * pitch), :], buf_ref.at[pl.ds(0, tm * pitch), :],
                              sem.at[cur]).wait()

    for grp in range(tm // COMBINE_GROUP):
        r0 = grp * COMBINE_GROUP
        rows = slice(r0, r0 + COMBINE_GROUP)
        gates = [jnp.broadcast_to(gate_ref[2 * r0:2 * (r0 + COMBINE_GROUP), k:k + 1], (2 * COMBINE_GROUP, LANES))
                 for k in range(TOP_K)]
        for s in range(pitch):
            acc = None
            for k in range(TOP_K):
                term = gates[k] * _load_pairs(buf_ref, cur * pairs + k * tm + r0, COMBINE_GROUP, pitch, s)
                acc = term if acc is None else acc + term
            even, odd = _split_pairs(acc, scr_ref, grp * pitch + s)
            c0 = slice((2 * s) * LANES, (2 * s + 1) * LANES)
            c1 = slice((2 * s + 1) * LANES, (2 * s + 2) * LANES)
            o_ref[rows, c0] = base_ref[rows, c0] + even
            o_ref[rows, c1] = base_ref[rows, c1] + odd


def _combine(dest, gate_rows, base, ys, pitch, tm=128):
    n, d = base.shape
    tiles = n // tm
    dest_t = dest.reshape(TOP_K, tiles, tm).transpose(1, 0, 2).reshape(tiles, 1, TOP_K * tm)
    dspec = lambda imap: pl.BlockSpec((1, 1, TOP_K * tm), imap, memory_space=pltpu.SMEM)
    return pl.pallas_call(
        functools.partial(_combine_kernel, tm=tm, pitch=pitch),
        grid=(tiles,),
        in_specs=[dspec(lambda i: (0, 0, 0)),
                  dspec(lambda i: (jnp.minimum(i + 1, tiles - 1), 0, 0)),
                  pl.BlockSpec((2 * tm, TOP_K), lambda i: (i, 0)),
                  pl.BlockSpec((tm, d), lambda i: (i, 0)),
                  pl.BlockSpec(memory_space=pl.ANY)],
        out_specs=pl.BlockSpec((tm, d), lambda i: (i, 0)),
        out_shape=jax.ShapeDtypeStruct((n, d), F32),
        scratch_shapes=[pltpu.VMEM((2 * TOP_K * tm * pitch, LANES), U32),
                        pltpu.VMEM((2 * tm * pitch, LANES), F32),
                        pltpu.SemaphoreType.DMA((2,))],
        compiler_params=_cp(("arbitrary",)),
        name="moe_combine",
    )(dest_t, dest_t, gate_rows, base, ys)


def _rope_tables(positions):
    half = ROPE_DIM // 2
    inv_freq = ROPE_THETA ** (-jnp.arange(0, ROPE_DIM, 2, dtype=F32) / ROPE_DIM)
    ang = positions.reshape(-1).astype(F32)[:, None] * inv_freq
    cos, sin = jnp.cos(ang), jnp.sin(ang)
    n = ang.shape[0]
    z = lambda w: jnp.zeros((n, w), F32)
    cos_t = jnp.concatenate([cos, cos, z(LANES - ROPE_DIM)], axis=1)
    sin_a = jnp.concatenate([-sin, z(LANES - half)], axis=1)
    sin_b = jnp.concatenate([z(half), sin, z(LANES - ROPE_DIM)], axis=1)
    return cos_t, sin_a, sin_b


def _pad_lanes(v, width):
    return jnp.concatenate([v, jnp.zeros((width - v.shape[0],), v.dtype)])


def kernel(x, mem, positions, g_mix, w_in, g_qa, g_ka, rel_bias, g_cq, w_uq, g_ckv, w_ukv, g_qb, g_kb, w_o, g_cross, g_mem, w_xq, w_xkv, g_qx, g_kx, w_xo, g_ffn, w_router, router_bias, w_sh_gate, w_sh_up, w_sh_down, w_ex_gate, w_ex_up, w_ex_down):
    batch, seq, d = x.shape
    n = batch * seq
    mem_tokens = mem.shape[1]
    width_a = HEADS_A * HEAD_DIM
    x2d = x.reshape(n, d)

    h = _norm_bf16(x2d, g_mix)
    gs_qk = jnp.concatenate([jnp.tile(g_qa * HEAD_DIM ** -0.5, HEADS_A), jnp.tile(g_ka, HEADS_A)])
    w_in_t = w_in.T
    qk = _in_proj(h, w_in_t, 0, 2 * width_a, BF16, "in_proj_qk", gs=gs_qk)
    v_a = _in_proj(h, w_in_t, 2 * width_a, width_a, BF16, "in_proj_v")
    c = _in_proj(h, w_in_t, 3 * width_a, Q_LORA + KV_LORA, F32, "in_proj_lora")
    w_kpe = jnp.pad(w_in_t[3 * width_a + Q_LORA + KV_LORA:], ((0, LANES - ROPE_DIM), (0, 0)))
    kpe = _in_proj(h, w_kpe, 0, LANES, F32, "in_proj_kpe")

    o_a = _attn_a(qk, v_a, _band_bias(rel_bias), batch, seq)

    cos_t, sin_a, sin_b = _rope_tables(positions)
    w_uq_pad = jnp.pad(w_uq.reshape(Q_LORA, HEADS_B, QK_B),
                       ((0, 0), (0, 0), (0, 2 * LANES - QK_B))).reshape(Q_LORA, HEADS_B * 2 * LANES)
    gq = _pad_lanes(g_qb, 2 * LANES).reshape(1, 2 * LANES)
    gk = _pad_lanes(g_kb, 2 * LANES).reshape(1, 2 * LANES)
    qf = _mla_q(c, g_cq, w_uq_pad, gq, cos_t, sin_a, sin_b)
    kf, v_b = _mla_kv(c, g_ckv, kpe, w_ukv, gk, cos_t, sin_a, sin_b)
    o_b = _attn_b(qf, kf, v_b, batch, seq)

    x1 = _out_proj(o_a, o_b, w_o, x2d)

    hw = MEM_HEADS * MEM_HEAD_DIM
    qx = _normed_proj(x1, g_cross, w_xq, jnp.tile(g_qx * MEM_HEAD_DIM ** -0.5, MEM_HEADS), hw, "cross_q")
    kvx = _normed_proj(mem.reshape(batch * mem_tokens, d), g_mem, w_xkv,
                       jnp.concatenate([jnp.tile(g_kx, MEM_HEADS), jnp.ones((hw,), F32)]), hw, "cross_kv")
    x2 = _cross_out(qx, kvx, kvx, w_xo, x1, batch, seq, mem_tokens)

    hp, eid, gate, rank, cnt = _router(x2, g_ffn, w_router, router_bias)
    counts = cnt[:, 0]
    nstep = (counts + STEP_ROWS - 1) // STEP_ROWS
    step_end = jnp.cumsum(nstep).astype(I32)
    step_start = step_end - nstep
    n_steps = n * TOP_K // STEP_ROWS + N_EXPERTS
    experts = jnp.arange(N_EXPERTS, dtype=I32)
    start_of = jnp.sum(jnp.where(eid[:, :, None] == experts, step_start, 0), axis=-1)
    dest = start_of * STEP_ROWS + rank
    steps = jnp.arange(n_steps, dtype=I32)
    step_e = jnp.minimum(jnp.sum((step_end[None, :] <= steps[:, None]).astype(I32), axis=1), N_EXPERTS - 1)
    mine = step_e[:, None] == experts[None, :]
    rows_left = jnp.sum(jnp.where(mine, counts[None, :] - STEP_ROWS * (steps[:, None] - step_start[None, :]), 0),
                        axis=1)
    rows_here = jnp.where(steps < step_end[-1], jnp.clip(rows_left, 0, STEP_ROWS), 0)
    nsub = ((rows_here + EXPERT_ROWS - 1) // EXPERT_ROWS).astype(I32)
    n_used = step_end[-1:]
    pitch = d // 2 // LANES

    xs = _dispatch(hp, dest, counts, step_start, n_steps * STEP_ROWS, pitch)
    hs = _expert_up(xs, w_ex_gate, w_ex_up, step_e, n_used, nsub, n_steps, pitch)
    ys = _expert_down(hs, w_ex_down, step_e, n_used, nsub, n_steps, pitch)

    base = _shared_down(_shared_up(hp, w_sh_gate, w_sh_up, pitch), w_sh_down, x2)
    out = _combine(dest, jnp.repeat(gate.T, 2, axis=0), base, ys, pitch)
    return out.reshape(batch, seq, d)
```

```python
import functools

import jax
import jax.numpy as jnp
import numpy as np
from jax import lax
from jax.experimental import pallas as pl
from jax.experimental.pallas import tpu as pltpu

F32 = jnp.float32
BF16 = jnp.bfloat16
I32 = jnp.int32
U32 = jnp.uint32

CHUNK = 64
LEFT_CHUNKS = 8
REL_CLIP = 128
HEAD_DIM = 128
HEADS_A = 16
HEADS_B = 16
Q_LORA = 1024
KV_LORA = 512
NOPE_DIM = 128
ROPE_DIM = 64
V_DIM = 128
QK_B = NOPE_DIM + ROPE_DIM
ROPE_THETA = 10000.0
MEM_HEADS = 4
MEM_HEAD_DIM = 128
N_EXPERTS = 64
N_GROUPS = 8
TOPK_GROUPS = 4
TOP_K = 8
ROUTED_SCALE = 2.5
EPS = 1e-6

LANES = 128
ATT_BLOCK = 256
EXPERT_ROWS = 256
STEP_ROWS = 2 * EXPERT_ROWS
DMA_UNROLL = 8
COMBINE_GROUP = 16
NEG = -1e30
VMEM_LIMIT = 56 * 1024 * 1024

_NT = (((1,), (1,)), ((), ()))


def _cp(sem, vmem=VMEM_LIMIT):
    return pltpu.CompilerParams(dimension_semantics=sem, vmem_limit_bytes=vmem)


def _rms(x, g):
    return x * lax.rsqrt(jnp.mean(x * x, axis=-1, keepdims=True) + EPS) * g


def _pack_store(ref, first_token, x, scr):
    m, w = x.shape
    pitch = w // (2 * LANES)
    for s in range(pitch):
        base = s * 2 * m
        scr[pl.ds(base, m, stride=2), :] = x[:, (2 * s) * LANES:(2 * s + 1) * LANES]
        scr[pl.ds(base + 1, m, stride=2), :] = x[:, (2 * s + 1) * LANES:(2 * s + 2) * LANES]
        z = scr[pl.ds(base, 2 * m), :].astype(BF16)
        ref[pl.ds(first_token * pitch + s, m, stride=pitch), :] = pltpu.bitcast(z, U32)


def _load_pairs(ref, first_token, m, pitch, s):
    w = ref[pl.ds(first_token * pitch + s, m, stride=pitch), :]
    return pltpu.bitcast(w, BF16).astype(F32)


def _split_pairs(z, scr, region):
    m = z.shape[0] // 2
    base = region * 2 * m
    scr[pl.ds(base, 2 * m), :] = z
    return scr[pl.ds(base, m, stride=2), :], scr[pl.ds(base + 1, m, stride=2), :]


def _load_unpacked(ref, first_token, m, pitch, scr):
    cols = []
    for s in range(pitch):
        cols.extend(_split_pairs(_load_pairs(ref, first_token, m, pitch, s), scr, s))
    return jnp.concatenate(cols, axis=1)


def _norm_kernel(x_ref, g_ref, o_ref):
    o_ref[...] = _rms(x_ref[...], g_ref[...]).astype(o_ref.dtype)


def _norm_bf16(x, g, tm=256):
    n, d = x.shape
    return pl.pallas_call(
        _norm_kernel,
        grid=(n // tm,),
        in_specs=[pl.BlockSpec((tm, d), lambda i: (i, 0)), pl.BlockSpec((1, d), lambda i: (0, 0))],
        out_specs=pl.BlockSpec((tm, d), lambda i: (i, 0)),
        out_shape=jax.ShapeDtypeStruct((n, d), BF16),
        compiler_params=_cp(("parallel",)),
        name="norm_mix",
    )(x, g.reshape(1, d))


def _in_proj_kernel(*refs, head_norm, tn):
    if head_norm:
        h_ref, w_ref, gs_ref, o_ref = refs
    else:
        h_ref, w_ref, o_ref = refs
    acc = lax.dot_general(h_ref[...], w_ref[...].astype(BF16), _NT, preferred_element_type=F32)
    if head_norm:
        for c in range(tn // LANES):
            sl = slice(c * LANES, (c + 1) * LANES)
            o_ref[:, sl] = _rms(acc[:, sl], gs_ref[:, sl]).astype(o_ref.dtype)
    else:
        o_ref[...] = acc.astype(o_ref.dtype)


def _in_proj(h, w_t, col0, ncols, out_dtype, name, gs=None, tm=1024, tn=512):
    n, d = h.shape
    tn = min(tn, ncols)
    j0 = col0 // tn
    in_specs = [pl.BlockSpec((tm, d), lambda i, j: (i, 0)),
                pl.BlockSpec((tn, d), lambda i, j: (j0 + j, 0))]
    args = [h, w_t]
    if gs is not None:
        in_specs.append(pl.BlockSpec((1, tn), lambda i, j: (0, j)))
        args.append(gs.reshape(1, ncols))
    return pl.pallas_call(
        functools.partial(_in_proj_kernel, head_norm=gs is not None, tn=tn),
        grid=(n // tm, ncols // tn),
        in_specs=in_specs,
        out_specs=pl.BlockSpec((tm, tn), lambda i, j: (i, j)),
        out_shape=jax.ShapeDtypeStruct((n, ncols), out_dtype),
        compiler_params=_cp(("parallel", "arbitrary")),
        name=name,
    )(*args)


def _attn_a_kernel(q_ref, k0_ref, k1_ref, k2_ref, v0_ref, v1_ref, v2_ref, b_ref, o_ref, *, heads):
    i = pl.program_id(2)
    k_refs = (k0_ref, k1_ref, k2_ref)
    v_refs = (v0_ref, v1_ref, v2_ref)
    for hh in range(heads):
        sl = slice(hh * HEAD_DIM, (hh + 1) * HEAD_DIM)
        q = q_ref[:, sl]
        s = []
        for d in range(3):
            sd = lax.dot_general(q, k_refs[d][:, sl], _NT, preferred_element_type=F32)
            sd = sd + b_ref[hh, :, d * ATT_BLOCK:(d + 1) * ATT_BLOCK]
            if d > 0:
                sd = jnp.where(i >= d, sd, NEG)
            s.append(sd)
        m = jnp.maximum(jnp.maximum(s[0].max(-1, keepdims=True), s[1].max(-1, keepdims=True)),
                        s[2].max(-1, keepdims=True))
        l = jnp.zeros_like(m)
        o = jnp.zeros((q.shape[0], HEAD_DIM), F32)
        for d in range(3):
            p = jnp.exp(s[d] - m)
            l = l + p.sum(-1, keepdims=True)
            o = o + jnp.dot(p.astype(BF16), v_refs[d][:, sl], preferred_element_type=F32)
        o_ref[:, sl] = (o / l).astype(o_ref.dtype)


def _band_bias(rel_bias):
    blk = ATT_BLOCK
    period = 2 * blk
    heads = rel_bias.shape[0]
    r = np.arange(blk)[:, None]
    c = np.arange(blk)[None, :]
    per_blk = blk // CHUNK
    k = np.arange(period)
    delta = np.where(k < blk, k, k - period)
    tiles = []
    for d in range(3):
        idx = np.clip(blk * d - delta, -(CHUNK - 1), REL_CLIP) + CHUNK - 1
        w = rel_bias[:, idx].astype(F32)
        b = jnp.tile(w, (1, blk))[:, :blk * (period - 1)].reshape(heads, blk, period - 1)[:, :, :blk]
        cdiff = per_blk * d + r // CHUNK - c // CHUNK
        valid = (cdiff >= 0) & (cdiff <= LEFT_CHUNKS)
        tiles.append(jnp.where(valid[None], b, NEG))
    return jnp.concatenate(tiles, axis=-1)


def _attn_a(qk, v, bias, batch, seq, heads_per_step=4):
    n = qk.shape[0]
    nq = seq // ATT_BLOCK
    hw = heads_per_step * HEAD_DIM
    groups = HEADS_A // heads_per_step
    kcol0 = HEADS_A * HEAD_DIM // hw

    def kspec(d, col0):
        return pl.BlockSpec((ATT_BLOCK, hw), lambda g, b, i: (b * nq + jnp.maximum(i - d, 0), col0 + g))

    return pl.pallas_call(
        functools.partial(_attn_a_kernel, heads=heads_per_step),
        grid=(groups, batch, nq),
        in_specs=[pl.BlockSpec((ATT_BLOCK, hw), lambda g, b, i: (b * nq + i, g)),
                  kspec(0, kcol0), kspec(1, kcol0), kspec(2, kcol0),
                  kspec(0, 0), kspec(1, 0), kspec(2, 0),
                  pl.BlockSpec((heads_per_step, ATT_BLOCK, 3 * ATT_BLOCK), lambda g, b, i: (g, 0, 0))],
        out_specs=pl.BlockSpec((ATT_BLOCK, hw), lambda g, b, i: (b * nq + i, g)),
        out_shape=jax.ShapeDtypeStruct((n, HEADS_A * HEAD_DIM), BF16),
        compiler_params=_cp(("parallel", "parallel", "arbitrary")),
        name="attn_band",
    )(qk, qk, qk, qk, v, v, v, bias)


def _rope_pe(pe, g, cos_ref, sa_ref, sb_ref):
    ss = jnp.sum(pe * pe, axis=-1, keepdims=True) * (1.0 / ROPE_DIM)
    pn = pe * lax.rsqrt(ss + EPS) * g
    half = ROPE_DIM // 2
    return (pn * cos_ref[...] + pltpu.roll(pn, LANES - half, 1) * sa_ref[...]
            + pltpu.roll(pn, half, 1) * sb_ref[...])


def _mla_q_kernel(c_ref, gc_ref, w_ref, gq_ref, cos_ref, sa_ref, sb_ref, o_ref, xn_ref, *, heads, scale):
    @pl.when(pl.program_id(1) == 0)
    def _():
        xn_ref[...] = _rms(c_ref[...], gc_ref[...]).astype(BF16)

    acc = jnp.dot(xn_ref[...], w_ref[...].astype(BF16), preferred_element_type=F32)
    for hh in range(heads):
        base = hh * 2 * LANES
        nope = acc[:, base:base + LANES]
        pe = acc[:, base + LANES:base + 2 * LANES]
        o_ref[:, base:base + LANES] = (_rms(nope, gq_ref[:, :LANES]) * scale).astype(o_ref.dtype)
        o_ref[:, base + LANES:base + 2 * LANES] = (
            _rope_pe(pe, gq_ref[:, LANES:], cos_ref, sa_ref, sb_ref) * scale).astype(o_ref.dtype)


def _mla_q(c, g_cq, w_uq_pad, gq, cos_t, sin_a, sin_b, tm=512, heads_per_step=2):
    n = c.shape[0]
    tn = heads_per_step * 2 * LANES
    ncols = w_uq_pad.shape[1]
    row = lambda i, j: (i, 0)
    return pl.pallas_call(
        functools.partial(_mla_q_kernel, heads=heads_per_step, scale=QK_B ** -0.5),
        grid=(n // tm, ncols // tn),
        in_specs=[pl.BlockSpec((tm, Q_LORA), row),
                  pl.BlockSpec((1, Q_LORA), lambda i, j: (0, 0)),
                  pl.BlockSpec((Q_LORA, tn), lambda i, j: (0, j)),
                  pl.BlockSpec((1, 2 * LANES), lambda i, j: (0, 0)),
                  pl.BlockSpec((tm, LANES), row), pl.BlockSpec((tm, LANES), row), pl.BlockSpec((tm, LANES), row)],
        out_specs=pl.BlockSpec((tm, tn), lambda i, j: (i, j)),
        out_shape=jax.ShapeDtypeStruct((n, ncols), BF16),
        scratch_shapes=[pltpu.VMEM((tm, Q_LORA), BF16)],
        compiler_params=_cp(("parallel", "arbitrary")),
        name="mla_q_proj",
    )(c, g_cq.reshape(1, Q_LORA), w_uq_pad, gq, cos_t, sin_a, sin_b)


def _mla_kv_kernel(c_ref, gc_ref, kpe_ref, w_ref, gk_ref, cos_ref, sa_ref, sb_ref, k_ref, v_ref,
                   xn_ref, pe_ref, *, heads):
    @pl.when(pl.program_id(1) == 0)
    def _():
        xn_ref[...] = _rms(c_ref[...], gc_ref[...]).astype(BF16)
        pe_ref[...] = _rope_pe(kpe_ref[...], gk_ref[:, LANES:], cos_ref, sa_ref, sb_ref).astype(BF16)

    acc = jnp.dot(xn_ref[...], w_ref[...].astype(BF16), preferred_element_type=F32)
    for hh in range(heads):
        base = hh * 2 * LANES
        k_ref[:, base:base + LANES] = _rms(acc[:, base:base + LANES], gk_ref[:, :LANES]).astype(k_ref.dtype)
        k_ref[:, base + LANES:base + 2 * LANES] = pe_ref[...]
        v_ref[:, hh * LANES:(hh + 1) * LANES] = acc[:, base + LANES:base + 2 * LANES].astype(v_ref.dtype)


def _mla_kv(c, g_ckv, kpe, w_ukv, gk, cos_t, sin_a, sin_b, tm=512, heads_per_step=2):
    n = c.shape[0]
    tn = heads_per_step * 2 * LANES
    ncols = w_ukv.shape[1]
    cblk = Q_LORA // KV_LORA
    row = lambda i, j: (i, 0)
    return pl.pallas_call(
        functools.partial(_mla_kv_kernel, heads=heads_per_step),
        grid=(n // tm, ncols // tn),
        in_specs=[pl.BlockSpec((tm, KV_LORA), lambda i, j: (i, cblk)),
                  pl.BlockSpec((1, KV_LORA), lambda i, j: (0, 0)),
                  pl.BlockSpec((tm, LANES), row),
                  pl.BlockSpec((KV_LORA, tn), lambda i, j: (0, j)),
                  pl.BlockSpec((1, 2 * LANES), lambda i, j: (0, 0)),
                  pl.BlockSpec((tm, LANES), row), pl.BlockSpec((tm, LANES), row), pl.BlockSpec((tm, LANES), row)],
        out_specs=[pl.BlockSpec((tm, tn), lambda i, j: (i, j)),
                   pl.BlockSpec((tm, tn // 2), lambda i, j: (i, j))],
        out_shape=[jax.ShapeDtypeStruct((n, ncols), BF16),
                   jax.ShapeDtypeStruct((n, ncols // 2), BF16)],
        scratch_shapes=[pltpu.VMEM((tm, KV_LORA), BF16), pltpu.VMEM((tm, LANES), BF16)],
        compiler_params=_cp(("parallel", "arbitrary")),
        name="mla_kv_proj",
    )(c, g_ckv.reshape(1, KV_LORA), kpe, w_ukv, gk, cos_t, sin_a, sin_b)


def _attn_b_kernel(q_ref, k_ref, v_ref, o_ref, *, heads):
    i = pl.program_id(2)
    tq = q_ref.shape[0]
    qw = 2 * LANES
    qs = [q_ref[:, h * qw:(h + 1) * qw] for h in range(heads)]

    def step(kb, carry, masked):
        start = pl.multiple_of(kb * ATT_BLOCK, ATT_BLOCK)
        if masked:
            r = lax.broadcasted_iota(I32, (tq, ATT_BLOCK), 0) // CHUNK
            c = lax.broadcasted_iota(I32, (tq, ATT_BLOCK), 1) // CHUNK
            keep = c <= r
        out = []
        for h in range(heads):
            m, l, acc = carry[h]
            s = lax.dot_general(qs[h], k_ref[pl.ds(start, ATT_BLOCK), h * qw:(h + 1) * qw], _NT,
                                preferred_element_type=F32)
            if masked:
                s = jnp.where(keep, s, NEG)
            m_new = jnp.maximum(m, s.max(-1, keepdims=True))
            alpha = jnp.exp(m - m_new)
            p = jnp.exp(s - m_new)
            l = alpha * l + p.sum(-1, keepdims=True)
            acc = alpha * acc + jnp.dot(p.astype(BF16),
                                        v_ref[pl.ds(start, ATT_BLOCK), h * V_DIM:(h + 1) * V_DIM],
                                        preferred_element_type=F32)
            out.append((m_new, l, acc))
        return tuple(out)

    init = tuple((jnp.full((tq, 1), NEG, F32), jnp.zeros((tq, 1), F32), jnp.zeros((tq, V_DIM), F32))
                 for _ in range(heads))
    carry = lax.fori_loop(0, i, lambda kb, c: step(kb, c, False), init)
    final = step(i, carry, True)
    for h in range(heads):
        _, l, acc = final[h]
        o_ref[:, h * V_DIM:(h + 1) * V_DIM] = (acc / l).astype(o_ref.dtype)


def _attn_b(qf, kf, vb, batch, seq, heads_per_step=4):
    n = qf.shape[0]
    nq = seq // ATT_BLOCK
    qw = heads_per_step * 2 * LANES
    vw = heads_per_step * V_DIM
    return pl.pallas_call(
        functools.partial(_attn_b_kernel, heads=heads_per_step),
        grid=(batch, HEADS_B // heads_per_step, nq),
        in_specs=[pl.BlockSpec((ATT_BLOCK, qw), lambda b, g, i: (b * nq + i, g)),
                  pl.BlockSpec((seq, qw), lambda b, g, i: (b, g)),
                  pl.BlockSpec((seq, vw), lambda b, g, i: (b, g))],
        out_specs=pl.BlockSpec((ATT_BLOCK, vw), lambda b, g, i: (b * nq + i, g)),
        out_shape=jax.ShapeDtypeStruct((n, HEADS_B * V_DIM), BF16),
        compiler_params=_cp(("parallel", "parallel", "arbitrary")),
        name="attn_latent",
    )(qf, kf, vb)


def _out_proj_kernel(oa_ref, ob_ref, w_ref, x_ref, o_ref):
    ka = oa_ref.shape[1]
    acc = jnp.dot(oa_ref[...], w_ref[:ka, :].astype(BF16), preferred_element_type=F32)
    acc = acc + jnp.dot(ob_ref[...], w_ref[ka:, :].astype(BF16), preferred_element_type=F32)
    o_ref[...] = x_ref[...] + acc


def _out_proj(oa, ob, w_o, x, tm=512, tn=512):
    n, ka = oa.shape
    kb = ob.shape[1]
    d = w_o.shape[1]
    return pl.pallas_call(
        _out_proj_kernel,
        grid=(d // tn, n // tm),
        in_specs=[pl.BlockSpec((tm, ka), lambda j, i: (i, 0)),
                  pl.BlockSpec((tm, kb), lambda j, i: (i, 0)),
                  pl.BlockSpec((ka + kb, tn), lambda j, i: (0, j)),
                  pl.BlockSpec((tm, tn), lambda j, i: (i, j))],
        out_specs=pl.BlockSpec((tm, tn), lambda j, i: (i, j)),
        out_shape=jax.ShapeDtypeStruct((n, d), F32),
        compiler_params=_cp(("parallel", "arbitrary")),
        name="out_proj",
    )(oa, ob, w_o, x)


def _normed_proj_kernel(x_ref, g_ref, w_ref, gs_ref, o_ref, *, norm_cols):
    h = _rms(x_ref[...], g_ref[...]).astype(BF16)
    acc = jnp.dot(h, w_ref[...].astype(BF16), preferred_element_type=F32)
    for c in range(acc.shape[1] // LANES):
        sl = slice(c * LANES, (c + 1) * LANES)
        if c * LANES < norm_cols:
            o_ref[:, sl] = _rms(acc[:, sl], gs_ref[:, sl]).astype(o_ref.dtype)
        else:
            o_ref[:, sl] = acc[:, sl].astype(o_ref.dtype)


def _normed_proj(x, g, w, gs, norm_cols, name, tm=256):
    n, d = x.shape
    nc = w.shape[1]
    return pl.pallas_call(
        functools.partial(_normed_proj_kernel, norm_cols=norm_cols),
        grid=(n // tm,),
        in_specs=[pl.BlockSpec((tm, d), lambda i: (i, 0)),
                  pl.BlockSpec((1, d), lambda i: (0, 0)),
                  pl.BlockSpec((d, nc), lambda i: (0, 0)),
                  pl.BlockSpec((1, nc), lambda i: (0, 0))],
        out_specs=pl.BlockSpec((tm, nc), lambda i: (i, 0)),
        out_shape=jax.ShapeDtypeStruct((n, nc), BF16),
        compiler_params=_cp(("parallel",)),
        name=name,
    )(x, g.reshape(1, d), w, gs.reshape(1, nc))


def _cross_out_kernel(q_ref, k_ref, v_ref, w_ref, x_ref, o_ref):
    outs = []
    for hh in range(MEM_HEADS):
        sl = slice(hh * MEM_HEAD_DIM, (hh + 1) * MEM_HEAD_DIM)
        s = lax.dot_general(q_ref[:, sl], k_ref[:, sl], _NT, preferred_element_type=F32)
        p = jnp.exp(s - s.max(-1, keepdims=True))
        o = jnp.dot(p.astype(BF16), v_ref[:, sl], preferred_element_type=F32)
        outs.append((o / p.sum(-1, keepdims=True)).astype(BF16))
    acc = jnp.zeros(x_ref.shape, F32)
    for hh in range(MEM_HEADS):
        sl = slice(hh * MEM_HEAD_DIM, (hh + 1) * MEM_HEAD_DIM)
        acc = acc + jnp.dot(outs[hh], w_ref[sl, :].astype(BF16), preferred_element_type=F32)
    o_ref[...] = x_ref[...] + acc


def _cross_out(qx, kx, vx, w_xo, x, batch, seq, mem_tokens, tm=256):
    n, d = x.shape
    per_b = seq // tm
    hw = MEM_HEADS * MEM_HEAD_DIM
    return pl.pallas_call(
        _cross_out_kernel,
        grid=(batch, per_b),
        in_specs=[pl.BlockSpec((tm, hw), lambda b, i: (b * per_b + i, 0)),
                  pl.BlockSpec((mem_tokens, hw), lambda b, i: (b, 0)),
                  pl.BlockSpec((mem_tokens, hw), lambda b, i: (b, 1)),
                  pl.BlockSpec((hw, d), lambda b, i: (0, 0)),
                  pl.BlockSpec((tm, d), lambda b, i: (b * per_b + i, 0))],
        out_specs=pl.BlockSpec((tm, d), lambda b, i: (b * per_b + i, 0)),
        out_shape=jax.ShapeDtypeStruct((n, d), F32),
        compiler_params=_cp(("parallel", "arbitrary")),
        name="cross_attn_out",
    )(qx, kx, vx, w_xo, x)


def _router_kernel(x_ref, g_ref, wr_ref, b_ref, tri_ref, hp_ref, eid_ref, gate_ref, rank_ref, cnt_ref,
                   run_ref, scr_ref):
    t = pl.program_id(0)

    @pl.when(t == 0)
    def _():
        run_ref[...] = jnp.zeros_like(run_ref)

    h = _rms(x_ref[...], g_ref[...])
    _pack_store(hp_ref, 0, h, scr_ref)
    tm = h.shape[0]
    per_g = N_EXPERTS // N_GROUPS

    logits = lax.dot_general(wr_ref[...], h, _NT, precision=lax.Precision.HIGHEST,
                             preferred_element_type=F32)
    scores = 1.0 / (1.0 + jnp.exp(-logits))
    choice = scores + b_ref[...]

    sub = lax.broadcasted_iota(I32, (per_g, tm), 0).astype(F32)
    rows = []
    for g in range(N_GROUPS):
        c = choice[g * per_g:(g + 1) * per_g, :]
        m1 = c.max(0, keepdims=True)
        first = jnp.where(c == m1, sub, float(per_g)).min(0, keepdims=True)
        m2 = jnp.where(sub == first, -jnp.inf, c).max(0, keepdims=True)
        rows.append(m1 + m2)
    gs = jnp.concatenate(rows, axis=0)

    gsub = lax.broadcasted_iota(I32, (N_GROUPS, tm), 0).astype(F32)
    beaten = jnp.zeros((N_GROUPS, tm), F32)
    for g2 in range(N_GROUPS):
        row = gs[g2:g2 + 1, :]
        wins = (row > gs) | ((row == gs) & (gsub > float(g2)))
        beaten = beaten + jnp.where(wins, 1.0, 0.0)
    g_ok = jnp.where(beaten < TOPK_GROUPS, 1.0, 0.0)
    e_ok = jnp.concatenate(
        [jnp.broadcast_to(g_ok[g:g + 1, :], (per_g, tm)) for g in range(N_GROUPS)], axis=0)

    eiota = lax.broadcasted_iota(I32, (N_EXPERTS, tm), 0).astype(F32)
    masked = jnp.where(e_ok > 0.5, choice, -jnp.inf)
    chosen = jnp.zeros((N_EXPERTS, tm), F32)
    eids, ws = [], []
    for _ in range(TOP_K):
        m = masked.max(0, keepdims=True)
        idx = jnp.where(masked == m, eiota, float(N_EXPERTS)).min(0, keepdims=True)
        sel = eiota == idx
        ws.append(jnp.where(sel, scores, 0.0).sum(0, keepdims=True))
        masked = jnp.where(sel, -jnp.inf, masked)
        chosen = jnp.where(sel, 1.0, chosen)
        eids.append(idx)
    wsum = ws[0]
    for w in ws[1:]:
        wsum = wsum + w
    denom = wsum + 1e-20

    pos = jnp.dot(chosen.astype(BF16), tri_ref[...], preferred_element_type=F32) + run_ref[:, 0:1]
    run_ref[...] = run_ref[...] + chosen.sum(1, keepdims=True)
    cnt_ref[...] = run_ref[...].astype(I32)

    for r in range(TOP_K):
        eid_ref[r:r + 1, :] = eids[r].astype(I32)
        gate_ref[r:r + 1, :] = ws[r] / denom * ROUTED_SCALE
        rank_ref[r:r + 1, :] = jnp.where(eiota == eids[r], pos, 0.0).sum(0, keepdims=True).astype(I32)


def _router(x, g, w_router, router_bias, tm=512):
    n, d = x.shape
    ne = w_router.shape[1]
    tri = (jnp.arange(tm)[:, None] < jnp.arange(tm)[None, :]).astype(BF16)
    row8 = lambda i: (0, i)
    pitch = d // 2 // LANES
    return pl.pallas_call(
        _router_kernel,
        grid=(n // tm,),
        in_specs=[pl.BlockSpec((tm, d), lambda i: (i, 0)),
                  pl.BlockSpec((1, d), lambda i: (0, 0)),
                  pl.BlockSpec((ne, d), lambda i: (0, 0)),
                  pl.BlockSpec((ne, 1), lambda i: (0, 0)),
                  pl.BlockSpec((tm, tm), lambda i: (0, 0))],
        out_specs=[pl.BlockSpec((tm * pitch, LANES), lambda i: (i, 0)),
                   pl.BlockSpec((TOP_K, tm), row8), pl.BlockSpec((TOP_K, tm), row8),
                   pl.BlockSpec((TOP_K, tm), row8),
                   pl.BlockSpec((ne, LANES), lambda i: (0, 0))],
        out_shape=[jax.ShapeDtypeStruct((n * pitch, LANES), U32),
                   jax.ShapeDtypeStruct((TOP_K, n), I32), jax.ShapeDtypeStruct((TOP_K, n), F32),
                   jax.ShapeDtypeStruct((TOP_K, n), I32),
                   jax.ShapeDtypeStruct((ne, LANES), I32)],
        scratch_shapes=[pltpu.VMEM((ne, LANES), F32), pltpu.VMEM((2 * tm * pitch, LANES), F32)],
        compiler_params=_cp(("arbitrary",)),
        name="router",
    )(x, g.reshape(1, d), w_router.T, router_bias.reshape(ne, 1), tri)


def _dispatch_kernel(cnt_ref, start_ref, dest_ref, hp_ref, xs_ref, sem, *, tm, pitch):
    t = pl.program_id(0)
    pairs = TOP_K * tm

    def row_copy(r, slot):
        src = hp_ref.at[pl.ds(pl.multiple_of(r * pitch, pitch), pitch), :]
        dst = xs_ref.at[pl.ds(pl.multiple_of(slot * pitch, pitch), pitch), :]
        return pltpu.make_async_copy(src, dst, sem)

    def issue(g, c):
        for u in range(DMA_UNROLL):
            p = g * DMA_UNROLL + u
            row_copy(p & (tm - 1), dest_ref[0, 0, p]).start(priority=u % 2)
        return c

    lax.fori_loop(0, pairs // DMA_UNROLL, issue, 0)
    for _ in range(TOP_K):
        pltpu.make_async_copy(hp_ref, xs_ref.at[pl.ds(0, tm * pitch), :], sem).wait()

    @pl.when(t == pl.num_programs(0) - 1)
    def _():
        def per_expert(e, c):
            used = cnt_ref[e]
            first = start_ref[e] * STEP_ROWS + used
            npad = (-used) & (EXPERT_ROWS - 1)

            def pad_issue(p, c2):
                row_copy(0, first + p).start()
                return c2

            def pad_drain(p, c2):
                row_copy(0, 0).wait()
                return c2

            lax.fori_loop(0, npad, pad_issue, 0)
            lax.fori_loop(0, npad, pad_drain, 0)
            return c

        lax.fori_loop(0, N_EXPERTS, per_expert, 0)


def _dispatch(hp, dest, counts, step_start, p_rows, pitch, tm=256):
    n = hp.shape[0] // pitch
    tiles = n // tm
    dest_t = dest.reshape(TOP_K, tiles, tm).transpose(1, 0, 2).reshape(tiles, 1, TOP_K * tm)
    grid_spec = pltpu.PrefetchScalarGridSpec(
        num_scalar_prefetch=2,
        grid=(tiles,),
        in_specs=[pl.BlockSpec((1, 1, TOP_K * tm), lambda i, *_: (i, 0, 0), memory_space=pltpu.SMEM),
                  pl.BlockSpec((tm * pitch, LANES), lambda i, *_: (i, 0))],
        out_specs=pl.BlockSpec(memory_space=pl.ANY),
        scratch_shapes=[pltpu.SemaphoreType.DMA(())],
    )
    return pl.pallas_call(
        functools.partial(_dispatch_kernel, tm=tm, pitch=pitch),
        grid_spec=grid_spec,
        out_shape=jax.ShapeDtypeStruct((p_rows * pitch, LANES), U32),
        compiler_params=_cp(("arbitrary",)),
        name="moe_dispatch",
    )(counts, step_start, dest_t, hp)


def _silu_mul(g, u):
    return g / (1.0 + jnp.exp(-g)) * u


def _expert_weights(sched_ref, w_hbm_refs, wbuf_ref, sem):
    b = pl.program_id(0)

    def copies(e, slot):
        return [pltpu.make_async_copy(w.at[e], wbuf_ref.at[slot, i], sem.at[slot])
                for i, w in enumerate(w_hbm_refs)]

    @pl.when(b == 0)
    def _():
        for c in copies(sched_ref[0, 0], 0):
            c.start()

    slot = sched_ref[3, b]

    @pl.when(sched_ref[2, b] == 1)
    def _():
        for c in copies(0, slot):
            c.wait()
        nxt = sched_ref[4, b]

        @pl.when(nxt >= 0)
        def _():
            for c in copies(nxt, 1 - slot):
                c.start()

    return slot


def _expert_up_kernel(sched_ref, nu_ref, xs_ref, wg_hbm, wu_hbm, o_ref, wbuf_ref, scr_ref, sem):
    b = pl.program_id(0)
    pitch = xs_ref.shape[0] // STEP_ROWS
    slot = _expert_weights(sched_ref, (wg_hbm, wu_hbm), wbuf_ref, sem)

    def sub_block(j):
        x = _load_unpacked(xs_ref, j * EXPERT_ROWS, EXPERT_ROWS, pitch, scr_ref)
        g = jnp.dot(x, wbuf_ref[slot, 0], preferred_element_type=F32)
        u = jnp.dot(x, wbuf_ref[slot, 1], preferred_element_type=F32)
        o_ref[j * EXPERT_ROWS:(j + 1) * EXPERT_ROWS, :] = _silu_mul(g, u).astype(o_ref.dtype)

    for j in range(STEP_ROWS // EXPERT_ROWS):
        pl.when(sched_ref[1, b] > j)(functools.partial(sub_block, j))


def _row_block(b, sched, nu):
    return (jnp.minimum(b, nu[0] - 1), 0)


def _expert_up(xs, w_gate, w_up, sched, n_used, n_steps, pitch):
    _, d, ff = w_gate.shape
    grid_spec = pltpu.PrefetchScalarGridSpec(
        num_scalar_prefetch=2,
        grid=(n_steps,),
        in_specs=[pl.BlockSpec((STEP_ROWS * pitch, LANES), _row_block),
                  pl.BlockSpec(memory_space=pl.ANY), pl.BlockSpec(memory_space=pl.ANY)],
        out_specs=pl.BlockSpec((STEP_ROWS, ff), _row_block),
        scratch_shapes=[pltpu.VMEM((2, 2, d, ff), F32),
                        pltpu.VMEM((2 * EXPERT_ROWS * pitch, LANES), F32),
                        pltpu.SemaphoreType.DMA((2,))],
    )
    return pl.pallas_call(
        _expert_up_kernel,
        grid_spec=grid_spec,
        out_shape=jax.ShapeDtypeStruct((n_steps * STEP_ROWS, ff), BF16),
        compiler_params=_cp(("arbitrary",)),
        name="expert_up",
    )(sched, n_used, xs, w_gate, w_up)


def _expert_down_kernel(sched_ref, nu_ref, h_ref, wd_hbm, o_ref, wbuf_ref, scr_ref, sem):
    b = pl.program_id(0)
    slot = _expert_weights(sched_ref, (wd_hbm,), wbuf_ref, sem)

    def sub_block(j):
        y = jnp.dot(h_ref[j * EXPERT_ROWS:(j + 1) * EXPERT_ROWS, :], wbuf_ref[slot, 0].astype(BF16),
                    preferred_element_type=F32)
        _pack_store(o_ref, j * EXPERT_ROWS, y, scr_ref)

    for j in range(STEP_ROWS // EXPERT_ROWS):
        pl.when(sched_ref[1, b] > j)(functools.partial(sub_block, j))


def _expert_down(hs, w_down, sched, n_used, n_steps, pitch):
    p_rows, ff = hs.shape
    d = w_down.shape[2]
    grid_spec = pltpu.PrefetchScalarGridSpec(
        num_scalar_prefetch=2,
        grid=(n_steps,),
        in_specs=[pl.BlockSpec((STEP_ROWS, ff), _row_block), pl.BlockSpec(memory_space=pl.ANY)],
        out_specs=pl.BlockSpec((STEP_ROWS * pitch, LANES), _row_block),
        scratch_shapes=[pltpu.VMEM((2, 1, ff, d), F32),
                        pltpu.VMEM((2 * EXPERT_ROWS * pitch, LANES), F32),
                        pltpu.SemaphoreType.DMA((2,))],
    )
    return pl.pallas_call(
        _expert_down_kernel,
        grid_spec=grid_spec,
        out_shape=jax.ShapeDtypeStruct((p_rows * pitch, LANES), U32),
        compiler_params=_cp(("arbitrary",)),
        name="expert_down",
    )(sched, n_used, hs, w_down)


def _shared_up_kernel(hp_ref, wg_ref, wu_ref, o_ref, scr_ref):
    tm = o_ref.shape[0]
    x = _load_unpacked(hp_ref, 0, tm, hp_ref.shape[0] // tm, scr_ref)
    g = jnp.dot(x, wg_ref[...], preferred_element_type=F32)
    u = jnp.dot(x, wu_ref[...], preferred_element_type=F32)
    o_ref[...] = _silu_mul(g, u).astype(o_ref.dtype)


def _shared_up(hp, w_gate, w_up, pitch, tm=256):
    n = hp.shape[0] // pitch
    d, ff = w_gate.shape
    return pl.pallas_call(
        _shared_up_kernel,
        grid=(n // tm,),
        in_specs=[pl.BlockSpec((tm * pitch, LANES), lambda i: (i, 0)),
                  pl.BlockSpec((d, ff), lambda i: (0, 0)), pl.BlockSpec((d, ff), lambda i: (0, 0))],
        out_specs=pl.BlockSpec((tm, ff), lambda i: (i, 0)),
        out_shape=jax.ShapeDtypeStruct((n, ff), BF16),
        scratch_shapes=[pltpu.VMEM((2 * tm * pitch, LANES), F32)],
        compiler_params=_cp(("parallel",)),
        name="shared_up",
    )(hp, w_gate, w_up)


def _shared_down_kernel(t_ref, w_ref, x_ref, o_ref):
    o_ref[...] = x_ref[...] + jnp.dot(t_ref[...], w_ref[...].astype(BF16), preferred_element_type=F32)


def _shared_down(t, w_down, x, tm=256):
    n, ff = t.shape
    d = w_down.shape[1]
    return pl.pallas_call(
        _shared_down_kernel,
        grid=(n // tm,),
        in_specs=[pl.BlockSpec((tm, ff), lambda i: (i, 0)),
                  pl.BlockSpec((ff, d), lambda i: (0, 0)),
                  pl.BlockSpec((tm, d), lambda i: (i, 0))],
        out_specs=pl.BlockSpec((tm, d), lambda i: (i, 0)),
        out_shape=jax.ShapeDtypeStruct((n, d), F32),
        compiler_params=_cp(("parallel",)),
        name="shared_down",
    )(t, w_down, x)


def _combine_kernel(dfirst_ref, dnext_ref, gate_ref, base_ref, ys_ref, o_ref, buf_ref, scr_ref, sem, *,
                    tm, pitch):
    t = pl.program_id(0)
    pairs = TOP_K * tm
    slot_rows = pairs * pitch

    def issue_tile(dref, slot):
        def row_copy(p):
            src = ys_ref.at[pl.ds(pl.multiple_of(dref[0, 0, p] * pitch, pitch), pitch), :]
            dst = buf_ref.at[pl.ds(pl.multiple_of(slot * slot_rows + p * pitch, pitch), pitch), :]
            return pltpu.make_async_copy(src, dst, sem.at[slot])

        def body(g, c):
            for u in range(DMA_UNROLL):
                row_copy(g * DMA_UNROLL + u).start(priority=u % 2)
            return c

        lax.fori_loop(0, pairs // DMA_UNROLL, body, 0)

    @pl.when(t == 0)
    def _():
        issue_tile(dfirst_ref, 0)

    @pl.when(t + 1 < pl.num_programs(0))
    def _():
        issue_tile(dnext_ref, (t + 1) % 2)

    cur = t % 2
    for _ in range(TOP_K):
        pltpu.make_async_copy(ys_ref.at[pl.ds(0, tm * pitch), :], buf_ref.at[pl.ds(0, tm * pitch), :],
                              sem.at[cur]).wait()

    for grp in range(tm // COMBINE_GROUP):
        r0 = grp * COMBINE_GROUP
        rows = slice(r0, r0 + COMBINE_GROUP)
        gates = [jnp.broadcast_to(gate_ref[2 * r0:2 * (r0 + COMBINE_GROUP), k:k + 1], (2 * COMBINE_GROUP, LANES))
                 for k in range(TOP_K)]
        for s in range(pitch):
            acc = None
            for k in range(TOP_K):
                term = gates[k] * _load_pairs(buf_ref, cur * pairs + k * tm + r0, COMBINE_GROUP, pitch, s)
                acc = term if acc is None else acc + term
            even, odd = _split_pairs(acc, scr_ref, grp * pitch + s)
            c0 = slice((2 * s) * LANES, (2 * s + 1) * LANES)
            c1 = slice((2 * s + 1) * LANES, (2 * s + 2) * LANES)
            o_ref[rows, c0] = base_ref[rows, c0] + even
            o_ref[rows, c1] = base_ref[rows, c1] + odd


def _combine(dest, gate_rows, base, ys, pitch, tm=128):
    n, d = base.shape
    tiles = n // tm
    dest_t = dest.reshape(TOP_K, tiles, tm).transpose(1, 0, 2).reshape(tiles, 1, TOP_K * tm)
    dspec = lambda imap: pl.BlockSpec((1, 1, TOP_K * tm), imap, memory_space=pltpu.SMEM)
    return pl.pallas_call(
        functools.partial(_combine_kernel, tm=tm, pitch=pitch),
        grid=(tiles,),
        in_specs=[dspec(lambda i: (0, 0, 0)),
                  dspec(lambda i: (jnp.minimum(i + 1, tiles - 1), 0, 0)),
                  pl.BlockSpec((2 * tm, TOP_K), lambda i: (i, 0)),
                  pl.BlockSpec((tm, d), lambda i: (i, 0)),
                  pl.BlockSpec(memory_space=pl.ANY)],
        out_specs=pl.BlockSpec((tm, d), lambda i: (i, 0)),
        out_shape=jax.ShapeDtypeStruct((n, d), F32),
        scratch_shapes=[pltpu.VMEM((2 * TOP_K * tm * pitch, LANES), U32),
                        pltpu.VMEM((2 * tm * pitch, LANES), F32),
                        pltpu.SemaphoreType.DMA((2,))],
        compiler_params=_cp(("arbitrary",)),
        name="moe_combine",
    )(dest_t, dest_t, gate_rows, base, ys)


def _rope_tables(positions):
    half = ROPE_DIM // 2
    inv_freq = ROPE_THETA ** (-jnp.arange(0, ROPE_DIM, 2, dtype=F32) / ROPE_DIM)
    ang = positions.reshape(-1).astype(F32)[:, None] * inv_freq
    cos, sin = jnp.cos(ang), jnp.sin(ang)
    n = ang.shape[0]
    z = lambda w: jnp.zeros((n, w), F32)
    cos_t = jnp.concatenate([cos, cos, z(LANES - ROPE_DIM)], axis=1)
    sin_a = jnp.concatenate([-sin, z(LANES - half)], axis=1)
    sin_b = jnp.concatenate([z(half), sin, z(LANES - ROPE_DIM)], axis=1)
    return cos_t, sin_a, sin_b


def _pad_lanes(v, width):
    return jnp.concatenate([v, jnp.zeros((width - v.shape[0],), v.dtype)])


def kernel(x, mem, positions, g_mix, w_in, g_qa, g_ka, rel_bias, g_cq, w_uq, g_ckv, w_ukv, g_qb, g_kb, w_o, g_cross, g_mem, w_xq, w_xkv, g_qx, g_kx, w_xo, g_ffn, w_router, router_bias, w_sh_gate, w_sh_up, w_sh_down, w_ex_gate, w_ex_up, w_ex_down):
    batch, seq, d = x.shape
    n = batch * seq
    mem_tokens = mem.shape[1]
    width_a = HEADS_A * HEAD_DIM
    x2d = x.reshape(n, d)

    h = _norm_bf16(x2d, g_mix)
    gs_qk = jnp.concatenate([jnp.tile(g_qa * HEAD_DIM ** -0.5, HEADS_A), jnp.tile(g_ka, HEADS_A)])
    w_in_t = w_in.T
    qk = _in_proj(h, w_in_t, 0, 2 * width_a, BF16, "in_proj_qk", gs=gs_qk)
    v_a = _in_proj(h, w_in_t, 2 * width_a, width_a, BF16, "in_proj_v")
    c = _in_proj(h, w_in_t, 3 * width_a, Q_LORA + KV_LORA, F32, "in_proj_lora")
    w_kpe = jnp.pad(w_in_t[3 * width_a + Q_LORA + KV_LORA:], ((0, LANES - ROPE_DIM), (0, 0)))
    kpe = _in_proj(h, w_kpe, 0, LANES, F32, "in_proj_kpe")

    o_a = _attn_a(qk, v_a, _band_bias(rel_bias), batch, seq)

    cos_t, sin_a, sin_b = _rope_tables(positions)
    w_uq_pad = jnp.pad(w_uq.reshape(Q_LORA, HEADS_B, QK_B),
                       ((0, 0), (0, 0), (0, 2 * LANES - QK_B))).reshape(Q_LORA, HEADS_B * 2 * LANES)
    gq = _pad_lanes(g_qb, 2 * LANES).reshape(1, 2 * LANES)
    gk = _pad_lanes(g_kb, 2 * LANES).reshape(1, 2 * LANES)
    qf = _mla_q(c, g_cq, w_uq_pad, gq, cos_t, sin_a, sin_b)
    kf, v_b = _mla_kv(c, g_ckv, kpe, w_ukv, gk, cos_t, sin_a, sin_b)
    o_b = _attn_b(qf, kf, v_b, batch, seq)

    x1 = _out_proj(o_a, o_b, w_o, x2d)

    hw = MEM_HEADS * MEM_HEAD_DIM
    qx = _normed_proj(x1, g_cross, w_xq, jnp.tile(g_qx * MEM_HEAD_DIM ** -0.5, MEM_HEADS), hw, "cross_q")
    kvx = _normed_proj(mem.reshape(batch * mem_tokens, d), g_mem, w_xkv,
                       jnp.concatenate([jnp.tile(g_kx, MEM_HEADS), jnp.ones((hw,), F32)]), hw, "cross_kv")
    x2 = _cross_out(qx, kvx, kvx, w_xo, x1, batch, seq, mem_tokens)

    hp, eid, gate, rank, cnt = _router(x2, g_ffn, w_router, router_bias)
    counts = cnt[:, 0]
    nstep = (counts + STEP_ROWS - 1) // STEP_ROWS
    step_end = jnp.cumsum(nstep).astype(I32)
    step_start = step_end - nstep
    n_steps = n * TOP_K // STEP_ROWS + N_EXPERTS
    experts = jnp.arange(N_EXPERTS, dtype=I32)
    start_of = jnp.sum(jnp.where(eid[:, :, None] == experts, step_start, 0), axis=-1)
    dest = start_of * STEP_ROWS + rank
    steps = jnp.arange(n_steps, dtype=I32)
    step_e = jnp.minimum(jnp.sum((step_end[None, :] <= steps[:, None]).astype(I32), axis=1), N_EXPERTS - 1)
    mine = step_e[:, None] == experts[None, :]
    rows_left = jnp.sum(jnp.where(mine, counts[None, :] - STEP_ROWS * (steps[:, None] - step_start[None, :]), 0),
                        axis=1)
    used = steps < step_end[-1]
    rows_here = jnp.where(used, jnp.clip(rows_left, 0, STEP_ROWS), 0)
    nsub = (rows_here + EXPERT_ROWS - 1) // EXPERT_ROWS
    nonempty = counts > 0
    ring_slot = (jnp.cumsum(nonempty.astype(I32)) - 1) % 2
    later = (experts[None, :] > experts[:, None]) & nonempty[None, :]
    succ = jnp.min(jnp.where(later, experts[None, :], N_EXPERTS), axis=1)
    succ = jnp.where(succ == N_EXPERTS, -1, succ)
    per_step = lambda v: jnp.sum(jnp.where(mine, v[None, :], 0), axis=1)
    first = (used & (steps == per_step(step_start))).astype(I32)
    sched = jnp.stack([step_e, nsub, first, per_step(ring_slot), per_step(succ)]).astype(I32)
    n_used = step_end[-1:]
    pitch = d // 2 // LANES

    xs = _dispatch(hp, dest, counts, step_start, n_steps * STEP_ROWS, pitch)
    hs = _expert_up(xs, w_ex_gate, w_ex_up, sched, n_used, n_steps, pitch)
    ys = _expert_down(hs, w_ex_down, sched, n_used, n_steps, pitch)

    base = _shared_down(_shared_up(hp, w_sh_gate, w_sh_up, pitch), w_sh_down, x2)
    out = _combine(dest, jnp.repeat(gate.T, 2, axis=0), base, ys, pitch)
    return out.reshape(batch, seq, d)
```

```python
import functools

import jax
import jax.numpy as jnp
import numpy as np
from jax import lax
from jax.experimental import pallas as pl
from jax.experimental.pallas import tpu as pltpu

F32 = jnp.float32
BF16 = jnp.bfloat16
I32 = jnp.int32
U32 = jnp.uint32

CHUNK = 64
LEFT_CHUNKS = 8
REL_CLIP = 128
HEAD_DIM = 128
HEADS_A = 16
HEADS_B = 16
Q_LORA = 1024
KV_LORA = 512
NOPE_DIM = 128
ROPE_DIM = 64
V_DIM = 128
QK_B = NOPE_DIM + ROPE_DIM
ROPE_THETA = 10000.0
MEM_HEADS = 4
MEM_HEAD_DIM = 128
N_EXPERTS = 64
N_GROUPS = 8
TOPK_GROUPS = 4
TOP_K = 8
ROUTED_SCALE = 2.5
EPS = 1e-6

LANES = 128
ATT_BLOCK = 256
EXPERT_ROWS = 256
STEP_ROWS = 2 * EXPERT_ROWS
DMA_UNROLL = 8
COMBINE_GROUP = 16
NEG = -1e30
VMEM_LIMIT = 56 * 1024 * 1024

_NT = (((1,), (1,)), ((), ()))


def _cp(sem, vmem=VMEM_LIMIT):
    return pltpu.CompilerParams(dimension_semantics=sem, vmem_limit_bytes=vmem)


def _rms(x, g):
    return x * lax.rsqrt(jnp.mean(x * x, axis=-1, keepdims=True) + EPS) * g


def _pack_store(ref, first_token, x, scr):
    m, w = x.shape
    pitch = w // (2 * LANES)
    regions = scr.shape[0] // (2 * m)
    for s in range(pitch):
        base = (s % regions) * 2 * m
        scr[pl.ds(base, m, stride=2), :] = x[:, (2 * s) * LANES:(2 * s + 1) * LANES]
        scr[pl.ds(base + 1, m, stride=2), :] = x[:, (2 * s + 1) * LANES:(2 * s + 2) * LANES]
        z = scr[pl.ds(base, 2 * m), :].astype(BF16)
        ref[pl.ds(first_token * pitch + s, m, stride=pitch), :] = pltpu.bitcast(z, U32)


def _load_pairs(ref, first_token, m, pitch, s):
    w = ref[pl.ds(first_token * pitch + s, m, stride=pitch), :]
    return pltpu.bitcast(w, BF16).astype(F32)


def _split_pairs(z, scr, region):
    m = z.shape[0] // 2
    base = region * 2 * m
    scr[pl.ds(base, 2 * m), :] = z
    return scr[pl.ds(base, m, stride=2), :], scr[pl.ds(base + 1, m, stride=2), :]


def _load_unpacked(ref, first_token, m, pitch, scr, dtype=F32):
    regions = scr.shape[0] // (2 * m)
    cols = []
    for s in range(pitch):
        for piece in _split_pairs(_load_pairs(ref, first_token, m, pitch, s), scr, s % regions):
            cols.append(piece.astype(dtype))
    return jnp.concatenate(cols, axis=1)


def _norm_kernel(x_ref, g_ref, o_ref):
    o_ref[...] = _rms(x_ref[...], g_ref[...]).astype(o_ref.dtype)


def _norm_bf16(x, g, tm=256):
    n, d = x.shape
    return pl.pallas_call(
        _norm_kernel,
        grid=(n // tm,),
        in_specs=[pl.BlockSpec((tm, d), lambda i: (i, 0)), pl.BlockSpec((1, d), lambda i: (0, 0))],
        out_specs=pl.BlockSpec((tm, d), lambda i: (i, 0)),
        out_shape=jax.ShapeDtypeStruct((n, d), BF16),
        compiler_params=_cp(("parallel",)),
        name="norm_mix",
    )(x, g.reshape(1, d))


def _in_proj_kernel(*refs, head_norm, tn):
    if head_norm:
        h_ref, w_ref, gs_ref, o_ref = refs
    else:
        h_ref, w_ref, o_ref = refs
    acc = lax.dot_general(h_ref[...], w_ref[...].astype(BF16), _NT, preferred_element_type=F32)
    if head_norm:
        for c in range(tn // LANES):
            sl = slice(c * LANES, (c + 1) * LANES)
            o_ref[:, sl] = _rms(acc[:, sl], gs_ref[:, sl]).astype(o_ref.dtype)
    else:
        o_ref[...] = acc.astype(o_ref.dtype)


def _in_proj(h, w_t, col0, ncols, out_dtype, name, gs=None, tm=1024, tn=512):
    n, d = h.shape
    tn = min(tn, ncols)
    j0 = col0 // tn
    in_specs = [pl.BlockSpec((tm, d), lambda i, j: (i, 0)),
                pl.BlockSpec((tn, d), lambda i, j: (j0 + j, 0))]
    args = [h, w_t]
    if gs is not None:
        in_specs.append(pl.BlockSpec((1, tn), lambda i, j: (0, j)))
        args.append(gs.reshape(1, ncols))
    return pl.pallas_call(
        functools.partial(_in_proj_kernel, head_norm=gs is not None, tn=tn),
        grid=(n // tm, ncols // tn),
        in_specs=in_specs,
        out_specs=pl.BlockSpec((tm, tn), lambda i, j: (i, j)),
        out_shape=jax.ShapeDtypeStruct((n, ncols), out_dtype),
        compiler_params=_cp(("parallel", "arbitrary")),
        name=name,
    )(*args)


def _attn_a_kernel(q_ref, k0_ref, k1_ref, k2_ref, v0_ref, v1_ref, v2_ref, b_ref, o_ref, *, heads):
    i = pl.program_id(2)
    k_refs = (k0_ref, k1_ref, k2_ref)
    v_refs = (v0_ref, v1_ref, v2_ref)
    for hh in range(heads):
        sl = slice(hh * HEAD_DIM, (hh + 1) * HEAD_DIM)
        q = q_ref[:, sl]
        s = []
        for d in range(3):
            sd = lax.dot_general(q, k_refs[d][:, sl], _NT, preferred_element_type=F32)
            sd = sd + b_ref[hh, :, d * ATT_BLOCK:(d + 1) * ATT_BLOCK]
            if d > 0:
                sd = jnp.where(i >= d, sd, NEG)
            s.append(sd)
        m = jnp.maximum(jnp.maximum(s[0].max(-1, keepdims=True), s[1].max(-1, keepdims=True)),
                        s[2].max(-1, keepdims=True))
        l = jnp.zeros_like(m)
        o = jnp.zeros((q.shape[0], HEAD_DIM), F32)
        for d in range(3):
            p = jnp.exp(s[d] - m)
            l = l + p.sum(-1, keepdims=True)
            o = o + jnp.dot(p.astype(BF16), v_refs[d][:, sl], preferred_element_type=F32)
        o_ref[:, sl] = (o / l).astype(o_ref.dtype)


def _band_bias(rel_bias):
    blk = ATT_BLOCK
    period = 2 * blk
    heads = rel_bias.shape[0]
    r = np.arange(blk)[:, None]
    c = np.arange(blk)[None, :]
    per_blk = blk // CHUNK
    k = np.arange(period)
    delta = np.where(k < blk, k, k - period)
    tiles = []
    for d in range(3):
        idx = np.clip(blk * d - delta, -(CHUNK - 1), REL_CLIP) + CHUNK - 1
        w = rel_bias[:, idx].astype(F32)
        b = jnp.tile(w, (1, blk))[:, :blk * (period - 1)].reshape(heads, blk, period - 1)[:, :, :blk]
        cdiff = per_blk * d + r // CHUNK - c // CHUNK
        valid = (cdiff >= 0) & (cdiff <= LEFT_CHUNKS)
        tiles.append(jnp.where(valid[None], b, NEG))
    return jnp.concatenate(tiles, axis=-1)


def _attn_a(qk, v, bias, batch, seq, heads_per_step=4):
    n = qk.shape[0]
    nq = seq // ATT_BLOCK
    hw = heads_per_step * HEAD_DIM
    groups = HEADS_A // heads_per_step
    kcol0 = HEADS_A * HEAD_DIM // hw

    def kspec(d, col0):
        return pl.BlockSpec((ATT_BLOCK, hw), lambda g, b, i: (b * nq + jnp.maximum(i - d, 0), col0 + g))

    return pl.pallas_call(
        functools.partial(_attn_a_kernel, heads=heads_per_step),
        grid=(groups, batch, nq),
        in_specs=[pl.BlockSpec((ATT_BLOCK, hw), lambda g, b, i: (b * nq + i, g)),
                  kspec(0, kcol0), kspec(1, kcol0), kspec(2, kcol0),
                  kspec(0, 0), kspec(1, 0), kspec(2, 0),
                  pl.BlockSpec((heads_per_step, ATT_BLOCK, 3 * ATT_BLOCK), lambda g, b, i: (g, 0, 0))],
        out_specs=pl.BlockSpec((ATT_BLOCK, hw), lambda g, b, i: (b * nq + i, g)),
        out_shape=jax.ShapeDtypeStruct((n, HEADS_A * HEAD_DIM), BF16),
        compiler_params=_cp(("parallel", "parallel", "arbitrary")),
        name="attn_band",
    )(qk, qk, qk, qk, v, v, v, bias)


def _rope_pe(pe, g, cos_ref, sa_ref, sb_ref):
    ss = jnp.sum(pe * pe, axis=-1, keepdims=True) * (1.0 / ROPE_DIM)
    pn = pe * lax.rsqrt(ss + EPS) * g
    half = ROPE_DIM // 2
    return (pn * cos_ref[...] + pltpu.roll(pn, LANES - half, 1) * sa_ref[...]
            + pltpu.roll(pn, half, 1) * sb_ref[...])


def _mla_q_kernel(c_ref, gc_ref, w_ref, gq_ref, cos_ref, sa_ref, sb_ref, o_ref, xn_ref, *, heads, scale):
    @pl.when(pl.program_id(1) == 0)
    def _():
        xn_ref[...] = _rms(c_ref[...], gc_ref[...]).astype(BF16)

    acc = jnp.dot(xn_ref[...], w_ref[...].astype(BF16), preferred_element_type=F32)
    for hh in range(heads):
        base = hh * 2 * LANES
        nope = acc[:, base:base + LANES]
        pe = acc[:, base + LANES:base + 2 * LANES]
        o_ref[:, base:base + LANES] = (_rms(nope, gq_ref[:, :LANES]) * scale).astype(o_ref.dtype)
        o_ref[:, base + LANES:base + 2 * LANES] = (
            _rope_pe(pe, gq_ref[:, LANES:], cos_ref, sa_ref, sb_ref) * scale).astype(o_ref.dtype)


def _mla_q(c, g_cq, w_uq_pad, gq, cos_t, sin_a, sin_b, tm=512, heads_per_step=4):
    n = c.shape[0]
    tn = heads_per_step * 2 * LANES
    ncols = w_uq_pad.shape[1]
    row = lambda i, j: (i, 0)
    return pl.pallas_call(
        functools.partial(_mla_q_kernel, heads=heads_per_step, scale=QK_B ** -0.5),
        grid=(n // tm, ncols // tn),
        in_specs=[pl.BlockSpec((tm, Q_LORA), row),
                  pl.BlockSpec((1, Q_LORA), lambda i, j: (0, 0)),
                  pl.BlockSpec((Q_LORA, tn), lambda i, j: (0, j)),
                  pl.BlockSpec((1, 2 * LANES), lambda i, j: (0, 0)),
                  pl.BlockSpec((tm, LANES), row), pl.BlockSpec((tm, LANES), row), pl.BlockSpec((tm, LANES), row)],
        out_specs=pl.BlockSpec((tm, tn), lambda i, j: (i, j)),
        out_shape=jax.ShapeDtypeStruct((n, ncols), BF16),
        scratch_shapes=[pltpu.VMEM((tm, Q_LORA), BF16)],
        compiler_params=_cp(("parallel", "arbitrary")),
        name="mla_q_proj",
    )(c, g_cq.reshape(1, Q_LORA), w_uq_pad, gq, cos_t, sin_a, sin_b)


def _mla_kv_kernel(c_ref, gc_ref, kpe_ref, w_ref, gk_ref, cos_ref, sa_ref, sb_ref, k_ref, v_ref,
                   xn_ref, pe_ref, *, heads):
    @pl.when(pl.program_id(1) == 0)
    def _():
        xn_ref[...] = _rms(c_ref[...], gc_ref[...]).astype(BF16)
        pe_ref[...] = _rope_pe(kpe_ref[...], gk_ref[:, LANES:], cos_ref, sa_ref, sb_ref).astype(BF16)

    acc = jnp.dot(xn_ref[...], w_ref[...].astype(BF16), preferred_element_type=F32)
    for hh in range(heads):
        base = hh * 2 * LANES
        k_ref[:, base:base + LANES] = _rms(acc[:, base:base + LANES], gk_ref[:, :LANES]).astype(k_ref.dtype)
        k_ref[:, base + LANES:base + 2 * LANES] = pe_ref[...]
        v_ref[:, hh * LANES:(hh + 1) * LANES] = acc[:, base + LANES:base + 2 * LANES].astype(v_ref.dtype)


def _mla_kv(c, g_ckv, kpe, w_ukv, gk, cos_t, sin_a, sin_b, tm=512, heads_per_step=4):
    n = c.shape[0]
    tn = heads_per_step * 2 * LANES
    ncols = w_ukv.shape[1]
    cblk = Q_LORA // KV_LORA
    row = lambda i, j: (i, 0)
    return pl.pallas_call(
        functools.partial(_mla_kv_kernel, heads=heads_per_step),
        grid=(n // tm, ncols // tn),
        in_specs=[pl.BlockSpec((tm, KV_LORA), lambda i, j: (i, cblk)),
                  pl.BlockSpec((1, KV_LORA), lambda i, j: (0, 0)),
                  pl.BlockSpec((tm, LANES), row),
                  pl.BlockSpec((KV_LORA, tn), lambda i, j: (0, j)),
                  pl.BlockSpec((1, 2 * LANES), lambda i, j: (0, 0)),
                  pl.BlockSpec((tm, LANES), row), pl.BlockSpec((tm, LANES), row), pl.BlockSpec((tm, LANES), row)],
        out_specs=[pl.BlockSpec((tm, tn), lambda i, j: (i, j)),
                   pl.BlockSpec((tm, tn // 2), lambda i, j: (i, j))],
        out_shape=[jax.ShapeDtypeStruct((n, ncols), BF16),
                   jax.ShapeDtypeStruct((n, ncols // 2), BF16)],
        scratch_shapes=[pltpu.VMEM((tm, KV_LORA), BF16), pltpu.VMEM((tm, LANES), BF16)],
        compiler_params=_cp(("parallel", "arbitrary")),
        name="mla_kv_proj",
    )(c, g_ckv.reshape(1, KV_LORA), kpe, w_ukv, gk, cos_t, sin_a, sin_b)


def _attn_b_kernel(q_ref, k_ref, v_ref, o_ref, *, heads):
    i = pl.program_id(2)
    tq = q_ref.shape[0]
    qw = 2 * LANES
    qs = [q_ref[:, h * qw:(h + 1) * qw] for h in range(heads)]

    def step(first_blk, width, carry, masked):
        start = pl.multiple_of(first_blk * ATT_BLOCK, ATT_BLOCK)
        if masked:
            per_blk = ATT_BLOCK // CHUNK
            r = i * per_blk + lax.broadcasted_iota(I32, (tq, width), 0) // CHUNK
            c = first_blk * per_blk + lax.broadcasted_iota(I32, (tq, width), 1) // CHUNK
            keep = c <= r
        out = []
        for h in range(heads):
            m, l, acc = carry[h]
            s = lax.dot_general(qs[h], k_ref[pl.ds(start, width), h * qw:(h + 1) * qw], _NT,
                                preferred_element_type=F32)
            if masked:
                s = jnp.where(keep, s, NEG)
            m_new = jnp.maximum(m, s.max(-1, keepdims=True))
            alpha = jnp.exp(m - m_new)
            p = jnp.exp(s - m_new)
            l = alpha * l + p.sum(-1, keepdims=True)
            acc = alpha * acc + jnp.dot(p.astype(BF16), v_ref[pl.ds(start, width), h * V_DIM:(h + 1) * V_DIM],
                                        preferred_element_type=F32)
            out.append((m_new, l, acc))
        return tuple(out)

    def finish(final):
        for h in range(heads):
            _, l, acc = final[h]
            o_ref[:, h * V_DIM:(h + 1) * V_DIM] = (acc / l).astype(o_ref.dtype)

    init = tuple((jnp.full((tq, 1), NEG, F32), jnp.zeros((tq, 1), F32), jnp.zeros((tq, V_DIM), F32))
                 for _ in range(heads))
    carry = lax.fori_loop(0, i // 2, lambda j, c: step(2 * j, 2 * ATT_BLOCK, c, False), init)

    @pl.when(i % 2 == 1)
    def _():
        finish(step(i - 1, 2 * ATT_BLOCK, carry, True))

    @pl.when(i % 2 == 0)
    def _():
        finish(step(i, ATT_BLOCK, carry, True))


def _attn_b(qf, kf, vb, batch, seq, heads_per_step=4):
    n = qf.shape[0]
    nq = seq // ATT_BLOCK
    qw = heads_per_step * 2 * LANES
    vw = heads_per_step * V_DIM
    return pl.pallas_call(
        functools.partial(_attn_b_kernel, heads=heads_per_step),
        grid=(batch, HEADS_B // heads_per_step, nq),
        in_specs=[pl.BlockSpec((ATT_BLOCK, qw), lambda b, g, i: (b * nq + i, g)),
                  pl.BlockSpec((seq, qw), lambda b, g, i: (b, g)),
                  pl.BlockSpec((seq, vw), lambda b, g, i: (b, g))],
        out_specs=pl.BlockSpec((ATT_BLOCK, vw), lambda b, g, i: (b * nq + i, g)),
        out_shape=jax.ShapeDtypeStruct((n, HEADS_B * V_DIM), BF16),
        compiler_params=_cp(("parallel", "parallel", "arbitrary")),
        name="attn_latent",
    )(qf, kf, vb)


def _out_proj_kernel(oa_ref, ob_ref, w_ref, x_ref, o_ref):
    ka = oa_ref.shape[1]
    acc = jnp.dot(oa_ref[...], w_ref[:ka, :].astype(BF16), preferred_element_type=F32)
    acc = acc + jnp.dot(ob_ref[...], w_ref[ka:, :].astype(BF16), preferred_element_type=F32)
    o_ref[...] = x_ref[...] + acc


def _out_proj(oa, ob, w_o, x, tm=512, tn=512):
    n, ka = oa.shape
    kb = ob.shape[1]
    d = w_o.shape[1]
    return pl.pallas_call(
        _out_proj_kernel,
        grid=(d // tn, n // tm),
        in_specs=[pl.BlockSpec((tm, ka), lambda j, i: (i, 0)),
                  pl.BlockSpec((tm, kb), lambda j, i: (i, 0)),
                  pl.BlockSpec((ka + kb, tn), lambda j, i: (0, j)),
                  pl.BlockSpec((tm, tn), lambda j, i: (i, j))],
        out_specs=pl.BlockSpec((tm, tn), lambda j, i: (i, j)),
        out_shape=jax.ShapeDtypeStruct((n, d), F32),
        compiler_params=_cp(("parallel", "arbitrary")),
        name="out_proj",
    )(oa, ob, w_o, x)


def _normed_proj_kernel(x_ref, g_ref, w_ref, gs_ref, o_ref, *, norm_cols):
    h = _rms(x_ref[...], g_ref[...]).astype(BF16)
    acc = jnp.dot(h, w_ref[...].astype(BF16), preferred_element_type=F32)
    for c in range(acc.shape[1] // LANES):
        sl = slice(c * LANES, (c + 1) * LANES)
        if c * LANES < norm_cols:
            o_ref[:, sl] = _rms(acc[:, sl], gs_ref[:, sl]).astype(o_ref.dtype)
        else:
            o_ref[:, sl] = acc[:, sl].astype(o_ref.dtype)


def _normed_proj(x, g, w, gs, norm_cols, name, tm=256):
    n, d = x.shape
    nc = w.shape[1]
    return pl.pallas_call(
        functools.partial(_normed_proj_kernel, norm_cols=norm_cols),
        grid=(n // tm,),
        in_specs=[pl.BlockSpec((tm, d), lambda i: (i, 0)),
                  pl.BlockSpec((1, d), lambda i: (0, 0)),
                  pl.BlockSpec((d, nc), lambda i: (0, 0)),
                  pl.BlockSpec((1, nc), lambda i: (0, 0))],
        out_specs=pl.BlockSpec((tm, nc), lambda i: (i, 0)),
        out_shape=jax.ShapeDtypeStruct((n, nc), BF16),
        compiler_params=_cp(("parallel",)),
        name=name,
    )(x, g.reshape(1, d), w, gs.reshape(1, nc))


def _cross_out_kernel(q_ref, k_ref, v_ref, w_ref, x_ref, o_ref):
    outs = []
    for hh in range(MEM_HEADS):
        sl = slice(hh * MEM_HEAD_DIM, (hh + 1) * MEM_HEAD_DIM)
        s = lax.dot_general(q_ref[:, sl], k_ref[:, sl], _NT, preferred_element_type=F32)
        p = jnp.exp(s - s.max(-1, keepdims=True))
        o = jnp.dot(p.astype(BF16), v_ref[:, sl], preferred_element_type=F32)
        outs.append((o / p.sum(-1, keepdims=True)).astype(BF16))
    acc = jnp.zeros(x_ref.shape, F32)
    for hh in range(MEM_HEADS):
        sl = slice(hh * MEM_HEAD_DIM, (hh + 1) * MEM_HEAD_DIM)
        acc = acc + jnp.dot(outs[hh], w_ref[sl, :].astype(BF16), preferred_element_type=F32)
    o_ref[...] = x_ref[...] + acc


def _cross_out(qx, kx, vx, w_xo, x, batch, seq, mem_tokens, tm=256):
    n, d = x.shape
    per_b = seq // tm
    hw = MEM_HEADS * MEM_HEAD_DIM
    return pl.pallas_call(
        _cross_out_kernel,
        grid=(batch, per_b),
        in_specs=[pl.BlockSpec((tm, hw), lambda b, i: (b * per_b + i, 0)),
                  pl.BlockSpec((mem_tokens, hw), lambda b, i: (b, 0)),
                  pl.BlockSpec((mem_tokens, hw), lambda b, i: (b, 1)),
                  pl.BlockSpec((hw, d), lambda b, i: (0, 0)),
                  pl.BlockSpec((tm, d), lambda b, i: (b * per_b + i, 0))],
        out_specs=pl.BlockSpec((tm, d), lambda b, i: (b * per_b + i, 0)),
        out_shape=jax.ShapeDtypeStruct((n, d), F32),
        compiler_params=_cp(("parallel", "arbitrary")),
        name="cross_attn_out",
    )(qx, kx, vx, w_xo, x)


def _router_kernel(x_ref, g_ref, wr_ref, b_ref, tri_ref, hp_ref, eid_ref, gate_ref, rank_ref, cnt_ref,
                   run_ref, scr_ref):
    t = pl.program_id(0)

    @pl.when(t == 0)
    def _():
        run_ref[...] = jnp.zeros_like(run_ref)

    h = _rms(x_ref[...], g_ref[...])
    _pack_store(hp_ref, 0, h, scr_ref)
    tm = h.shape[0]
    per_g = N_EXPERTS // N_GROUPS

    logits = lax.dot_general(wr_ref[...], h, _NT, precision=lax.Precision.HIGHEST,
                             preferred_element_type=F32)
    scores = 1.0 / (1.0 + jnp.exp(-logits))
    choice = scores + b_ref[...]

    sub = lax.broadcasted_iota(I32, (per_g, tm), 0).astype(F32)
    rows = []
    for g in range(N_GROUPS):
        c = choice[g * per_g:(g + 1) * per_g, :]
        m1 = c.max(0, keepdims=True)
        first = jnp.where(c == m1, sub, float(per_g)).min(0, keepdims=True)
        m2 = jnp.where(sub == first, -jnp.inf, c).max(0, keepdims=True)
        rows.append(m1 + m2)
    gs = jnp.concatenate(rows, axis=0)

    gsub = lax.broadcasted_iota(I32, (N_GROUPS, tm), 0).astype(F32)
    beaten = jnp.zeros((N_GROUPS, tm), F32)
    for g2 in range(N_GROUPS):
        row = gs[g2:g2 + 1, :]
        wins = (row > gs) | ((row == gs) & (gsub > float(g2)))
        beaten = beaten + jnp.where(wins, 1.0, 0.0)
    g_ok = jnp.where(beaten < TOPK_GROUPS, 1.0, 0.0)
    e_ok = jnp.concatenate(
        [jnp.broadcast_to(g_ok[g:g + 1, :], (per_g, tm)) for g in range(N_GROUPS)], axis=0)

    eiota = lax.broadcasted_iota(I32, (N_EXPERTS, tm), 0).astype(F32)
    masked = jnp.where(e_ok > 0.5, choice, -jnp.inf)
    chosen = jnp.zeros((N_EXPERTS, tm), F32)
    eids, ws = [], []
    for _ in range(TOP_K):
        m = masked.max(0, keepdims=True)
        idx = jnp.where(masked == m, eiota, float(N_EXPERTS)).min(0, keepdims=True)
        sel = eiota == idx
        ws.append(jnp.where(sel, scores, 0.0).sum(0, keepdims=True))
        masked = jnp.where(sel, -jnp.inf, masked)
        chosen = jnp.where(sel, 1.0, chosen)
        eids.append(idx)
    wsum = ws[0]
    for w in ws[1:]:
        wsum = wsum + w
    denom = wsum + 1e-20

    pos = jnp.dot(chosen.astype(BF16), tri_ref[...], preferred_element_type=F32) + run_ref[:, 0:1]
    run_ref[...] = run_ref[...] + chosen.sum(1, keepdims=True)
    cnt_ref[...] = run_ref[...].astype(I32)

    for r in range(TOP_K):
        eid_ref[r:r + 1, :] = eids[r].astype(I32)
        gate_ref[r:r + 1, :] = ws[r] / denom * ROUTED_SCALE
        rank_ref[r:r + 1, :] = jnp.where(eiota == eids[r], pos, 0.0).sum(0, keepdims=True).astype(I32)


def _router(x, g, w_router, router_bias, tm=512):
    n, d = x.shape
    ne = w_router.shape[1]
    tri = (jnp.arange(tm)[:, None] < jnp.arange(tm)[None, :]).astype(BF16)
    row8 = lambda i: (0, i)
    pitch = d // 2 // LANES
    return pl.pallas_call(
        _router_kernel,
        grid=(n // tm,),
        in_specs=[pl.BlockSpec((tm, d), lambda i: (i, 0)),
                  pl.BlockSpec((1, d), lambda i: (0, 0)),
                  pl.BlockSpec((ne, d), lambda i: (0, 0)),
                  pl.BlockSpec((ne, 1), lambda i: (0, 0)),
                  pl.BlockSpec((tm, tm), lambda i: (0, 0))],
        out_specs=[pl.BlockSpec((tm * pitch, LANES), lambda i: (i, 0)),
                   pl.BlockSpec((TOP_K, tm), row8), pl.BlockSpec((TOP_K, tm), row8),
                   pl.BlockSpec((TOP_K, tm), row8),
                   pl.BlockSpec((ne, LANES), lambda i: (0, 0))],
        out_shape=[jax.ShapeDtypeStruct((n * pitch, LANES), U32),
                   jax.ShapeDtypeStruct((TOP_K, n), I32), jax.ShapeDtypeStruct((TOP_K, n), F32),
                   jax.ShapeDtypeStruct((TOP_K, n), I32),
                   jax.ShapeDtypeStruct((ne, LANES), I32)],
        scratch_shapes=[pltpu.VMEM((ne, LANES), F32), pltpu.VMEM((2 * tm * pitch, LANES), F32)],
        compiler_params=_cp(("arbitrary",)),
        name="router",
    )(x, g.reshape(1, d), w_router.T, router_bias.reshape(ne, 1), tri)


def _dispatch_kernel(cnt_ref, start_ref, dest_ref, hp_ref, xs_ref, sem, *, tm, pitch):
    t = pl.program_id(0)
    pairs = TOP_K * tm

    def row_copy(r, slot):
        src = hp_ref.at[pl.ds(pl.multiple_of(r * pitch, pitch), pitch), :]
        dst = xs_ref.at[pl.ds(pl.multiple_of(slot * pitch, pitch), pitch), :]
        return pltpu.make_async_copy(src, dst, sem)

    def issue(g, c):
        for u in range(DMA_UNROLL):
            p = g * DMA_UNROLL + u
            row_copy(p & (tm - 1), dest_ref[0, 0, p]).start(priority=u % 2)
        return c

    lax.fori_loop(0, pairs // DMA_UNROLL, issue, 0)
    for _ in range(TOP_K):
        pltpu.make_async_copy(hp_ref, xs_ref.at[pl.ds(0, tm * pitch), :], sem).wait()

    @pl.when(t == pl.num_programs(0) - 1)
    def _():
        def per_expert(e, c):
            used = cnt_ref[e]
            first = start_ref[e] * STEP_ROWS + used
            npad = (-used) & (EXPERT_ROWS - 1)

            def pad_issue(p, c2):
                row_copy(0, first + p).start()
                return c2

            def pad_drain(p, c2):
                row_copy(0, 0).wait()
                return c2

            lax.fori_loop(0, npad, pad_issue, 0)
            lax.fori_loop(0, npad, pad_drain, 0)
            return c

        lax.fori_loop(0, N_EXPERTS, per_expert, 0)


def _dispatch(hp, dest, counts, step_start, p_rows, pitch, tm=256):
    n = hp.shape[0] // pitch
    tiles = n // tm
    dest_t = dest.reshape(TOP_K, tiles, tm).transpose(1, 0, 2).reshape(tiles, 1, TOP_K * tm)
    grid_spec = pltpu.PrefetchScalarGridSpec(
        num_scalar_prefetch=2,
        grid=(tiles,),
        in_specs=[pl.BlockSpec((1, 1, TOP_K * tm), lambda i, *_: (i, 0, 0), memory_space=pltpu.SMEM),
                  pl.BlockSpec((tm * pitch, LANES), lambda i, *_: (i, 0))],
        out_specs=pl.BlockSpec(memory_space=pl.ANY),
        scratch_shapes=[pltpu.SemaphoreType.DMA(())],
    )
    return pl.pallas_call(
        functools.partial(_dispatch_kernel, tm=tm, pitch=pitch),
        grid_spec=grid_spec,
        out_shape=jax.ShapeDtypeStruct((p_rows * pitch, LANES), U32),
        compiler_params=_cp(("arbitrary",)),
        name="moe_dispatch",
    )(counts, step_start, dest_t, hp)


def _silu_mul(g, u):
    return g / (1.0 + jnp.exp(-g)) * u


def _expert_weights(sched_ref, w_hbm_refs, wbuf_ref, sem):
    b = pl.program_id(0)

    def copies(e, slot):
        return [pltpu.make_async_copy(w.at[e], wbuf_ref.at[slot, i], sem.at[slot])
                for i, w in enumerate(w_hbm_refs)]

    @pl.when(b == 0)
    def _():
        for c in copies(sched_ref[0, 0], 0):
            c.start()

    slot = sched_ref[3, b]

    @pl.when(sched_ref[2, b] == 1)
    def _():
        for c in copies(0, slot):
            c.wait()
        nxt = sched_ref[4, b]

        @pl.when(nxt >= 0)
        def _():
            for c in copies(nxt, 1 - slot):
                c.start()

    return slot


def _expert_up_kernel(sched_ref, nu_ref, xs_ref, wg_hbm, wu_hbm, o_ref, wbuf_ref, scr_ref, sem):
    b = pl.program_id(0)
    pitch = xs_ref.shape[0] // STEP_ROWS
    slot = _expert_weights(sched_ref, (wg_hbm, wu_hbm), wbuf_ref, sem)

    def sub_block(j):
        x = _load_unpacked(xs_ref, j * EXPERT_ROWS, EXPERT_ROWS, pitch, scr_ref)
        g = jnp.dot(x, wbuf_ref[slot, 0], preferred_element_type=F32)
        u = jnp.dot(x, wbuf_ref[slot, 1], preferred_element_type=F32)
        o_ref[j * EXPERT_ROWS:(j + 1) * EXPERT_ROWS, :] = _silu_mul(g, u).astype(o_ref.dtype)

    for j in range(STEP_ROWS // EXPERT_ROWS):
        pl.when(sched_ref[1, b] > j)(functools.partial(sub_block, j))


def _row_block(b, sched, nu):
    return (jnp.minimum(b, nu[0] - 1), 0)


def _expert_up(xs, w_gate, w_up, sched, n_used, n_steps, pitch):
    _, d, ff = w_gate.shape
    grid_spec = pltpu.PrefetchScalarGridSpec(
        num_scalar_prefetch=2,
        grid=(n_steps,),
        in_specs=[pl.BlockSpec((STEP_ROWS * pitch, LANES), _row_block),
                  pl.BlockSpec(memory_space=pl.ANY), pl.BlockSpec(memory_space=pl.ANY)],
        out_specs=pl.BlockSpec((STEP_ROWS, ff), _row_block),
        scratch_shapes=[pltpu.VMEM((2, 2, d, ff), F32),
                        pltpu.VMEM((2 * EXPERT_ROWS * pitch, LANES), F32),
                        pltpu.SemaphoreType.DMA((2,))],
    )
    return pl.pallas_call(
        _expert_up_kernel,
        grid_spec=grid_spec,
        out_shape=jax.ShapeDtypeStruct((n_steps * STEP_ROWS, ff), BF16),
        compiler_params=_cp(("arbitrary",)),
        name="expert_up",
    )(sched, n_used, xs, w_gate, w_up)


def _expert_down_kernel(sched_ref, nu_ref, h_ref, wd_hbm, o_ref, wbuf_ref, scr_ref, sem):
    b = pl.program_id(0)
    slot = _expert_weights(sched_ref, (wd_hbm,), wbuf_ref, sem)

    def sub_block(j):
        y = jnp.dot(h_ref[j * EXPERT_ROWS:(j + 1) * EXPERT_ROWS, :], wbuf_ref[slot, 0].astype(BF16),
                    preferred_element_type=F32)
        _pack_store(o_ref, j * EXPERT_ROWS, y, scr_ref)

    for j in range(STEP_ROWS // EXPERT_ROWS):
        pl.when(sched_ref[1, b] > j)(functools.partial(sub_block, j))


def _expert_down(hs, w_down, sched, n_used, n_steps, pitch):
    p_rows, ff = hs.shape
    d = w_down.shape[2]
    grid_spec = pltpu.PrefetchScalarGridSpec(
        num_scalar_prefetch=2,
        grid=(n_steps,),
        in_specs=[pl.BlockSpec((STEP_ROWS, ff), _row_block), pl.BlockSpec(memory_space=pl.ANY)],
        out_specs=pl.BlockSpec((STEP_ROWS * pitch, LANES), _row_block),
        scratch_shapes=[pltpu.VMEM((2, 1, ff, d), F32),
                        pltpu.VMEM((2 * EXPERT_ROWS * pitch, LANES), F32),
                        pltpu.SemaphoreType.DMA((2,))],
    )
    return pl.pallas_call(
        _expert_down_kernel,
        grid_spec=grid_spec,
        out_shape=jax.ShapeDtypeStruct((p_rows * pitch, LANES), U32),
        compiler_params=_cp(("arbitrary",)),
        name="expert_down",
    )(sched, n_used, hs, w_down)


def _shared_up_kernel(hp_ref, wg_ref, wu_ref, o_ref, scr_ref):
    tm = o_ref.shape[0]
    x = _load_unpacked(hp_ref, 0, tm, hp_ref.shape[0] // tm, scr_ref)
    g = jnp.dot(x, wg_ref[...], preferred_element_type=F32)
    u = jnp.dot(x, wu_ref[...], preferred_element_type=F32)
    o_ref[...] = _silu_mul(g, u).astype(o_ref.dtype)


def _shared_up(hp, w_gate, w_up, pitch, tm=256):
    n = hp.shape[0] // pitch
    d, ff = w_gate.shape
    return pl.pallas_call(
        _shared_up_kernel,
        grid=(n // tm,),
        in_specs=[pl.BlockSpec((tm * pitch, LANES), lambda i: (i, 0)),
                  pl.BlockSpec((d, ff), lambda i: (0, 0)), pl.BlockSpec((d, ff), lambda i: (0, 0))],
        out_specs=pl.BlockSpec((tm, ff), lambda i: (i, 0)),
        out_shape=jax.ShapeDtypeStruct((n, ff), BF16),
        scratch_shapes=[pltpu.VMEM((2 * tm * pitch, LANES), F32)],
        compiler_params=_cp(("parallel",)),
        name="shared_up",
    )(hp, w_gate, w_up)


def _shared_down_kernel(t_ref, w_ref, x_ref, o_ref):
    o_ref[...] = x_ref[...] + jnp.dot(t_ref[...], w_ref[...].astype(BF16), preferred_element_type=F32)


def _shared_down(t, w_down, x, tm=256):
    n, ff = t.shape
    d = w_down.shape[1]
    return pl.pallas_call(
        _shared_down_kernel,
        grid=(n // tm,),
        in_specs=[pl.BlockSpec((tm, ff), lambda i: (i, 0)),
                  pl.BlockSpec((ff, d), lambda i: (0, 0)),
                  pl.BlockSpec((tm, d), lambda i: (i, 0))],
        out_specs=pl.BlockSpec((tm, d), lambda i: (i, 0)),
        out_shape=jax.ShapeDtypeStruct((n, d), F32),
        compiler_params=_cp(("parallel",)),
        name="shared_down",
    )(t, w_down, x)


def _combine_kernel(dfirst_ref, dnext_ref, gate_ref, base_ref, ys_ref, o_ref, buf_ref, scr_ref, sem, *,
                    tm, pitch):
    t = pl.program_id(0)
    pairs = TOP_K * tm
    slot_rows = pairs * pitch

    def issue_tile(dref, slot):
        def row_copy(p):
            src = ys_ref.at[pl.ds(pl.multiple_of(dref[0, 0, p] * pitch, pitch), pitch), :]
            dst = buf_ref.at[pl.ds(pl.multiple_of(slot * slot_rows + p * pitch, pitch), pitch), :]
            return pltpu.make_async_copy(src, dst, sem.at[slot])

        def body(g, c):
            for u in range(DMA_UNROLL):
                row_copy(g * DMA_UNROLL + u).start(priority=u % 2)
            return c

        lax.fori_loop(0, pairs // DMA_UNROLL, body, 0)

    @pl.when(t == 0)
    def _():
        issue_tile(dfirst_ref, 0)

    @pl.when(t + 1 < pl.num_programs(0))
    def _():
        issue_tile(dnext_ref, (t + 1) % 2)

    cur = t % 2
    for _ in range(TOP_K):
        pltpu.make_async_copy(ys_ref.at[pl.ds(0, tm * pitch), :], buf_ref.at[pl.ds(0, tm * pitch), :],
                              sem.at[cur]).wait()

    for grp in range(tm // COMBINE_GROUP):
        r0 = grp * COMBINE_GROUP
        rows = slice(r0, r0 + COMBINE_GROUP)
        gates = [jnp.broadcast_to(gate_ref[2 * r0:2 * (r0 + COMBINE_GROUP), k:k + 1], (2 * COMBINE_GROUP, LANES))
                 for k in range(TOP_K)]
        for s in range(pitch):
            acc = None
            for k in range(TOP_K):
                term = gates[k] * _load_pairs(buf_ref, cur * pairs + k * tm + r0, COMBINE_GROUP, pitch, s)
                acc = term if acc is None else acc + term
            even, odd = _split_pairs(acc, scr_ref, grp * pitch + s)
            c0 = slice((2 * s) * LANES, (2 * s + 1) * LANES)
            c1 = slice((2 * s + 1) * LANES, (2 * s + 2) * LANES)
            o_ref[rows, c0] = base_ref[rows, c0] + even
            o_ref[rows, c1] = base_ref[rows, c1] + odd


def _combine(dest, gate_rows, base, ys, pitch, tm=128):
    n, d = base.shape
    tiles = n // tm
    dest_t = dest.reshape(TOP_K, tiles, tm).transpose(1, 0, 2).reshape(tiles, 1, TOP_K * tm)
    dspec = lambda imap: pl.BlockSpec((1, 1, TOP_K * tm), imap, memory_space=pltpu.SMEM)
    return pl.pallas_call(
        functools.partial(_combine_kernel, tm=tm, pitch=pitch),
        grid=(tiles,),
        in_specs=[dspec(lambda i: (0, 0, 0)),
                  dspec(lambda i: (jnp.minimum(i + 1, tiles - 1), 0, 0)),
                  pl.BlockSpec((2 * tm, TOP_K), lambda i: (i, 0)),
                  pl.BlockSpec((tm, d), lambda i: (i, 0)),
                  pl.BlockSpec(memory_space=pl.ANY)],
        out_specs=pl.BlockSpec((tm, d), lambda i: (i, 0)),
        out_shape=jax.ShapeDtypeStruct((n, d), F32),
        scratch_shapes=[pltpu.VMEM((2 * TOP_K * tm * pitch, LANES), U32),
                        pltpu.VMEM((2 * tm * pitch, LANES), F32),
                        pltpu.SemaphoreType.DMA((2,))],
        compiler_params=_cp(("arbitrary",)),
        name="moe_combine",
    )(dest_t, dest_t, gate_rows, base, ys)


def _rope_tables(positions):
    half = ROPE_DIM // 2
    inv_freq = ROPE_THETA ** (-jnp.arange(0, ROPE_DIM, 2, dtype=F32) / ROPE_DIM)
    ang = positions.reshape(-1).astype(F32)[:, None] * inv_freq
    cos, sin = jnp.cos(ang), jnp.sin(ang)
    n = ang.shape[0]
    z = lambda w: jnp.zeros((n, w), F32)
    cos_t = jnp.concatenate([cos, cos, z(LANES - ROPE_DIM)], axis=1)
    sin_a = jnp.concatenate([-sin, z(LANES - half)], axis=1)
    sin_b = jnp.concatenate([z(half), sin, z(LANES - ROPE_DIM)], axis=1)
    return cos_t, sin_a, sin_b


def _pad_lanes(v, width):
    return jnp.concatenate([v, jnp.zeros((width - v.shape[0],), v.dtype)])


def kernel(x, mem, positions, g_mix, w_in, g_qa, g_ka, rel_bias, g_cq, w_uq, g_ckv, w_ukv, g_qb, g_kb, w_o, g_cross, g_mem, w_xq, w_xkv, g_qx, g_kx, w_xo, g_ffn, w_router, router_bias, w_sh_gate, w_sh_up, w_sh_down, w_ex_gate, w_ex_up, w_ex_down):
    batch, seq, d = x.shape
    n = batch * seq
    mem_tokens = mem.shape[1]
    width_a = HEADS_A * HEAD_DIM
    x2d = x.reshape(n, d)

    h = _norm_bf16(x2d, g_mix)
    gs_qk = jnp.concatenate([jnp.tile(g_qa * HEAD_DIM ** -0.5, HEADS_A), jnp.tile(g_ka, HEADS_A)])
    w_in_t = w_in.T
    qk = _in_proj(h, w_in_t, 0, 2 * width_a, BF16, "in_proj_qk", gs=gs_qk)
    v_a = _in_proj(h, w_in_t, 2 * width_a, width_a, BF16, "in_proj_v")
    c = _in_proj(h, w_in_t, 3 * width_a, Q_LORA + KV_LORA, F32, "in_proj_lora")
    w_kpe = jnp.pad(w_in_t[3 * width_a + Q_LORA + KV_LORA:], ((0, LANES - ROPE_DIM), (0, 0)))
    kpe = _in_proj(h, w_kpe, 0, LANES, F32, "in_proj_kpe")

    o_a = _attn_a(qk, v_a, _band_bias(rel_bias), batch, seq)

    cos_t, sin_a, sin_b = _rope_tables(positions)
    w_uq_pad = jnp.pad(w_uq.reshape(Q_LORA, HEADS_B, QK_B),
                       ((0, 0), (0, 0), (0, 2 * LANES - QK_B))).reshape(Q_LORA, HEADS_B * 2 * LANES)
    gq = _pad_lanes(g_qb, 2 * LANES).reshape(1, 2 * LANES)
    gk = _pad_lanes(g_kb, 2 * LANES).reshape(1, 2 * LANES)
    qf = _mla_q(c, g_cq, w_uq_pad, gq, cos_t, sin_a, sin_b)
    kf, v_b = _mla_kv(c, g_ckv, kpe, w_ukv, gk, cos_t, sin_a, sin_b)
    o_b = _attn_b(qf, kf, v_b, batch, seq)

    x1 = _out_proj(o_a, o_b, w_o, x2d)

    hw = MEM_HEADS * MEM_HEAD_DIM
    qx = _normed_proj(x1, g_cross, w_xq, jnp.tile(g_qx * MEM_HEAD_DIM ** -0.5, MEM_HEADS), hw, "cross_q")
    kvx = _normed_proj(mem.reshape(batch * mem_tokens, d), g_mem, w_xkv,
                       jnp.concatenate([jnp.tile(g_kx, MEM_HEADS), jnp.ones((hw,), F32)]), hw, "cross_kv")
    x2 = _cross_out(qx, kvx, kvx, w_xo, x1, batch, seq, mem_tokens)

    hp, eid, gate, rank, cnt = _router(x2, g_ffn, w_router, router_bias)
    counts = cnt[:, 0]
    nstep = (counts + STEP_ROWS - 1) // STEP_ROWS
    step_end = jnp.cumsum(nstep).astype(I32)
    step_start = step_end - nstep
    n_steps = n * TOP_K // STEP_ROWS + N_EXPERTS
    experts = jnp.arange(N_EXPERTS, dtype=I32)
    start_of = jnp.sum(jnp.where(eid[:, :, None] == experts, step_start, 0), axis=-1)
    dest = start_of * STEP_ROWS + rank
    steps = jnp.arange(n_steps, dtype=I32)
    step_e = jnp.minimum(jnp.sum((step_end[None, :] <= steps[:, None]).astype(I32), axis=1), N_EXPERTS - 1)
    mine = step_e[:, None] == experts[None, :]
    rows_left = jnp.sum(jnp.where(mine, counts[None, :] - STEP_ROWS * (steps[:, None] - step_start[None, :]), 0),
                        axis=1)
    used = steps < step_end[-1]
    rows_here = jnp.where(used, jnp.clip(rows_left, 0, STEP_ROWS), 0)
    nsub = (rows_here + EXPERT_ROWS - 1) // EXPERT_ROWS
    nonempty = counts > 0
    ring_slot = (jnp.cumsum(nonempty.astype(I32)) - 1) % 2
    later = (experts[None, :] > experts[:, None]) & nonempty[None, :]
    succ = jnp.min(jnp.where(later, experts[None, :], N_EXPERTS), axis=1)
    succ = jnp.where(succ == N_EXPERTS, -1, succ)
    per_step = lambda v: jnp.sum(jnp.where(mine, v[None, :], 0), axis=1)
    first = (used & (steps == per_step(step_start))).astype(I32)
    sched = jnp.stack([step_e, nsub, first, per_step(ring_slot), per_step(succ)]).astype(I32)
    n_used = step_end[-1:]
    pitch = d // 2 // LANES

    xs = _dispatch(hp, dest, counts, step_start, n_steps * STEP_ROWS, pitch)
    hs = _expert_up(xs, w_ex_gate, w_ex_up, sched, n_used, n_steps, pitch)
    ys = _expert_down(hs, w_ex_down, sched, n_used, n_steps, pitch)

    base = _shared_down(_shared_up(hp, w_sh_gate, w_sh_up, pitch), w_sh_down, x2)
    out = _combine(dest, jnp.repeat(gate.T, 2, axis=0), base, ys, pitch)
    return out.reshape(batch, seq, d)
```

```python
import functools

import jax
import jax.numpy as jnp
import numpy as np
from jax import lax
from jax.experimental import pallas as pl
from jax.experimental.pallas import tpu as pltpu

F32 = jnp.float32
BF16 = jnp.bfloat16
I32 = jnp.int32
U32 = jnp.uint32

CHUNK = 64
LEFT_CHUNKS = 8
REL_CLIP = 128
HEAD_DIM = 128
HEADS_A = 16
HEADS_B = 16
Q_LORA = 1024
KV_LORA = 512
NOPE_DIM = 128
ROPE_DIM = 64
V_DIM = 128
QK_B = NOPE_DIM + ROPE_DIM
ROPE_THETA = 10000.0
MEM_HEADS = 4
MEM_HEAD_DIM = 128
N_EXPERTS = 64
N_GROUPS = 8
TOPK_GROUPS = 4
TOP_K = 8
ROUTED_SCALE = 2.5
EPS = 1e-6

LANES = 128
SUBLANES = 8
ATT_BLOCK = 256
EXPERT_ROWS = 256
STEP_ROWS = 2 * EXPERT_ROWS
DMA_UNROLL = 8
COMBINE_GROUP = 16
NEG = -1e30
VMEM_LIMIT = 56 * 1024 * 1024

_NT = (((1,), (1,)), ((), ()))


def _cp(sem, vmem=VMEM_LIMIT):
    return pltpu.CompilerParams(dimension_semantics=sem, vmem_limit_bytes=vmem)


def _rms(x, g):
    return x * lax.rsqrt(jnp.mean(x * x, axis=-1, keepdims=True) + EPS) * g


def _pack_store(ref, first_token, x, scr):
    m, w = x.shape
    pitch = w // (2 * LANES)
    regions = scr.shape[0] // (2 * m)
    for s in range(pitch):
        base = (s % regions) * 2 * m
        scr[pl.ds(base, m, stride=2), :] = x[:, (2 * s) * LANES:(2 * s + 1) * LANES]
        scr[pl.ds(base + 1, m, stride=2), :] = x[:, (2 * s + 1) * LANES:(2 * s + 2) * LANES]
        z = scr[pl.ds(base, 2 * m), :].astype(BF16)
        ref[pl.ds(first_token * pitch + s, m, stride=pitch), :] = pltpu.bitcast(z, U32)


def _load_pairs(ref, first_token, m, pitch, s):
    w = ref[pl.ds(first_token * pitch + s, m, stride=pitch), :]
    return pltpu.bitcast(w, BF16).astype(F32)


def _split_pairs(z, scr, region):
    m = z.shape[0] // 2
    base = region * 2 * m
    scr[pl.ds(base, 2 * m), :] = z
    return scr[pl.ds(base, m, stride=2), :], scr[pl.ds(base + 1, m, stride=2), :]


def _load_unpacked(ref, first_token, m, pitch, scr, dtype=F32):
    regions = scr.shape[0] // (2 * m)
    cols = []
    for s in range(pitch):
        for piece in _split_pairs(_load_pairs(ref, first_token, m, pitch, s), scr, s % regions):
            cols.append(piece.astype(dtype))
    return jnp.concatenate(cols, axis=1)


def _norm_kernel(x_ref, g_ref, o_ref):
    o_ref[...] = _rms(x_ref[...], g_ref[...]).astype(o_ref.dtype)


def _norm_bf16(x, g, tm=256):
    n, d = x.shape
    return pl.pallas_call(
        _norm_kernel,
        grid=(n // tm,),
        in_specs=[pl.BlockSpec((tm, d), lambda i: (i, 0)), pl.BlockSpec((1, d), lambda i: (0, 0))],
        out_specs=pl.BlockSpec((tm, d), lambda i: (i, 0)),
        out_shape=jax.ShapeDtypeStruct((n, d), BF16),
        compiler_params=_cp(("parallel",)),
        name="norm_mix",
    )(x, g.reshape(1, d))


def _in_proj_kernel(*refs, head_norm, tn):
    if head_norm:
        h_ref, w_ref, gs_ref, o_ref = refs
    else:
        h_ref, w_ref, o_ref = refs
    acc = lax.dot_general(h_ref[...], w_ref[...].astype(BF16), _NT, preferred_element_type=F32)
    if head_norm:
        for c in range(tn // LANES):
            sl = slice(c * LANES, (c + 1) * LANES)
            o_ref[:, sl] = _rms(acc[:, sl], gs_ref[:, sl]).astype(o_ref.dtype)
    else:
        o_ref[...] = acc.astype(o_ref.dtype)


def _in_proj(h, w_t, col0, ncols, out_dtype, name, gs=None, tm=1024, tn=512):
    n, d = h.shape
    tn = min(tn, ncols)
    j0 = col0 // tn
    in_specs = [pl.BlockSpec((tm, d), lambda i, j: (i, 0)),
                pl.BlockSpec((tn, d), lambda i, j: (j0 + j, 0))]
    args = [h, w_t]
    if gs is not None:
        in_specs.append(pl.BlockSpec((1, tn), lambda i, j: (0, j)))
        args.append(gs.reshape(1, ncols))
    return pl.pallas_call(
        functools.partial(_in_proj_kernel, head_norm=gs is not None, tn=tn),
        grid=(n // tm, ncols // tn),
        in_specs=in_specs,
        out_specs=pl.BlockSpec((tm, tn), lambda i, j: (i, j)),
        out_shape=jax.ShapeDtypeStruct((n, ncols), out_dtype),
        compiler_params=_cp(("parallel", "arbitrary")),
        name=name,
    )(*args)


def _attn_a_kernel(q_ref, k0_ref, k1_ref, k2_ref, v0_ref, v1_ref, v2_ref, b_ref, o_ref, *, heads):
    i = pl.program_id(2)
    k_refs = (k0_ref, k1_ref, k2_ref)
    v_refs = (v0_ref, v1_ref, v2_ref)
    for hh in range(heads):
        sl = slice(hh * HEAD_DIM, (hh + 1) * HEAD_DIM)
        q = q_ref[:, sl]
        s = []
        for d in range(3):
            sd = lax.dot_general(q, k_refs[d][:, sl], _NT, preferred_element_type=F32)
            sd = sd + b_ref[hh, :, d * ATT_BLOCK:(d + 1) * ATT_BLOCK]
            if d > 0:
                sd = jnp.where(i >= d, sd, NEG)
            s.append(sd)
        m = jnp.maximum(jnp.maximum(s[0].max(-1, keepdims=True), s[1].max(-1, keepdims=True)),
                        s[2].max(-1, keepdims=True))
        l = jnp.zeros_like(m)
        o = jnp.zeros((q.shape[0], HEAD_DIM), F32)
        for d in range(3):
            p = jnp.exp(s[d] - m)
            l = l + p.sum(-1, keepdims=True)
            o = o + jnp.dot(p.astype(BF16), v_refs[d][:, sl], preferred_element_type=F32)
        o_ref[:, sl] = (o / l).astype(o_ref.dtype)


def _band_bias(rel_bias):
    blk = ATT_BLOCK
    period = 2 * blk
    heads = rel_bias.shape[0]
    r = np.arange(blk)[:, None]
    c = np.arange(blk)[None, :]
    per_blk = blk // CHUNK
    k = np.arange(period)
    delta = np.where(k < blk, k, k - period)
    tiles = []
    for d in range(3):
        idx = np.clip(blk * d - delta, -(CHUNK - 1), REL_CLIP) + CHUNK - 1
        w = rel_bias[:, idx].astype(F32)
        b = jnp.tile(w, (1, blk))[:, :blk * (period - 1)].reshape(heads, blk, period - 1)[:, :, :blk]
        cdiff = per_blk * d + r // CHUNK - c // CHUNK
        valid = (cdiff >= 0) & (cdiff <= LEFT_CHUNKS)
        tiles.append(jnp.where(valid[None], b, NEG))
    return jnp.concatenate(tiles, axis=-1)


def _attn_a(qk, v, bias, batch, seq, heads_per_step=4):
    n = qk.shape[0]
    nq = seq // ATT_BLOCK
    hw = heads_per_step * HEAD_DIM
    groups = HEADS_A // heads_per_step
    kcol0 = HEADS_A * HEAD_DIM // hw

    def kspec(d, col0):
        return pl.BlockSpec((ATT_BLOCK, hw), lambda g, b, i: (b * nq + jnp.maximum(i - d, 0), col0 + g))

    return pl.pallas_call(
        functools.partial(_attn_a_kernel, heads=heads_per_step),
        grid=(groups, batch, nq),
        in_specs=[pl.BlockSpec((ATT_BLOCK, hw), lambda g, b, i: (b * nq + i, g)),
                  kspec(0, kcol0), kspec(1, kcol0), kspec(2, kcol0),
                  kspec(0, 0), kspec(1, 0), kspec(2, 0),
                  pl.BlockSpec((heads_per_step, ATT_BLOCK, 3 * ATT_BLOCK), lambda g, b, i: (g, 0, 0))],
        out_specs=pl.BlockSpec((ATT_BLOCK, hw), lambda g, b, i: (b * nq + i, g)),
        out_shape=jax.ShapeDtypeStruct((n, HEADS_A * HEAD_DIM), BF16),
        compiler_params=_cp(("parallel", "parallel", "arbitrary")),
        name="attn_band",
    )(qk, qk, qk, qk, v, v, v, bias)


def _rope_pe(pe, g, cos_ref, sa_ref, sb_ref):
    ss = jnp.sum(pe * pe, axis=-1, keepdims=True) * (1.0 / ROPE_DIM)
    pn = pe * lax.rsqrt(ss + EPS) * g
    half = ROPE_DIM // 2
    return (pn * cos_ref[...] + pltpu.roll(pn, LANES - half, 1) * sa_ref[...]
            + pltpu.roll(pn, half, 1) * sb_ref[...])


def _mla_q_kernel(c_ref, gc_ref, w_ref, gq_ref, cos_ref, sa_ref, sb_ref, o_ref, xn_ref, *, heads, scale):
    @pl.when(pl.program_id(1) == 0)
    def _():
        xn_ref[...] = _rms(c_ref[...], gc_ref[...]).astype(BF16)

    acc = jnp.dot(xn_ref[...], w_ref[...].astype(BF16), preferred_element_type=F32)
    for hh in range(heads):
        base = hh * 2 * LANES
        nope = acc[:, base:base + LANES]
        pe = acc[:, base + LANES:base + 2 * LANES]
        o_ref[:, base:base + LANES] = (_rms(nope, gq_ref[:, :LANES]) * scale).astype(o_ref.dtype)
        o_ref[:, base + LANES:base + 2 * LANES] = (
            _rope_pe(pe, gq_ref[:, LANES:], cos_ref, sa_ref, sb_ref) * scale).astype(o_ref.dtype)


def _mla_q(c, g_cq, w_uq_pad, gq, cos_t, sin_a, sin_b, tm=512, heads_per_step=4):
    n = c.shape[0]
    tn = heads_per_step * 2 * LANES
    ncols = w_uq_pad.shape[1]
    row = lambda i, j: (i, 0)
    return pl.pallas_call(
        functools.partial(_mla_q_kernel, heads=heads_per_step, scale=QK_B ** -0.5),
        grid=(n // tm, ncols // tn),
        in_specs=[pl.BlockSpec((tm, Q_LORA), row),
                  pl.BlockSpec((1, Q_LORA), lambda i, j: (0, 0)),
                  pl.BlockSpec((Q_LORA, tn), lambda i, j: (0, j)),
                  pl.BlockSpec((1, 2 * LANES), lambda i, j: (0, 0)),
                  pl.BlockSpec((tm, LANES), row), pl.BlockSpec((tm, LANES), row), pl.BlockSpec((tm, LANES), row)],
        out_specs=pl.BlockSpec((tm, tn), lambda i, j: (i, j)),
        out_shape=jax.ShapeDtypeStruct((n, ncols), BF16),
        scratch_shapes=[pltpu.VMEM((tm, Q_LORA), BF16)],
        compiler_params=_cp(("parallel", "arbitrary")),
        name="mla_q_proj",
    )(c, g_cq.reshape(1, Q_LORA), w_uq_pad, gq, cos_t, sin_a, sin_b)


def _mla_kv_kernel(c_ref, gc_ref, kpe_ref, w_ref, gk_ref, cos_ref, sa_ref, sb_ref, k_ref, v_ref,
                   xn_ref, pe_ref, *, heads):
    @pl.when(pl.program_id(1) == 0)
    def _():
        xn_ref[...] = _rms(c_ref[...], gc_ref[...]).astype(BF16)
        pe_ref[...] = _rope_pe(kpe_ref[...], gk_ref[:, LANES:], cos_ref, sa_ref, sb_ref).astype(BF16)

    acc = jnp.dot(xn_ref[...], w_ref[...].astype(BF16), preferred_element_type=F32)
    for hh in range(heads):
        base = hh * 2 * LANES
        k_ref[:, base:base + LANES] = _rms(acc[:, base:base + LANES], gk_ref[:, :LANES]).astype(k_ref.dtype)
        k_ref[:, base + LANES:base + 2 * LANES] = pe_ref[...]
        v_ref[:, hh * LANES:(hh + 1) * LANES] = acc[:, base + LANES:base + 2 * LANES].astype(v_ref.dtype)


def _mla_kv(c, g_ckv, kpe, w_ukv, gk, cos_t, sin_a, sin_b, tm=512, heads_per_step=4):
    n = c.shape[0]
    tn = heads_per_step * 2 * LANES
    ncols = w_ukv.shape[1]
    cblk = Q_LORA // KV_LORA
    row = lambda i, j: (i, 0)
    return pl.pallas_call(
        functools.partial(_mla_kv_kernel, heads=heads_per_step),
        grid=(n // tm, ncols // tn),
        in_specs=[pl.BlockSpec((tm, KV_LORA), lambda i, j: (i, cblk)),
                  pl.BlockSpec((1, KV_LORA), lambda i, j: (0, 0)),
                  pl.BlockSpec((tm, LANES), row),
                  pl.BlockSpec((KV_LORA, tn), lambda i, j: (0, j)),
                  pl.BlockSpec((1, 2 * LANES), lambda i, j: (0, 0)),
                  pl.BlockSpec((tm, LANES), row), pl.BlockSpec((tm, LANES), row), pl.BlockSpec((tm, LANES), row)],
        out_specs=[pl.BlockSpec((tm, tn), lambda i, j: (i, j)),
                   pl.BlockSpec((tm, tn // 2), lambda i, j: (i, j))],
        out_shape=[jax.ShapeDtypeStruct((n, ncols), BF16),
                   jax.ShapeDtypeStruct((n, ncols // 2), BF16)],
        scratch_shapes=[pltpu.VMEM((tm, KV_LORA), BF16), pltpu.VMEM((tm, LANES), BF16)],
        compiler_params=_cp(("parallel", "arbitrary")),
        name="mla_kv_proj",
    )(c, g_ckv.reshape(1, KV_LORA), kpe, w_ukv, gk, cos_t, sin_a, sin_b)


def _attn_b_kernel(q_ref, k_ref, v_ref, o_ref, *, heads):
    i = pl.program_id(2)
    tq = q_ref.shape[0]
    qw = 2 * LANES
    qs = [q_ref[:, h * qw:(h + 1) * qw] for h in range(heads)]

    def step(first_blk, width, carry, masked):
        start = pl.multiple_of(first_blk * ATT_BLOCK, ATT_BLOCK)
        if masked:
            per_blk = ATT_BLOCK // CHUNK
            r = i * per_blk + lax.broadcasted_iota(I32, (tq, width), 0) // CHUNK
            c = first_blk * per_blk + lax.broadcasted_iota(I32, (tq, width), 1) // CHUNK
            keep = c <= r
        out = []
        for h in range(heads):
            m, l, acc = carry[h]
            s = lax.dot_general(qs[h], k_ref[pl.ds(start, width), h * qw:(h + 1) * qw], _NT,
                                preferred_element_type=F32)
            if masked:
                s = jnp.where(keep, s, NEG)
            m_new = jnp.maximum(m, s.max(-1, keepdims=True))
            alpha = jnp.exp(m - m_new)
            p = jnp.exp(s - m_new)
            l = alpha * l + p.sum(-1, keepdims=True)
            acc = alpha * acc + jnp.dot(p.astype(BF16), v_ref[pl.ds(start, width), h * V_DIM:(h + 1) * V_DIM],
                                        preferred_element_type=F32)
            out.append((m_new, l, acc))
        return tuple(out)

    def finish(final):
        for h in range(heads):
            _, l, acc = final[h]
            o_ref[:, h * V_DIM:(h + 1) * V_DIM] = (acc / l).astype(o_ref.dtype)

    init = tuple((jnp.full((tq, 1), NEG, F32), jnp.zeros((tq, 1), F32), jnp.zeros((tq, V_DIM), F32))
                 for _ in range(heads))
    carry = lax.fori_loop(0, i // 2, lambda j, c: step(2 * j, 2 * ATT_BLOCK, c, False), init)

    @pl.when(i % 2 == 1)
    def _():
        finish(step(i - 1, 2 * ATT_BLOCK, carry, True))

    @pl.when(i % 2 == 0)
    def _():
        finish(step(i, ATT_BLOCK, carry, True))


def _attn_b(qf, kf, vb, batch, seq, heads_per_step=4):
    n = qf.shape[0]
    nq = seq // ATT_BLOCK
    qw = heads_per_step * 2 * LANES
    vw = heads_per_step * V_DIM
    return pl.pallas_call(
        functools.partial(_attn_b_kernel, heads=heads_per_step),
        grid=(batch, HEADS_B // heads_per_step, nq),
        in_specs=[pl.BlockSpec((ATT_BLOCK, qw), lambda b, g, i: (b * nq + i, g)),
                  pl.BlockSpec((seq, qw), lambda b, g, i: (b, g)),
                  pl.BlockSpec((seq, vw), lambda b, g, i: (b, g))],
        out_specs=pl.BlockSpec((ATT_BLOCK, vw), lambda b, g, i: (b * nq + i, g)),
        out_shape=jax.ShapeDtypeStruct((n, HEADS_B * V_DIM), BF16),
        compiler_params=_cp(("parallel", "parallel", "arbitrary")),
        name="attn_latent",
    )(qf, kf, vb)


def _out_proj_kernel(oa_ref, ob_ref, w_ref, x_ref, o_ref, wb_ref):
    @pl.when(pl.program_id(1) == 0)
    def _():
        wb_ref[...] = w_ref[...].astype(BF16)

    ka = oa_ref.shape[1]
    acc = jnp.dot(oa_ref[...], wb_ref[:ka, :], preferred_element_type=F32)
    acc = acc + jnp.dot(ob_ref[...], wb_ref[ka:, :], preferred_element_type=F32)
    o_ref[...] = x_ref[...] + acc


def _out_proj(oa, ob, w_o, x, tm=256, tn=1024):
    n, ka = oa.shape
    kb = ob.shape[1]
    d = w_o.shape[1]
    return pl.pallas_call(
        _out_proj_kernel,
        grid=(d // tn, n // tm),
        in_specs=[pl.BlockSpec((tm, ka), lambda j, i: (i, 0)),
                  pl.BlockSpec((tm, kb), lambda j, i: (i, 0)),
                  pl.BlockSpec((ka + kb, tn), lambda j, i: (0, j)),
                  pl.BlockSpec((tm, tn), lambda j, i: (i, j))],
        out_specs=pl.BlockSpec((tm, tn), lambda j, i: (i, j)),
        out_shape=jax.ShapeDtypeStruct((n, d), F32),
        scratch_shapes=[pltpu.VMEM((ka + kb, tn), BF16)],
        compiler_params=_cp(("parallel", "arbitrary")),
        name="out_proj",
    )(oa, ob, w_o, x)


def _normed_proj_kernel(x_ref, g_ref, w_ref, gs_ref, o_ref, wb_ref, *, norm_cols):
    @pl.when(pl.program_id(0) == 0)
    def _():
        wb_ref[...] = w_ref[...].astype(BF16)

    h = _rms(x_ref[...], g_ref[...]).astype(BF16)
    acc = jnp.dot(h, wb_ref[...], preferred_element_type=F32)
    for c in range(acc.shape[1] // LANES):
        sl = slice(c * LANES, (c + 1) * LANES)
        if c * LANES < norm_cols:
            o_ref[:, sl] = _rms(acc[:, sl], gs_ref[:, sl]).astype(o_ref.dtype)
        else:
            o_ref[:, sl] = acc[:, sl].astype(o_ref.dtype)


def _normed_proj(x, g, w, gs, norm_cols, name, tm=256):
    n, d = x.shape
    nc = w.shape[1]
    return pl.pallas_call(
        functools.partial(_normed_proj_kernel, norm_cols=norm_cols),
        grid=(n // tm,),
        in_specs=[pl.BlockSpec((tm, d), lambda i: (i, 0)),
                  pl.BlockSpec((1, d), lambda i: (0, 0)),
                  pl.BlockSpec((d, nc), lambda i: (0, 0)),
                  pl.BlockSpec((1, nc), lambda i: (0, 0))],
        out_specs=pl.BlockSpec((tm, nc), lambda i: (i, 0)),
        out_shape=jax.ShapeDtypeStruct((n, nc), BF16),
        scratch_shapes=[pltpu.VMEM((d, nc), BF16)],
        compiler_params=_cp(("arbitrary",)),
        name=name,
    )(x, g.reshape(1, d), w, gs.reshape(1, nc))


def _cross_out_kernel(q_ref, k_ref, v_ref, w_ref, x_ref, o_ref):
    outs = []
    for hh in range(MEM_HEADS):
        sl = slice(hh * MEM_HEAD_DIM, (hh + 1) * MEM_HEAD_DIM)
        s = lax.dot_general(q_ref[:, sl], k_ref[:, sl], _NT, preferred_element_type=F32)
        p = jnp.exp(s - s.max(-1, keepdims=True))
        o = jnp.dot(p.astype(BF16), v_ref[:, sl], preferred_element_type=F32)
        outs.append((o / p.sum(-1, keepdims=True)).astype(BF16))
    acc = jnp.zeros(x_ref.shape, F32)
    for hh in range(MEM_HEADS):
        sl = slice(hh * MEM_HEAD_DIM, (hh + 1) * MEM_HEAD_DIM)
        acc = acc + jnp.dot(outs[hh], w_ref[sl, :].astype(BF16), preferred_element_type=F32)
    o_ref[...] = x_ref[...] + acc


def _cross_out(qx, kx, vx, w_xo, x, batch, seq, mem_tokens, tm=256):
    n, d = x.shape
    per_b = seq // tm
    hw = MEM_HEADS * MEM_HEAD_DIM
    return pl.pallas_call(
        _cross_out_kernel,
        grid=(batch, per_b),
        in_specs=[pl.BlockSpec((tm, hw), lambda b, i: (b * per_b + i, 0)),
                  pl.BlockSpec((mem_tokens, hw), lambda b, i: (b, 0)),
                  pl.BlockSpec((mem_tokens, hw), lambda b, i: (b, 1)),
                  pl.BlockSpec((hw, d), lambda b, i: (0, 0)),
                  pl.BlockSpec((tm, d), lambda b, i: (b * per_b + i, 0))],
        out_specs=pl.BlockSpec((tm, d), lambda b, i: (b * per_b + i, 0)),
        out_shape=jax.ShapeDtypeStruct((n, d), F32),
        compiler_params=_cp(("parallel", "arbitrary")),
        name="cross_attn_out",
    )(qx, kx, vx, w_xo, x)


def _router_kernel(x_ref, g_ref, wr_ref, b_ref, tri_ref, hp_ref, eid_ref, gate_ref, rank_ref, cnt_ref,
                   run_ref, scr_ref):
    t = pl.program_id(0)

    @pl.when(t == 0)
    def _():
        run_ref[...] = jnp.zeros_like(run_ref)

    h = _rms(x_ref[...], g_ref[...])
    _pack_store(hp_ref, 0, h, scr_ref)
    tm = h.shape[0]
    per_g = N_EXPERTS // N_GROUPS

    logits = lax.dot_general(wr_ref[...], h, _NT, precision=lax.Precision.HIGHEST,
                             preferred_element_type=F32)
    scores = 1.0 / (1.0 + jnp.exp(-logits))
    choice = scores + b_ref[...]

    sub = lax.broadcasted_iota(I32, (per_g, tm), 0).astype(F32)
    rows = []
    for g in range(N_GROUPS):
        c = choice[g * per_g:(g + 1) * per_g, :]
        m1 = c.max(0, keepdims=True)
        first = jnp.where(c == m1, sub, float(per_g)).min(0, keepdims=True)
        m2 = jnp.where(sub == first, -jnp.inf, c).max(0, keepdims=True)
        rows.append(m1 + m2)
    gs = jnp.concatenate(rows, axis=0)

    gsub = lax.broadcasted_iota(I32, (N_GROUPS, tm), 0).astype(F32)
    beaten = jnp.zeros((N_GROUPS, tm), F32)
    for g2 in range(N_GROUPS):
        row = gs[g2:g2 + 1, :]
        wins = (row > gs) | ((row == gs) & (gsub > float(g2)))
        beaten = beaten + jnp.where(wins, 1.0, 0.0)
    g_ok = jnp.where(beaten < TOPK_GROUPS, 1.0, 0.0)
    e_ok = jnp.concatenate(
        [jnp.broadcast_to(g_ok[g:g + 1, :], (per_g, tm)) for g in range(N_GROUPS)], axis=0)

    eiota = lax.broadcasted_iota(I32, (N_EXPERTS, tm), 0).astype(F32)
    masked = jnp.where(e_ok > 0.5, choice, -jnp.inf)
    chosen = jnp.zeros((N_EXPERTS, tm), F32)
    eids, ws = [], []
    for _ in range(TOP_K):
        m = masked.max(0, keepdims=True)
        idx = jnp.where(masked == m, eiota, float(N_EXPERTS)).min(0, keepdims=True)
        sel = eiota == idx
        ws.append(jnp.where(sel, scores, 0.0).sum(0, keepdims=True))
        masked = jnp.where(sel, -jnp.inf, masked)
        chosen = jnp.where(sel, 1.0, chosen)
        eids.append(idx)
    wsum = ws[0]
    for w in ws[1:]:
        wsum = wsum + w
    denom = wsum + 1e-20

    pos = jnp.dot(chosen.astype(BF16), tri_ref[...], preferred_element_type=F32) + run_ref[:, 0:1]
    run_ref[...] = run_ref[...] + chosen.sum(1, keepdims=True)
    cnt_ref[...] = run_ref[...].astype(I32)

    for r in range(TOP_K):
        eid_ref[r:r + 1, :] = eids[r].astype(I32)
        gate_ref[r:r + 1, :] = ws[r] / denom * ROUTED_SCALE
        rank_ref[r:r + 1, :] = jnp.where(eiota == eids[r], pos, 0.0).sum(0, keepdims=True).astype(I32)


def _router(x, g, w_router, router_bias, tm=512):
    n, d = x.shape
    ne = w_router.shape[1]
    tri = (jnp.arange(tm)[:, None] < jnp.arange(tm)[None, :]).astype(BF16)
    row8 = lambda i: (0, i)
    pitch = d // 2 // LANES
    return pl.pallas_call(
        _router_kernel,
        grid=(n // tm,),
        in_specs=[pl.BlockSpec((tm, d), lambda i: (i, 0)),
                  pl.BlockSpec((1, d), lambda i: (0, 0)),
                  pl.BlockSpec((ne, d), lambda i: (0, 0)),
                  pl.BlockSpec((ne, 1), lambda i: (0, 0)),
                  pl.BlockSpec((tm, tm), lambda i: (0, 0))],
        out_specs=[pl.BlockSpec((tm * pitch, LANES), lambda i: (i, 0)),
                   pl.BlockSpec((TOP_K, tm), row8), pl.BlockSpec((TOP_K, tm), row8),
                   pl.BlockSpec((TOP_K, tm), row8),
                   pl.BlockSpec((ne, LANES), lambda i: (0, 0))],
        out_shape=[jax.ShapeDtypeStruct((n * pitch, LANES), U32),
                   jax.ShapeDtypeStruct((TOP_K, n), I32), jax.ShapeDtypeStruct((TOP_K, n), F32),
                   jax.ShapeDtypeStruct((TOP_K, n), I32),
                   jax.ShapeDtypeStruct((ne, LANES), I32)],
        scratch_shapes=[pltpu.VMEM((ne, LANES), F32), pltpu.VMEM((2 * tm * pitch, LANES), F32)],
        compiler_params=_cp(("arbitrary",)),
        name="router",
    )(x, g.reshape(1, d), w_router.T, router_bias.reshape(ne, 1), tri)


def _dispatch_kernel(cnt_ref, start_ref, dest_ref, hp_ref, xs_ref, sem, *, tm, pitch):
    t = pl.program_id(0)
    pairs = TOP_K * tm

    def row_copy(r, slot):
        src = hp_ref.at[pl.ds(pl.multiple_of(r * pitch, pitch), pitch), :]
        dst = xs_ref.at[pl.ds(pl.multiple_of(slot * pitch, pitch), pitch), :]
        return pltpu.make_async_copy(src, dst, sem)

    def issue(g, c):
        for u in range(DMA_UNROLL):
            p = g * DMA_UNROLL + u
            row_copy(p & (tm - 1), dest_ref[0, 0, p]).start(priority=u % 2)
        return c

    lax.fori_loop(0, pairs // DMA_UNROLL, issue, 0)
    for _ in range(TOP_K):
        pltpu.make_async_copy(hp_ref, xs_ref.at[pl.ds(0, tm * pitch), :], sem).wait()

    @pl.when(t == pl.num_programs(0) - 1)
    def _():
        def run_copy(first, rows):
            dst = xs_ref.at[pl.ds(pl.multiple_of(first * pitch, pitch), rows * pitch), :]
            return pltpu.make_async_copy(hp_ref.at[pl.ds(0, rows * pitch), :], dst, sem)

        def per_expert(e, c):
            used = cnt_ref[e]
            npad = (-used) & (EXPERT_ROWS - 1)
            first = start_ref[e] * STEP_ROWS + used
            sizes = [1 << k for k in range(EXPERT_ROWS.bit_length() - 1)]
            offs = []
            off = first
            for size in sizes:
                offs.append(off)
                off = off + (npad & size)
            for size, o in zip(sizes, offs):
                pl.when((npad & size) != 0)(lambda size=size, o=o: run_copy(o, size).start())
            for size in sizes:
                pl.when((npad & size) != 0)(lambda size=size: run_copy(0, size).wait())
            return c

        lax.fori_loop(0, N_EXPERTS, per_expert, 0)


def _dispatch(hp, dest, counts, step_start, p_rows, pitch, tm=256):
    n = hp.shape[0] // pitch
    tiles = n // tm
    dest_t = dest.reshape(TOP_K, tiles, tm).transpose(1, 0, 2).reshape(tiles, 1, TOP_K * tm)
    grid_spec = pltpu.PrefetchScalarGridSpec(
        num_scalar_prefetch=2,
        grid=(tiles,),
        in_specs=[pl.BlockSpec((1, 1, TOP_K * tm), lambda i, *_: (i, 0, 0), memory_space=pltpu.SMEM),
                  pl.BlockSpec((tm * pitch, LANES), lambda i, *_: (i, 0))],
        out_specs=pl.BlockSpec(memory_space=pl.ANY),
        scratch_shapes=[pltpu.SemaphoreType.DMA(())],
    )
    return pl.pallas_call(
        functools.partial(_dispatch_kernel, tm=tm, pitch=pitch),
        grid_spec=grid_spec,
        out_shape=jax.ShapeDtypeStruct((p_rows * pitch, LANES), U32),
        compiler_params=_cp(("arbitrary",)),
        name="moe_dispatch",
    )(counts, step_start, dest_t, hp)


def _silu_mul(g, u):
    return g / (1.0 + jnp.exp(-g)) * u


def _expert_weights(sched_ref, w_hbm_refs, wbuf_ref, sem):
    b = pl.program_id(0)

    def copies(e, slot):
        return [pltpu.make_async_copy(w.at[e], wbuf_ref.at[slot, i], sem.at[slot])
                for i, w in enumerate(w_hbm_refs)]

    @pl.when(b == 0)
    def _():
        for c in copies(sched_ref[0, 0], 0):
            c.start()

    slot = sched_ref[3, b]

    @pl.when(sched_ref[2, b] == 1)
    def _():
        for c in copies(0, slot):
            c.wait()
        nxt = sched_ref[4, b]

        @pl.when(nxt >= 0)
        def _():
            for c in copies(nxt, 1 - slot):
                c.start()

    return slot


def _expert_up_kernel(sched_ref, nu_ref, xs_ref, wg_hbm, wu_hbm, o_ref, wbuf_ref, scr_ref, sem):
    b = pl.program_id(0)
    pitch = xs_ref.shape[0] // STEP_ROWS
    slot = _expert_weights(sched_ref, (wg_hbm, wu_hbm), wbuf_ref, sem)

    def sub_block(j):
        x = _load_unpacked(xs_ref, j * EXPERT_ROWS, EXPERT_ROWS, pitch, scr_ref)
        g = jnp.dot(x, wbuf_ref[slot, 0], preferred_element_type=F32)
        u = jnp.dot(x, wbuf_ref[slot, 1], preferred_element_type=F32)
        o_ref[j * EXPERT_ROWS:(j + 1) * EXPERT_ROWS, :] = _silu_mul(g, u).astype(o_ref.dtype)

    for j in range(STEP_ROWS // EXPERT_ROWS):
        pl.when(sched_ref[1, b] > j)(functools.partial(sub_block, j))


def _row_block(b, sched, nu):
    return (jnp.minimum(b, nu[0] - 1), 0)


def _expert_up(xs, w_gate, w_up, sched, n_used, n_steps, pitch):
    _, d, ff = w_gate.shape
    grid_spec = pltpu.PrefetchScalarGridSpec(
        num_scalar_prefetch=2,
        grid=(n_steps,),
        in_specs=[pl.BlockSpec((STEP_ROWS * pitch, LANES), _row_block),
                  pl.BlockSpec(memory_space=pl.ANY), pl.BlockSpec(memory_space=pl.ANY)],
        out_specs=pl.BlockSpec((STEP_ROWS, ff), _row_block),
        scratch_shapes=[pltpu.VMEM((2, 2, d, ff), F32),
                        pltpu.VMEM((2 * EXPERT_ROWS * pitch, LANES), F32),
                        pltpu.SemaphoreType.DMA((2,))],
    )
    return pl.pallas_call(
        _expert_up_kernel,
        grid_spec=grid_spec,
        out_shape=jax.ShapeDtypeStruct((n_steps * STEP_ROWS, ff), BF16),
        compiler_params=_cp(("arbitrary",)),
        name="expert_up",
    )(sched, n_used, xs, w_gate, w_up)


def _expert_down_kernel(sched_ref, nu_ref, h_ref, wd_hbm, o_ref, wbuf_ref, scr_ref, sem):
    b = pl.program_id(0)
    slot = _expert_weights(sched_ref, (wd_hbm,), wbuf_ref, sem)

    def sub_block(j):
        y = jnp.dot(h_ref[j * EXPERT_ROWS:(j + 1) * EXPERT_ROWS, :], wbuf_ref[slot, 0].astype(BF16),
                    preferred_element_type=F32)
        _pack_store(o_ref, j * EXPERT_ROWS, y, scr_ref)

    for j in range(STEP_ROWS // EXPERT_ROWS):
        pl.when(sched_ref[1, b] > j)(functools.partial(sub_block, j))


def _expert_down(hs, w_down, sched, n_used, n_steps, pitch):
    p_rows, ff = hs.shape
    d = w_down.shape[2]
    grid_spec = pltpu.PrefetchScalarGridSpec(
        num_scalar_prefetch=2,
        grid=(n_steps,),
        in_specs=[pl.BlockSpec((STEP_ROWS, ff), _row_block), pl.BlockSpec(memory_space=pl.ANY)],
        out_specs=pl.BlockSpec((STEP_ROWS * pitch, LANES), _row_block),
        scratch_shapes=[pltpu.VMEM((2, 1, ff, d), F32),
                        pltpu.VMEM((2 * EXPERT_ROWS * pitch, LANES), F32),
                        pltpu.SemaphoreType.DMA((2,))],
    )
    return pl.pallas_call(
        _expert_down_kernel,
        grid_spec=grid_spec,
        out_shape=jax.ShapeDtypeStruct((p_rows * pitch, LANES), U32),
        compiler_params=_cp(("arbitrary",)),
        name="expert_down",
    )(sched, n_used, hs, w_down)


def _shared_up_kernel(hp_ref, wg_ref, wu_ref, o_ref, scr_ref, wgb_ref, wub_ref):
    @pl.when(pl.program_id(0) == 0)
    def _():
        wgb_ref[...] = wg_ref[...].astype(BF16)
        wub_ref[...] = wu_ref[...].astype(BF16)

    tm = o_ref.shape[0]
    x = _load_unpacked(hp_ref, 0, tm, hp_ref.shape[0] // tm, scr_ref, BF16)
    g = jnp.dot(x, wgb_ref[...], preferred_element_type=F32)
    u = jnp.dot(x, wub_ref[...], preferred_element_type=F32)
    o_ref[...] = _silu_mul(g, u).astype(o_ref.dtype)


def _shared_up(hp, w_gate, w_up, pitch, tm=256):
    n = hp.shape[0] // pitch
    d, ff = w_gate.shape
    return pl.pallas_call(
        _shared_up_kernel,
        grid=(n // tm,),
        in_specs=[pl.BlockSpec((tm * pitch, LANES), lambda i: (i, 0)),
                  pl.BlockSpec((d, ff), lambda i: (0, 0)), pl.BlockSpec((d, ff), lambda i: (0, 0))],
        out_specs=pl.BlockSpec((tm, ff), lambda i: (i, 0)),
        out_shape=jax.ShapeDtypeStruct((n, ff), BF16),
        scratch_shapes=[pltpu.VMEM((2 * tm * pitch, LANES), F32),
                        pltpu.VMEM((d, ff), BF16), pltpu.VMEM((d, ff), BF16)],
        compiler_params=_cp(("arbitrary",)),
        name="shared_up",
    )(hp, w_gate, w_up)


def _shared_down_kernel(t_ref, w_ref, x_ref, o_ref):
    o_ref[...] = x_ref[...] + jnp.dot(t_ref[...], w_ref[...].astype(BF16), preferred_element_type=F32)


def _shared_down(t, w_down, x, tm=256):
    n, ff = t.shape
    d = w_down.shape[1]
    return pl.pallas_call(
        _shared_down_kernel,
        grid=(n // tm,),
        in_specs=[pl.BlockSpec((tm, ff), lambda i: (i, 0)),
                  pl.BlockSpec((ff, d), lambda i: (0, 0)),
                  pl.BlockSpec((tm, d), lambda i: (i, 0))],
        out_specs=pl.BlockSpec((tm, d), lambda i: (i, 0)),
        out_shape=jax.ShapeDtypeStruct((n, d), F32),
        compiler_params=_cp(("parallel",)),
        name="shared_down",
    )(t, w_down, x)


def _combine_kernel(dfirst_ref, dnext_ref, gate_ref, base_ref, ys_ref, o_ref, buf_ref, scr_ref, sem, *,
                    tm, pitch):
    t = pl.program_id(0)
    pairs = TOP_K * tm
    bpitch = buf_ref.shape[0] // (2 * pairs)

    def row_copy(dref, slot, p):
        src = ys_ref.at[pl.ds(pl.multiple_of(dref[0, 0, p] * pitch, pitch), pitch), :]
        dst = buf_ref.at[pl.ds(pl.multiple_of((slot * pairs + p) * bpitch, SUBLANES), pitch), :]
        return pltpu.make_async_copy(src, dst, sem.at[slot])

    def wait_slot(slot):
        for _ in range(TOP_K):
            pltpu.make_async_copy(ys_ref.at[pl.ds(0, tm * pitch), :], buf_ref.at[pl.ds(0, tm * pitch), :],
                                  sem.at[slot]).wait()

    def issue_tile(dref, slot):
        def body(g, c):
            for u in range(DMA_UNROLL):
                row_copy(dref, slot, g * DMA_UNROLL + u).start(priority=u % 2)
            return c

        lax.fori_loop(0, pairs // DMA_UNROLL, body, 0)

    @pl.when(t == 0)
    def _():
        issue_tile(dfirst_ref, 0)

    @pl.when(t + 1 < pl.num_programs(0))
    def _():
        issue_tile(dnext_ref, (t + 1) % 2)

    cur = t % 2
    wait_slot(cur)

    for grp in range(tm // COMBINE_GROUP):
        r0 = grp * COMBINE_GROUP
        rows = slice(r0, r0 + COMBINE_GROUP)
        gates = [jnp.broadcast_to(gate_ref[2 * r0:2 * (r0 + COMBINE_GROUP), k:k + 1], (2 * COMBINE_GROUP, LANES))
                 for k in range(TOP_K)]
        for s in range(pitch):
            acc = None
            for k in range(TOP_K):
                term = gates[k] * _load_pairs(buf_ref, cur * pairs + k * tm + r0, COMBINE_GROUP, bpitch, s)
                acc = term if acc is None else acc + term
            even, odd = _split_pairs(acc, scr_ref, grp * pitch + s)
            c0 = slice((2 * s) * LANES, (2 * s + 1) * LANES)
            c1 = slice((2 * s + 1) * LANES, (2 * s + 2) * LANES)
            o_ref[rows, c0] = base_ref[rows, c0] + even
            o_ref[rows, c1] = base_ref[rows, c1] + odd


def _combine(dest, gate_rows, base, ys, pitch, tm=128):
    n, d = base.shape
    tiles = n // tm
    dest_t = dest.reshape(TOP_K, tiles, tm).transpose(1, 0, 2).reshape(tiles, 1, TOP_K * tm)
    dspec = lambda imap: pl.BlockSpec((1, 1, TOP_K * tm), imap, memory_space=pltpu.SMEM)
    bpitch = pitch if (pitch // SUBLANES) % 2 else pitch + SUBLANES
    return pl.pallas_call(
        functools.partial(_combine_kernel, tm=tm, pitch=pitch),
        grid=(tiles,),
        in_specs=[dspec(lambda i: (0, 0, 0)),
                  dspec(lambda i: (jnp.minimum(i + 1, tiles - 1), 0, 0)),
                  pl.BlockSpec((2 * tm, TOP_K), lambda i: (i, 0)),
                  pl.BlockSpec((tm, d), lambda i: (i, 0)),
                  pl.BlockSpec(memory_space=pl.ANY)],
        out_specs=pl.BlockSpec((tm, d), lambda i: (i, 0)),
        out_shape=jax.ShapeDtypeStruct((n, d), F32),
        scratch_shapes=[pltpu.VMEM((2 * TOP_K * tm * bpitch, LANES), U32),
                        pltpu.VMEM((2 * tm * pitch, LANES), F32),
                        pltpu.SemaphoreType.DMA((2,))],
        compiler_params=_cp(("arbitrary",)),
        name="moe_combine",
    )(dest_t, dest_t, gate_rows, base, ys)


def _rope_tables(positions):
    half = ROPE_DIM // 2
    inv_freq = ROPE_THETA ** (-jnp.arange(0, ROPE_DIM, 2, dtype=F32) / ROPE_DIM)
    ang = positions.reshape(-1).astype(F32)[:, None] * inv_freq
    cos, sin = jnp.cos(ang), jnp.sin(ang)
    n = ang.shape[0]
    z = lambda w: jnp.zeros((n, w), F32)
    cos_t = jnp.concatenate([cos, cos, z(LANES - ROPE_DIM)], axis=1)
    sin_a = jnp.concatenate([-sin, z(LANES - half)], axis=1)
    sin_b = jnp.concatenate([z(half), sin, z(LANES - ROPE_DIM)], axis=1)
    return cos_t, sin_a, sin_b


def _pad_lanes(v, width):
    return jnp.concatenate([v, jnp.zeros((width - v.shape[0],), v.dtype)])


def kernel(x, mem, positions, g_mix, w_in, g_qa, g_ka, rel_bias, g_cq, w_uq, g_ckv, w_ukv, g_qb, g_kb, w_o, g_cross, g_mem, w_xq, w_xkv, g_qx, g_kx, w_xo, g_ffn, w_router, router_bias, w_sh_gate, w_sh_up, w_sh_down, w_ex_gate, w_ex_up, w_ex_down):
    batch, seq, d = x.shape
    n = batch * seq
    mem_tokens = mem.shape[1]
    width_a = HEADS_A * HEAD_DIM
    x2d = x.reshape(n, d)

    h = _norm_bf16(x2d, g_mix)
    gs_qk = jnp.concatenate([jnp.tile(g_qa * HEAD_DIM ** -0.5, HEADS_A), jnp.tile(g_ka, HEADS_A)])
    w_in_t = w_in.T
    qk = _in_proj(h, w_in_t, 0, 2 * width_a, BF16, "in_proj_qk", gs=gs_qk)
    v_a = _in_proj(h, w_in_t, 2 * width_a, width_a, BF16, "in_proj_v")
    c = _in_proj(h, w_in_t, 3 * width_a, Q_LORA + KV_LORA, F32, "in_proj_lora")
    w_kpe = jnp.pad(w_in_t[3 * width_a + Q_LORA + KV_LORA:], ((0, LANES - ROPE_DIM), (0, 0)))
    kpe = _in_proj(h, w_kpe, 0, LANES, F32, "in_proj_kpe")

    o_a = _attn_a(qk, v_a, _band_bias(rel_bias), batch, seq)

    cos_t, sin_a, sin_b = _rope_tables(positions)
    w_uq_pad = jnp.pad(w_uq.reshape(Q_LORA, HEADS_B, QK_B),
                       ((0, 0), (0, 0), (0, 2 * LANES - QK_B))).reshape(Q_LORA, HEADS_B * 2 * LANES)
    gq = _pad_lanes(g_qb, 2 * LANES).reshape(1, 2 * LANES)
    gk = _pad_lanes(g_kb, 2 * LANES).reshape(1, 2 * LANES)
    qf = _mla_q(c, g_cq, w_uq_pad, gq, cos_t, sin_a, sin_b)
    kf, v_b = _mla_kv(c, g_ckv, kpe, w_ukv, gk, cos_t, sin_a, sin_b)
    o_b = _attn_b(qf, kf, v_b, batch, seq)

    x1 = _out_proj(o_a, o_b, w_o, x2d)

    hw = MEM_HEADS * MEM_HEAD_DIM
    qx = _normed_proj(x1, g_cross, w_xq, jnp.tile(g_qx * MEM_HEAD_DIM ** -0.5, MEM_HEADS), hw, "cross_q")
    kvx = _normed_proj(mem.reshape(batch * mem_tokens, d), g_mem, w_xkv,
                       jnp.concatenate([jnp.tile(g_kx, MEM_HEADS), jnp.ones((hw,), F32)]), hw, "cross_kv")
    x2 = _cross_out(qx, kvx, kvx, w_xo, x1, batch, seq, mem_tokens)

    hp, eid, gate, rank, cnt = _router(x2, g_ffn, w_router, router_bias)
    counts = cnt[:, 0]
    nstep = (counts + STEP_ROWS - 1) // STEP_ROWS
    step_end = jnp.cumsum(nstep).astype(I32)
    step_start = step_end - nstep
    n_steps = n * TOP_K // STEP_ROWS + N_EXPERTS
    experts = jnp.arange(N_EXPERTS, dtype=I32)
    start_of = jnp.sum(jnp.where(eid[:, :, None] == experts, step_start, 0), axis=-1)
    dest = start_of * STEP_ROWS + rank
    steps = jnp.arange(n_steps, dtype=I32)
    step_e = jnp.minimum(jnp.sum((step_end[None, :] <= steps[:, None]).astype(I32), axis=1), N_EXPERTS - 1)
    mine = step_e[:, None] == experts[None, :]
    rows_left = jnp.sum(jnp.where(mine, counts[None, :] - STEP_ROWS * (steps[:, None] - step_start[None, :]), 0),
                        axis=1)
    used = steps < step_end[-1]
    rows_here = jnp.where(used, jnp.clip(rows_left, 0, STEP_ROWS), 0)
    nsub = (rows_here + EXPERT_ROWS - 1) // EXPERT_ROWS
    nonempty = counts > 0
    ring_slot = (jnp.cumsum(nonempty.astype(I32)) - 1) % 2
    later = (experts[None, :] > experts[:, None]) & nonempty[None, :]
    succ = jnp.min(jnp.where(later, experts[None, :], N_EXPERTS), axis=1)
    succ = jnp.where(succ == N_EXPERTS, -1, succ)
    per_step = lambda v: jnp.sum(jnp.where(mine, v[None, :], 0), axis=1)
    first = (used & (steps == per_step(step_start))).astype(I32)
    sched = jnp.stack([step_e, nsub, first, per_step(ring_slot), per_step(succ)]).astype(I32)
    n_used = step_end[-1:]
    pitch = d // 2 // LANES

    xs = _dispatch(hp, dest, counts, step_start, n_steps * STEP_ROWS, pitch)
    hs = _expert_up(xs, w_ex_gate, w_ex_up, sched, n_used, n_steps, pitch)
    ys = _expert_down(hs, w_ex_down, sched, n_used, n_steps, pitch)

    base = _shared_down(_shared_up(hp, w_sh_gate, w_sh_up, pitch), w_sh_down, x2)
    out = _combine(dest, jnp.repeat(gate.T, 2, axis=0), base, ys, pitch)
    return out.reshape(batch, seq, d)
```

```python
import functools

import jax
import jax.numpy as jnp
import numpy as np
from jax import lax
from jax.experimental import pallas as pl
from jax.experimental.pallas import tpu as pltpu

F32 = jnp.float32
BF16 = jnp.bfloat16
I32 = jnp.int32
U32 = jnp.uint32

CHUNK = 64
LEFT_CHUNKS = 8
REL_CLIP = 128
HEAD_DIM = 128
HEADS_A = 16
HEADS_B = 16
Q_LORA = 1024
KV_LORA = 512
NOPE_DIM = 128
ROPE_DIM = 64
V_DIM = 128
QK_B = NOPE_DIM + ROPE_DIM
ROPE_THETA = 10000.0
MEM_HEADS = 4
MEM_HEAD_DIM = 128
N_EXPERTS = 64
N_GROUPS = 8
TOPK_GROUPS = 4
TOP_K = 8
ROUTED_SCALE = 2.5
EPS = 1e-6

LANES = 128
SUBLANES = 8
ATT_BLOCK = 256
EXPERT_ROWS = 256
STEP_ROWS = 2 * EXPERT_ROWS
DMA_UNROLL = 8
COMBINE_GROUP = 16
NEG = -1e30
VMEM_LIMIT = 56 * 1024 * 1024

_NT = (((1,), (1,)), ((), ()))


def _cp(sem, vmem=VMEM_LIMIT):
    return pltpu.CompilerParams(dimension_semantics=sem, vmem_limit_bytes=vmem)


def _rms(x, g):
    return x * lax.rsqrt(jnp.mean(x * x, axis=-1, keepdims=True) + EPS) * g


def _pack_store(ref, first_token, x, scr):
    m, w = x.shape
    pitch = w // (2 * LANES)
    regions = scr.shape[0] // (2 * m)
    for s in range(pitch):
        base = (s % regions) * 2 * m
        scr[pl.ds(base, m, stride=2), :] = x[:, (2 * s) * LANES:(2 * s + 1) * LANES]
        scr[pl.ds(base + 1, m, stride=2), :] = x[:, (2 * s + 1) * LANES:(2 * s + 2) * LANES]
        z = scr[pl.ds(base, 2 * m), :].astype(BF16)
        ref[pl.ds(first_token * pitch + s, m, stride=pitch), :] = pltpu.bitcast(z, U32)


def _load_pairs(ref, first_token, m, pitch, s):
    w = ref[pl.ds(first_token * pitch + s, m, stride=pitch), :]
    return pltpu.bitcast(w, BF16).astype(F32)


def _split_pairs(z, scr, region):
    m = z.shape[0] // 2
    base = region * 2 * m
    scr[pl.ds(base, 2 * m), :] = z
    return scr[pl.ds(base, m, stride=2), :], scr[pl.ds(base + 1, m, stride=2), :]


def _load_unpacked(ref, first_token, m, pitch, scr, dtype=F32):
    regions = scr.shape[0] // (2 * m)
    cols = []
    for s in range(pitch):
        for piece in _split_pairs(_load_pairs(ref, first_token, m, pitch, s), scr, s % regions):
            cols.append(piece.astype(dtype))
    return jnp.concatenate(cols, axis=1)


def _norm_kernel(x_ref, g_ref, o_ref):
    o_ref[...] = _rms(x_ref[...], g_ref[...]).astype(o_ref.dtype)


def _norm_bf16(x, g, tm=256):
    n, d = x.shape
    return pl.pallas_call(
        _norm_kernel,
        grid=(n // tm,),
        in_specs=[pl.BlockSpec((tm, d), lambda i: (i, 0)), pl.BlockSpec((1, d), lambda i: (0, 0))],
        out_specs=pl.BlockSpec((tm, d), lambda i: (i, 0)),
        out_shape=jax.ShapeDtypeStruct((n, d), BF16),
        compiler_params=_cp(("parallel",)),
        name="norm_mix",
    )(x, g.reshape(1, d))


def _in_proj_kernel(*refs, head_norm, tn):
    if head_norm:
        h_ref, w_ref, gs_ref, o_ref = refs
    else:
        h_ref, w_ref, o_ref = refs
    acc = lax.dot_general(h_ref[...], w_ref[...].astype(BF16), _NT, preferred_element_type=F32)
    if head_norm:
        for c in range(tn // LANES):
            sl = slice(c * LANES, (c + 1) * LANES)
            o_ref[:, sl] = _rms(acc[:, sl], gs_ref[:, sl]).astype(o_ref.dtype)
    else:
        o_ref[...] = acc.astype(o_ref.dtype)


def _in_proj(h, w_t, col0, ncols, out_dtype, name, gs=None, tm=1024, tn=512):
    n, d = h.shape
    tn = min(tn, ncols)
    j0 = col0 // tn
    in_specs = [pl.BlockSpec((tm, d), lambda i, j: (i, 0)),
                pl.BlockSpec((tn, d), lambda i, j: (j0 + j, 0))]
    args = [h, w_t]
    if gs is not None:
        in_specs.append(pl.BlockSpec((1, tn), lambda i, j: (0, j)))
        args.append(gs.reshape(1, ncols))
    return pl.pallas_call(
        functools.partial(_in_proj_kernel, head_norm=gs is not None, tn=tn),
        grid=(n // tm, ncols // tn),
        in_specs=in_specs,
        out_specs=pl.BlockSpec((tm, tn), lambda i, j: (i, j)),
        out_shape=jax.ShapeDtypeStruct((n, ncols), out_dtype),
        compiler_params=_cp(("parallel", "arbitrary")),
        name=name,
    )(*args)


def _attn_a_kernel(q_ref, k0_ref, k1_ref, k2_ref, v0_ref, v1_ref, v2_ref, b_ref, o_ref, *, heads):
    i = pl.program_id(2)
    k_refs = (k0_ref, k1_ref, k2_ref)
    v_refs = (v0_ref, v1_ref, v2_ref)
    for hh in range(heads):
        sl = slice(hh * HEAD_DIM, (hh + 1) * HEAD_DIM)
        q = q_ref[:, sl]
        s = []
        for d in range(3):
            sd = lax.dot_general(q, k_refs[d][:, sl], _NT, preferred_element_type=F32)
            sd = sd + b_ref[hh, :, d * ATT_BLOCK:(d + 1) * ATT_BLOCK]
            if d > 0:
                sd = jnp.where(i >= d, sd, NEG)
            s.append(sd)
        m = jnp.maximum(jnp.maximum(s[0].max(-1, keepdims=True), s[1].max(-1, keepdims=True)),
                        s[2].max(-1, keepdims=True))
        l = jnp.zeros_like(m)
        o = jnp.zeros((q.shape[0], HEAD_DIM), F32)
        for d in range(3):
            p = jnp.exp(s[d] - m)
            l = l + p.sum(-1, keepdims=True)
            o = o + jnp.dot(p.astype(BF16), v_refs[d][:, sl], preferred_element_type=F32)
        o_ref[:, sl] = (o / l).astype(o_ref.dtype)


def _band_bias(rel_bias):
    blk = ATT_BLOCK
    period = 2 * blk
    heads = rel_bias.shape[0]
    r = np.arange(blk)[:, None]
    c = np.arange(blk)[None, :]
    per_blk = blk // CHUNK
    k = np.arange(period)
    delta = np.where(k < blk, k, k - period)
    tiles = []
    for d in range(3):
        idx = np.clip(blk * d - delta, -(CHUNK - 1), REL_CLIP) + CHUNK - 1
        w = rel_bias[:, idx].astype(F32)
        b = jnp.tile(w, (1, blk))[:, :blk * (period - 1)].reshape(heads, blk, period - 1)[:, :, :blk]
        cdiff = per_blk * d + r // CHUNK - c // CHUNK
        valid = (cdiff >= 0) & (cdiff <= LEFT_CHUNKS)
        tiles.append(jnp.where(valid[None], b, NEG))
    return jnp.concatenate(tiles, axis=-1)


def _attn_a(qk, v, bias, batch, seq, heads_per_step=4):
    n = qk.shape[0]
    nq = seq // ATT_BLOCK
    hw = heads_per_step * HEAD_DIM
    groups = HEADS_A // heads_per_step
    kcol0 = HEADS_A * HEAD_DIM // hw

    def kspec(d, col0):
        return pl.BlockSpec((ATT_BLOCK, hw), lambda g, b, i: (b * nq + jnp.maximum(i - d, 0), col0 + g))

    return pl.pallas_call(
        functools.partial(_attn_a_kernel, heads=heads_per_step),
        grid=(groups, batch, nq),
        in_specs=[pl.BlockSpec((ATT_BLOCK, hw), lambda g, b, i: (b * nq + i, g)),
                  kspec(0, kcol0), kspec(1, kcol0), kspec(2, kcol0),
                  kspec(0, 0), kspec(1, 0), kspec(2, 0),
                  pl.BlockSpec((heads_per_step, ATT_BLOCK, 3 * ATT_BLOCK), lambda g, b, i: (g, 0, 0))],
        out_specs=pl.BlockSpec((ATT_BLOCK, hw), lambda g, b, i: (b * nq + i, g)),
        out_shape=jax.ShapeDtypeStruct((n, HEADS_A * HEAD_DIM), BF16),
        compiler_params=_cp(("parallel", "parallel", "arbitrary")),
        name="attn_band",
    )(qk, qk, qk, qk, v, v, v, bias)


def _rope_pe(pe, g, cos_ref, sa_ref, sb_ref):
    ss = jnp.sum(pe * pe, axis=-1, keepdims=True) * (1.0 / ROPE_DIM)
    pn = pe * lax.rsqrt(ss + EPS) * g
    half = ROPE_DIM // 2
    return (pn * cos_ref[...] + pltpu.roll(pn, LANES - half, 1) * sa_ref[...]
            + pltpu.roll(pn, half, 1) * sb_ref[...])


def _mla_q_kernel(c_ref, gc_ref, w_ref, gq_ref, cos_ref, sa_ref, sb_ref, o_ref, xn_ref, *, heads, scale):
    @pl.when(pl.program_id(1) == 0)
    def _():
        xn_ref[...] = _rms(c_ref[...], gc_ref[...]).astype(BF16)

    acc = jnp.dot(xn_ref[...], w_ref[...].astype(BF16), preferred_element_type=F32)
    for hh in range(heads):
        base = hh * 2 * LANES
        nope = acc[:, base:base + LANES]
        pe = acc[:, base + LANES:base + 2 * LANES]
        o_ref[:, base:base + LANES] = (_rms(nope, gq_ref[:, :LANES]) * scale).astype(o_ref.dtype)
        o_ref[:, base + LANES:base + 2 * LANES] = (
            _rope_pe(pe, gq_ref[:, LANES:], cos_ref, sa_ref, sb_ref) * scale).astype(o_ref.dtype)


def _mla_q(c, g_cq, w_uq_pad, gq, cos_t, sin_a, sin_b, tm=512, heads_per_step=4):
    n = c.shape[0]
    tn = heads_per_step * 2 * LANES
    ncols = w_uq_pad.shape[1]
    row = lambda i, j: (i, 0)
    return pl.pallas_call(
        functools.partial(_mla_q_kernel, heads=heads_per_step, scale=QK_B ** -0.5),
        grid=(n // tm, ncols // tn),
        in_specs=[pl.BlockSpec((tm, Q_LORA), row),
                  pl.BlockSpec((1, Q_LORA), lambda i, j: (0, 0)),
                  pl.BlockSpec((Q_LORA, tn), lambda i, j: (0, j)),
                  pl.BlockSpec((1, 2 * LANES), lambda i, j: (0, 0)),
                  pl.BlockSpec((tm, LANES), row), pl.BlockSpec((tm, LANES), row), pl.BlockSpec((tm, LANES), row)],
        out_specs=pl.BlockSpec((tm, tn), lambda i, j: (i, j)),
        out_shape=jax.ShapeDtypeStruct((n, ncols), BF16),
        scratch_shapes=[pltpu.VMEM((tm, Q_LORA), BF16)],
        compiler_params=_cp(("parallel", "arbitrary")),
        name="mla_q_proj",
    )(c, g_cq.reshape(1, Q_LORA), w_uq_pad, gq, cos_t, sin_a, sin_b)


def _mla_kv_kernel(c_ref, gc_ref, kpe_ref, w_ref, gk_ref, cos_ref, sa_ref, sb_ref, k_ref, v_ref,
                   xn_ref, pe_ref, *, heads):
    @pl.when(pl.program_id(1) == 0)
    def _():
        xn_ref[...] = _rms(c_ref[...], gc_ref[...]).astype(BF16)
        pe_ref[...] = _rope_pe(kpe_ref[...], gk_ref[:, LANES:], cos_ref, sa_ref, sb_ref).astype(BF16)

    acc = jnp.dot(xn_ref[...], w_ref[...].astype(BF16), preferred_element_type=F32)
    for hh in range(heads):
        base = hh * 2 * LANES
        k_ref[:, base:base + LANES] = _rms(acc[:, base:base + LANES], gk_ref[:, :LANES]).astype(k_ref.dtype)
        k_ref[:, base + LANES:base + 2 * LANES] = pe_ref[...]
        v_ref[:, hh * LANES:(hh + 1) * LANES] = acc[:, base + LANES:base + 2 * LANES].astype(v_ref.dtype)


def _mla_kv(c, g_ckv, kpe, w_ukv, gk, cos_t, sin_a, sin_b, tm=512, heads_per_step=4):
    n = c.shape[0]
    tn = heads_per_step * 2 * LANES
    ncols = w_ukv.shape[1]
    cblk = Q_LORA // KV_LORA
    row = lambda i, j: (i, 0)
    return pl.pallas_call(
        functools.partial(_mla_kv_kernel, heads=heads_per_step),
        grid=(n // tm, ncols // tn),
        in_specs=[pl.BlockSpec((tm, KV_LORA), lambda i, j: (i, cblk)),
                  pl.BlockSpec((1, KV_LORA), lambda i, j: (0, 0)),
                  pl.BlockSpec((tm, LANES), row),
                  pl.BlockSpec((KV_LORA, tn), lambda i, j: (0, j)),
                  pl.BlockSpec((1, 2 * LANES), lambda i, j: (0, 0)),
                  pl.BlockSpec((tm, LANES), row), pl.BlockSpec((tm, LANES), row), pl.BlockSpec((tm, LANES), row)],
        out_specs=[pl.BlockSpec((tm, tn), lambda i, j: (i, j)),
                   pl.BlockSpec((tm, tn // 2), lambda i, j: (i, j))],
        out_shape=[jax.ShapeDtypeStruct((n, ncols), BF16),
                   jax.ShapeDtypeStruct((n, ncols // 2), BF16)],
        scratch_shapes=[pltpu.VMEM((tm, KV_LORA), BF16), pltpu.VMEM((tm, LANES), BF16)],
        compiler_params=_cp(("parallel", "arbitrary")),
        name="mla_kv_proj",
    )(c, g_ckv.reshape(1, KV_LORA), kpe, w_ukv, gk, cos_t, sin_a, sin_b)


def _attn_b_kernel(q_ref, k_ref, v_ref, o_ref, *, heads):
    i = pl.program_id(2)
    tq = q_ref.shape[0]
    qw = 2 * LANES
    qs = [q_ref[:, h * qw:(h + 1) * qw] for h in range(heads)]

    def step(first_blk, width, carry, masked):
        start = pl.multiple_of(first_blk * ATT_BLOCK, ATT_BLOCK)
        if masked:
            per_blk = ATT_BLOCK // CHUNK
            r = i * per_blk + lax.broadcasted_iota(I32, (tq, width), 0) // CHUNK
            c = first_blk * per_blk + lax.broadcasted_iota(I32, (tq, width), 1) // CHUNK
            keep = c <= r
        out = []
        for h in range(heads):
            m, l, acc = carry[h]
            s = lax.dot_general(qs[h], k_ref[pl.ds(start, width), h * qw:(h + 1) * qw], _NT,
                                preferred_element_type=F32)
            if masked:
                s = jnp.where(keep, s, NEG)
            m_new = jnp.maximum(m, s.max(-1, keepdims=True))
            alpha = jnp.exp(m - m_new)
            p = jnp.exp(s - m_new)
            l = alpha * l + p.sum(-1, keepdims=True)
            acc = alpha * acc + jnp.dot(p.astype(BF16), v_ref[pl.ds(start, width), h * V_DIM:(h + 1) * V_DIM],
                                        preferred_element_type=F32)
            out.append((m_new, l, acc))
        return tuple(out)

    def finish(final):
        for h in range(heads):
            _, l, acc = final[h]
            o_ref[:, h * V_DIM:(h + 1) * V_DIM] = (acc / l).astype(o_ref.dtype)

    init = tuple((jnp.full((tq, 1), NEG, F32), jnp.zeros((tq, 1), F32), jnp.zeros((tq, V_DIM), F32))
                 for _ in range(heads))
    carry = lax.fori_loop(0, i // 2, lambda j, c: step(2 * j, 2 * ATT_BLOCK, c, False), init)

    @pl.when(i % 2 == 1)
    def _():
        finish(step(i - 1, 2 * ATT_BLOCK, carry, True))

    @pl.when(i % 2 == 0)
    def _():
        finish(step(i, ATT_BLOCK, carry, True))


def _attn_b(qf, kf, vb, batch, seq, heads_per_step=4):
    n = qf.shape[0]
    nq = seq // ATT_BLOCK
    qw = heads_per_step * 2 * LANES
    vw = heads_per_step * V_DIM
    return pl.pallas_call(
        functools.partial(_attn_b_kernel, heads=heads_per_step),
        grid=(batch, HEADS_B // heads_per_step, nq),
        in_specs=[pl.BlockSpec((ATT_BLOCK, qw), lambda b, g, i: (b * nq + i, g)),
                  pl.BlockSpec((seq, qw), lambda b, g, i: (b, g)),
                  pl.BlockSpec((seq, vw), lambda b, g, i: (b, g))],
        out_specs=pl.BlockSpec((ATT_BLOCK, vw), lambda b, g, i: (b * nq + i, g)),
        out_shape=jax.ShapeDtypeStruct((n, HEADS_B * V_DIM), BF16),
        compiler_params=_cp(("parallel", "parallel", "arbitrary")),
        name="attn_latent",
    )(qf, kf, vb)


def _out_proj_kernel(oa_ref, ob_ref, w_ref, x_ref, o_ref, wb_ref):
    @pl.when(pl.program_id(1) == 0)
    def _():
        wb_ref[...] = w_ref[...].astype(BF16)

    ka = oa_ref.shape[1]
    acc = jnp.dot(oa_ref[...], wb_ref[:ka, :], preferred_element_type=F32)
    acc = acc + jnp.dot(ob_ref[...], wb_ref[ka:, :], preferred_element_type=F32)
    o_ref[...] = x_ref[...] + acc


def _out_proj(oa, ob, w_o, x, tm=256, tn=1024):
    n, ka = oa.shape
    kb = ob.shape[1]
    d = w_o.shape[1]
    return pl.pallas_call(
        _out_proj_kernel,
        grid=(d // tn, n // tm),
        in_specs=[pl.BlockSpec((tm, ka), lambda j, i: (i, 0)),
                  pl.BlockSpec((tm, kb), lambda j, i: (i, 0)),
                  pl.BlockSpec((ka + kb, tn), lambda j, i: (0, j)),
                  pl.BlockSpec((tm, tn), lambda j, i: (i, j))],
        out_specs=pl.BlockSpec((tm, tn), lambda j, i: (i, j)),
        out_shape=jax.ShapeDtypeStruct((n, d), F32),
        scratch_shapes=[pltpu.VMEM((ka + kb, tn), BF16)],
        compiler_params=_cp(("parallel", "arbitrary")),
        name="out_proj",
    )(oa, ob, w_o, x)


def _normed_proj_kernel(x_ref, g_ref, w_ref, gs_ref, o_ref, wb_ref, *, norm_cols):
    @pl.when(pl.program_id(0) == 0)
    def _():
        wb_ref[...] = w_ref[...].astype(BF16)

    h = _rms(x_ref[...], g_ref[...]).astype(BF16)
    acc = jnp.dot(h, wb_ref[...], preferred_element_type=F32)
    for c in range(acc.shape[1] // LANES):
        sl = slice(c * LANES, (c + 1) * LANES)
        if c * LANES < norm_cols:
            o_ref[:, sl] = _rms(acc[:, sl], gs_ref[:, sl]).astype(o_ref.dtype)
        else:
            o_ref[:, sl] = acc[:, sl].astype(o_ref.dtype)


def _normed_proj(x, g, w, gs, norm_cols, name, tm=256):
    n, d = x.shape
    nc = w.shape[1]
    return pl.pallas_call(
        functools.partial(_normed_proj_kernel, norm_cols=norm_cols),
        grid=(n // tm,),
        in_specs=[pl.BlockSpec((tm, d), lambda i: (i, 0)),
                  pl.BlockSpec((1, d), lambda i: (0, 0)),
                  pl.BlockSpec((d, nc), lambda i: (0, 0)),
                  pl.BlockSpec((1, nc), lambda i: (0, 0))],
        out_specs=pl.BlockSpec((tm, nc), lambda i: (i, 0)),
        out_shape=jax.ShapeDtypeStruct((n, nc), BF16),
        scratch_shapes=[pltpu.VMEM((d, nc), BF16)],
        compiler_params=_cp(("arbitrary",)),
        name=name,
    )(x, g.reshape(1, d), w, gs.reshape(1, nc))


def _cross_kernel(x_ref, g_ref, wq_ref, gq_ref, k_ref, v_ref, wo_ref, o_ref):
    x = x_ref[...]
    q = jnp.dot(_rms(x, g_ref[...]).astype(BF16), wq_ref[...], preferred_element_type=F32)
    outs = []
    for hh in range(MEM_HEADS):
        sl = slice(hh * MEM_HEAD_DIM, (hh + 1) * MEM_HEAD_DIM)
        qh = _rms(q[:, sl], gq_ref[:, sl]).astype(BF16)
        s = lax.dot_general(qh, k_ref[:, sl], _NT, preferred_element_type=F32)
        p = jnp.exp(s - s.max(-1, keepdims=True))
        o = jnp.dot(p.astype(BF16), v_ref[:, sl], preferred_element_type=F32)
        outs.append((o / p.sum(-1, keepdims=True)).astype(BF16))
    acc = x
    for hh in range(MEM_HEADS):
        sl = slice(hh * MEM_HEAD_DIM, (hh + 1) * MEM_HEAD_DIM)
        acc = acc + jnp.dot(outs[hh], wo_ref[sl, :], preferred_element_type=F32)
    o_ref[...] = acc


def _cross_attn(x, g, w_xq, gq, kvx, w_xo, batch, seq, mem_tokens, tm=256):
    n, d = x.shape
    per_b = seq // tm
    hw = MEM_HEADS * MEM_HEAD_DIM
    const = lambda b, i: (0, 0)
    return pl.pallas_call(
        _cross_kernel,
        grid=(batch, per_b),
        in_specs=[pl.BlockSpec((tm, d), lambda b, i: (b * per_b + i, 0)),
                  pl.BlockSpec((1, d), const),
                  pl.BlockSpec((d, hw), const),
                  pl.BlockSpec((1, hw), const),
                  pl.BlockSpec((mem_tokens, hw), lambda b, i: (b, 0)),
                  pl.BlockSpec((mem_tokens, hw), lambda b, i: (b, 1)),
                  pl.BlockSpec((hw, d), const)],
        out_specs=pl.BlockSpec((tm, d), lambda b, i: (b * per_b + i, 0)),
        out_shape=jax.ShapeDtypeStruct((n, d), F32),
        compiler_params=_cp(("parallel", "arbitrary")),
        name="cross_attn",
    )(x, g.reshape(1, d), w_xq.astype(BF16), gq.reshape(1, hw), kvx, kvx, w_xo.astype(BF16))


def _router_kernel(x_ref, g_ref, wr_ref, b_ref, tri_ref, hp_ref, eid_ref, gate_ref, rank_ref, cnt_ref,
                   run_ref, scr_ref):
    t = pl.program_id(0)

    @pl.when(t == 0)
    def _():
        run_ref[...] = jnp.zeros_like(run_ref)

    h = _rms(x_ref[...], g_ref[...])
    _pack_store(hp_ref, 0, h, scr_ref)
    tm = h.shape[0]
    per_g = N_EXPERTS // N_GROUPS

    logits = lax.dot_general(wr_ref[...], h, _NT, precision=lax.Precision.HIGHEST,
                             preferred_element_type=F32)
    scores = 1.0 / (1.0 + jnp.exp(-logits))
    choice = scores + b_ref[...]

    sub = lax.broadcasted_iota(I32, (per_g, tm), 0).astype(F32)
    rows = []
    for g in range(N_GROUPS):
        c = choice[g * per_g:(g + 1) * per_g, :]
        m1 = c.max(0, keepdims=True)
        first = jnp.where(c == m1, sub, float(per_g)).min(0, keepdims=True)
        m2 = jnp.where(sub == first, -jnp.inf, c).max(0, keepdims=True)
        rows.append(m1 + m2)
    gs = jnp.concatenate(rows, axis=0)

    gsub = lax.broadcasted_iota(I32, (N_GROUPS, tm), 0).astype(F32)
    beaten = jnp.zeros((N_GROUPS, tm), F32)
    for g2 in range(N_GROUPS):
        row = gs[g2:g2 + 1, :]
        wins = (row > gs) | ((row == gs) & (gsub > float(g2)))
        beaten = beaten + jnp.where(wins, 1.0, 0.0)
    g_ok = jnp.where(beaten < TOPK_GROUPS, 1.0, 0.0)
    e_ok = jnp.concatenate(
        [jnp.broadcast_to(g_ok[g:g + 1, :], (per_g, tm)) for g in range(N_GROUPS)], axis=0)

    eiota = lax.broadcasted_iota(I32, (N_EXPERTS, tm), 0).astype(F32)
    masked = jnp.where(e_ok > 0.5, choice, -jnp.inf)
    chosen = jnp.zeros((N_EXPERTS, tm), F32)
    eids, ws = [], []
    for _ in range(TOP_K):
        m = masked.max(0, keepdims=True)
        idx = jnp.where(masked == m, eiota, float(N_EXPERTS)).min(0, keepdims=True)
        sel = eiota == idx
        ws.append(jnp.where(sel, scores, 0.0).sum(0, keepdims=True))
        masked = jnp.where(sel, -jnp.inf, masked)
        chosen = jnp.where(sel, 1.0, chosen)
        eids.append(idx)
    wsum = ws[0]
    for w in ws[1:]:
        wsum = wsum + w
    denom = wsum + 1e-20

    pos = jnp.dot(chosen.astype(BF16), tri_ref[...], preferred_element_type=F32) + run_ref[:, 0:1]
    run_ref[...] = run_ref[...] + chosen.sum(1, keepdims=True)
    cnt_ref[...] = run_ref[...].astype(I32)

    for r in range(TOP_K):
        eid_ref[r:r + 1, :] = eids[r].astype(I32)
        gate_ref[r:r + 1, :] = ws[r] / denom * ROUTED_SCALE
        rank_ref[r:r + 1, :] = jnp.where(eiota == eids[r], pos, 0.0).sum(0, keepdims=True).astype(I32)


def _router(x, g, w_router, router_bias, tm=512):
    n, d = x.shape
    ne = w_router.shape[1]
    tri = (jnp.arange(tm)[:, None] < jnp.arange(tm)[None, :]).astype(BF16)
    row8 = lambda i: (0, i)
    pitch = d // 2 // LANES
    return pl.pallas_call(
        _router_kernel,
        grid=(n // tm,),
        in_specs=[pl.BlockSpec((tm, d), lambda i: (i, 0)),
                  pl.BlockSpec((1, d), lambda i: (0, 0)),
                  pl.BlockSpec((ne, d), lambda i: (0, 0)),
                  pl.BlockSpec((ne, 1), lambda i: (0, 0)),
                  pl.BlockSpec((tm, tm), lambda i: (0, 0))],
        out_specs=[pl.BlockSpec((tm * pitch, LANES), lambda i: (i, 0)),
                   pl.BlockSpec((TOP_K, tm), row8), pl.BlockSpec((TOP_K, tm), row8),
                   pl.BlockSpec((TOP_K, tm), row8),
                   pl.BlockSpec((ne, LANES), lambda i: (0, 0))],
        out_shape=[jax.ShapeDtypeStruct((n * pitch, LANES), U32),
                   jax.ShapeDtypeStruct((TOP_K, n), I32), jax.ShapeDtypeStruct((TOP_K, n), F32),
                   jax.ShapeDtypeStruct((TOP_K, n), I32),
                   jax.ShapeDtypeStruct((ne, LANES), I32)],
        scratch_shapes=[pltpu.VMEM((ne, LANES), F32), pltpu.VMEM((2 * tm * pitch, LANES), F32)],
        compiler_params=_cp(("arbitrary",)),
        name="router",
    )(x, g.reshape(1, d), w_router.T, router_bias.reshape(ne, 1), tri)


def _dispatch_kernel(cnt_ref, start_ref, dest_ref, hp_ref, xs_ref, sem, *, tm, pitch):
    t = pl.program_id(0)
    pairs = TOP_K * tm

    def row_copy(r, slot):
        src = hp_ref.at[pl.ds(pl.multiple_of(r * pitch, pitch), pitch), :]
        dst = xs_ref.at[pl.ds(pl.multiple_of(slot * pitch, pitch), pitch), :]
        return pltpu.make_async_copy(src, dst, sem)

    def issue(g, c):
        for u in range(DMA_UNROLL):
            p = g * DMA_UNROLL + u
            row_copy(p & (tm - 1), dest_ref[0, 0, p]).start(priority=u % 2)
        return c

    lax.fori_loop(0, pairs // DMA_UNROLL, issue, 0)
    for _ in range(TOP_K):
        pltpu.make_async_copy(hp_ref, xs_ref.at[pl.ds(0, tm * pitch), :], sem).wait()

    @pl.when(t == pl.num_programs(0) - 1)
    def _():
        def run_copy(first, rows):
            dst = xs_ref.at[pl.ds(pl.multiple_of(first * pitch, pitch), rows * pitch), :]
            return pltpu.make_async_copy(hp_ref.at[pl.ds(0, rows * pitch), :], dst, sem)

        def per_expert(e, c):
            used = cnt_ref[e]
            npad = (-used) & (EXPERT_ROWS - 1)
            first = start_ref[e] * STEP_ROWS + used
            sizes = [1 << k for k in range(EXPERT_ROWS.bit_length() - 1)]
            offs = []
            off = first
            for size in sizes:
                offs.append(off)
                off = off + (npad & size)
            for size, o in zip(sizes, offs):
                pl.when((npad & size) != 0)(lambda size=size, o=o: run_copy(o, size).start())
            for size in sizes:
                pl.when((npad & size) != 0)(lambda size=size: run_copy(0, size).wait())
            return c

        lax.fori_loop(0, N_EXPERTS, per_expert, 0)


def _dispatch(hp, dest, counts, step_start, p_rows, pitch, tm=256):
    n = hp.shape[0] // pitch
    tiles = n // tm
    dest_t = dest.reshape(TOP_K, tiles, tm).transpose(1, 0, 2).reshape(tiles, 1, TOP_K * tm)
    grid_spec = pltpu.PrefetchScalarGridSpec(
        num_scalar_prefetch=2,
        grid=(tiles,),
        in_specs=[pl.BlockSpec((1, 1, TOP_K * tm), lambda i, *_: (i, 0, 0), memory_space=pltpu.SMEM),
                  pl.BlockSpec((tm * pitch, LANES), lambda i, *_: (i, 0))],
        out_specs=pl.BlockSpec(memory_space=pl.ANY),
        scratch_shapes=[pltpu.SemaphoreType.DMA(())],
    )
    return pl.pallas_call(
        functools.partial(_dispatch_kernel, tm=tm, pitch=pitch),
        grid_spec=grid_spec,
        out_shape=jax.ShapeDtypeStruct((p_rows * pitch, LANES), U32),
        compiler_params=_cp(("arbitrary",)),
        name="moe_dispatch",
    )(counts, step_start, dest_t, hp)


def _silu_mul(g, u):
    return g / (1.0 + jnp.exp(-g)) * u


def _expert_weights(sched_ref, w_hbm_refs, wbuf_ref, sem):
    b = pl.program_id(0)

    def copies(e, slot):
        return [pltpu.make_async_copy(w.at[e], wbuf_ref.at[slot, i], sem.at[slot])
                for i, w in enumerate(w_hbm_refs)]

    @pl.when(b == 0)
    def _():
        for c in copies(sched_ref[0, 0], 0):
            c.start()

    slot = sched_ref[3, b]

    @pl.when(sched_ref[2, b] == 1)
    def _():
        for c in copies(0, slot):
            c.wait()
        nxt = sched_ref[4, b]

        @pl.when(nxt >= 0)
        def _():
            for c in copies(nxt, 1 - slot):
                c.start()

    return slot


def _expert_up_kernel(sched_ref, nu_ref, xs_ref, wg_hbm, wu_hbm, o_ref, wbuf_ref, scr_ref, sem):
    b = pl.program_id(0)
    pitch = xs_ref.shape[0] // STEP_ROWS
    slot = _expert_weights(sched_ref, (wg_hbm, wu_hbm), wbuf_ref, sem)

    def sub_block(j):
        x = _load_unpacked(xs_ref, j * EXPERT_ROWS, EXPERT_ROWS, pitch, scr_ref)
        g = jnp.dot(x, wbuf_ref[slot, 0], preferred_element_type=F32)
        u = jnp.dot(x, wbuf_ref[slot, 1], preferred_element_type=F32)
        o_ref[j * EXPERT_ROWS:(j + 1) * EXPERT_ROWS, :] = _silu_mul(g, u).astype(o_ref.dtype)

    for j in range(STEP_ROWS // EXPERT_ROWS):
        pl.when(sched_ref[1, b] > j)(functools.partial(sub_block, j))


def _row_block(b, sched, nu):
    return (jnp.minimum(b, nu[0] - 1), 0)


def _expert_up(xs, w_gate, w_up, sched, n_used, n_steps, pitch):
    _, d, ff = w_gate.shape
    grid_spec = pltpu.PrefetchScalarGridSpec(
        num_scalar_prefetch=2,
        grid=(n_steps,),
        in_specs=[pl.BlockSpec((STEP_ROWS * pitch, LANES), _row_block),
                  pl.BlockSpec(memory_space=pl.ANY), pl.BlockSpec(memory_space=pl.ANY)],
        out_specs=pl.BlockSpec((STEP_ROWS, ff), _row_block),
        scratch_shapes=[pltpu.VMEM((2, 2, d, ff), F32),
                        pltpu.VMEM((2 * EXPERT_ROWS * pitch, LANES), F32),
                        pltpu.SemaphoreType.DMA((2,))],
    )
    return pl.pallas_call(
        _expert_up_kernel,
        grid_spec=grid_spec,
        out_shape=jax.ShapeDtypeStruct((n_steps * STEP_ROWS, ff), BF16),
        compiler_params=_cp(("arbitrary",)),
        name="expert_up",
    )(sched, n_used, xs, w_gate, w_up)


def _expert_down_kernel(sched_ref, nu_ref, h_ref, wd_hbm, o_ref, wbuf_ref, scr_ref, sem):
    b = pl.program_id(0)
    slot = _expert_weights(sched_ref, (wd_hbm,), wbuf_ref, sem)

    def sub_block(j):
        y = jnp.dot(h_ref[j * EXPERT_ROWS:(j + 1) * EXPERT_ROWS, :], wbuf_ref[slot, 0].astype(BF16),
                    preferred_element_type=F32)
        _pack_store(o_ref, j * EXPERT_ROWS, y, scr_ref)

    for j in range(STEP_ROWS // EXPERT_ROWS):
        pl.when(sched_ref[1, b] > j)(functools.partial(sub_block, j))


def _expert_down(hs, w_down, sched, n_used, n_steps, pitch):
    p_rows, ff = hs.shape
    d = w_down.shape[2]
    grid_spec = pltpu.PrefetchScalarGridSpec(
        num_scalar_prefetch=2,
        grid=(n_steps,),
        in_specs=[pl.BlockSpec((STEP_ROWS, ff), _row_block), pl.BlockSpec(memory_space=pl.ANY)],
        out_specs=pl.BlockSpec((STEP_ROWS * pitch, LANES), _row_block),
        scratch_shapes=[pltpu.VMEM((2, 1, ff, d), F32),
                        pltpu.VMEM((2 * EXPERT_ROWS * pitch, LANES), F32),
                        pltpu.SemaphoreType.DMA((2,))],
    )
    return pl.pallas_call(
        _expert_down_kernel,
        grid_spec=grid_spec,
        out_shape=jax.ShapeDtypeStruct((p_rows * pitch, LANES), U32),
        compiler_params=_cp(("arbitrary",)),
        name="expert_down",
    )(sched, n_used, hs, w_down)


def _shared_up_kernel(hp_ref, wg_ref, wu_ref, o_ref, scr_ref, wgb_ref, wub_ref):
    @pl.when(pl.program_id(0) == 0)
    def _():
        wgb_ref[...] = wg_ref[...].astype(BF16)
        wub_ref[...] = wu_ref[...].astype(BF16)

    tm = o_ref.shape[0]
    x = _load_unpacked(hp_ref, 0, tm, hp_ref.shape[0] // tm, scr_ref, BF16)
    g = jnp.dot(x, wgb_ref[...], preferred_element_type=F32)
    u = jnp.dot(x, wub_ref[...], preferred_element_type=F32)
    o_ref[...] = _silu_mul(g, u).astype(o_ref.dtype)


def _shared_up(hp, w_gate, w_up, pitch, tm=256):
    n = hp.shape[0] // pitch
    d, ff = w_gate.shape
    return pl.pallas_call(
        _shared_up_kernel,
        grid=(n // tm,),
        in_specs=[pl.BlockSpec((tm * pitch, LANES), lambda i: (i, 0)),
                  pl.BlockSpec((d, ff), lambda i: (0, 0)), pl.BlockSpec((d, ff), lambda i: (0, 0))],
        out_specs=pl.BlockSpec((tm, ff), lambda i: (i, 0)),
        out_shape=jax.ShapeDtypeStruct((n, ff), BF16),
        scratch_shapes=[pltpu.VMEM((2 * tm * pitch, LANES), F32),
                        pltpu.VMEM((d, ff), BF16), pltpu.VMEM((d, ff), BF16)],
        compiler_params=_cp(("arbitrary",)),
        name="shared_up",
    )(hp, w_gate, w_up)


def _combine_kernel(dfirst_ref, dnext_ref, gate_ref, x_ref, t_ref, wsd_ref, ys_ref, o_ref, buf_ref, scr_ref,
                    base_ref, sem, *, tm, pitch):
    t = pl.program_id(0)
    pairs = TOP_K * tm
    bpitch = buf_ref.shape[0] // (2 * pairs)

    def row_copy(dref, slot, p):
        src = ys_ref.at[pl.ds(pl.multiple_of(dref[0, 0, p] * pitch, pitch), pitch), :]
        dst = buf_ref.at[pl.ds(pl.multiple_of((slot * pairs + p) * bpitch, SUBLANES), pitch), :]
        return pltpu.make_async_copy(src, dst, sem.at[slot])

    def wait_slot(slot):
        for _ in range(TOP_K):
            pltpu.make_async_copy(ys_ref.at[pl.ds(0, tm * pitch), :], buf_ref.at[pl.ds(0, tm * pitch), :],
                                  sem.at[slot]).wait()

    def issue_tile(dref, slot):
        def body(g, c):
            for u in range(DMA_UNROLL):
                row_copy(dref, slot, g * DMA_UNROLL + u).start(priority=u % 2)
            return c

        lax.fori_loop(0, pairs // DMA_UNROLL, body, 0)

    @pl.when(t == 0)
    def _():
        issue_tile(dfirst_ref, 0)

    @pl.when(t + 1 < pl.num_programs(0))
    def _():
        issue_tile(dnext_ref, (t + 1) % 2)

    base_ref[...] = x_ref[...] + jnp.dot(t_ref[...], wsd_ref[...], preferred_element_type=F32)

    cur = t % 2
    wait_slot(cur)

    for grp in range(tm // COMBINE_GROUP):
        r0 = grp * COMBINE_GROUP
        rows = slice(r0, r0 + COMBINE_GROUP)
        gates = [jnp.broadcast_to(gate_ref[2 * r0:2 * (r0 + COMBINE_GROUP), k:k + 1], (2 * COMBINE_GROUP, LANES))
                 for k in range(TOP_K)]
        for s in range(pitch):
            acc = None
            for k in range(TOP_K):
                term = gates[k] * _load_pairs(buf_ref, cur * pairs + k * tm + r0, COMBINE_GROUP, bpitch, s)
                acc = term if acc is None else acc + term
            even, odd = _split_pairs(acc, scr_ref, grp * pitch + s)
            c0 = slice((2 * s) * LANES, (2 * s + 1) * LANES)
            c1 = slice((2 * s + 1) * LANES, (2 * s + 2) * LANES)
            o_ref[rows, c0] = base_ref[rows, c0] + even
            o_ref[rows, c1] = base_ref[rows, c1] + odd


def _combine(dest, gate_rows, x, t_shared, w_sh_down, ys, pitch, tm=128):
    n, d = x.shape
    ff = t_shared.shape[1]
    tiles = n // tm
    dest_t = dest.reshape(TOP_K, tiles, tm).transpose(1, 0, 2).reshape(tiles, 1, TOP_K * tm)
    dspec = lambda imap: pl.BlockSpec((1, 1, TOP_K * tm), imap, memory_space=pltpu.SMEM)
    bpitch = pitch if (pitch // SUBLANES) % 2 else pitch + SUBLANES
    return pl.pallas_call(
        functools.partial(_combine_kernel, tm=tm, pitch=pitch),
        grid=(tiles,),
        in_specs=[dspec(lambda i: (0, 0, 0)),
                  dspec(lambda i: (jnp.minimum(i + 1, tiles - 1), 0, 0)),
                  pl.BlockSpec((2 * tm, TOP_K), lambda i: (i, 0)),
                  pl.BlockSpec((tm, d), lambda i: (i, 0)),
                  pl.BlockSpec((tm, ff), lambda i: (i, 0)),
                  pl.BlockSpec((ff, d), lambda i: (0, 0)),
                  pl.BlockSpec(memory_space=pl.ANY)],
        out_specs=pl.BlockSpec((tm, d), lambda i: (i, 0)),
        out_shape=jax.ShapeDtypeStruct((n, d), F32),
        scratch_shapes=[pltpu.VMEM((2 * TOP_K * tm * bpitch, LANES), U32),
                        pltpu.VMEM((2 * tm * pitch, LANES), F32),
                        pltpu.VMEM((tm, d), F32),
                        pltpu.SemaphoreType.DMA((2,))],
        compiler_params=_cp(("arbitrary",)),
        name="moe_combine",
    )(dest_t, dest_t, gate_rows, x, t_shared, w_sh_down.astype(BF16), ys)


def _rope_tables(positions):
    half = ROPE_DIM // 2
    inv_freq = ROPE_THETA ** (-jnp.arange(0, ROPE_DIM, 2, dtype=F32) / ROPE_DIM)
    ang = positions.reshape(-1).astype(F32)[:, None] * inv_freq
    cos, sin = jnp.cos(ang), jnp.sin(ang)
    n = ang.shape[0]
    z = lambda w: jnp.zeros((n, w), F32)
    cos_t = jnp.concatenate([cos, cos, z(LANES - ROPE_DIM)], axis=1)
    sin_a = jnp.concatenate([-sin, z(LANES - half)], axis=1)
    sin_b = jnp.concatenate([z(half), sin, z(LANES - ROPE_DIM)], axis=1)
    return cos_t, sin_a, sin_b


def _pad_lanes(v, width):
    return jnp.concatenate([v, jnp.zeros((width - v.shape[0],), v.dtype)])


def kernel(x, mem, positions, g_mix, w_in, g_qa, g_ka, rel_bias, g_cq, w_uq, g_ckv, w_ukv, g_qb, g_kb, w_o, g_cross, g_mem, w_xq, w_xkv, g_qx, g_kx, w_xo, g_ffn, w_router, router_bias, w_sh_gate, w_sh_up, w_sh_down, w_ex_gate, w_ex_up, w_ex_down):
    batch, seq, d = x.shape
    n = batch * seq
    mem_tokens = mem.shape[1]
    width_a = HEADS_A * HEAD_DIM
    x2d = x.reshape(n, d)

    h = _norm_bf16(x2d, g_mix)
    gs_qk = jnp.concatenate([jnp.tile(g_qa * HEAD_DIM ** -0.5, HEADS_A), jnp.tile(g_ka, HEADS_A)])
    w_in_t = w_in.T
    qk = _in_proj(h, w_in_t, 0, 2 * width_a, BF16, "in_proj_qk", gs=gs_qk)
    v_a = _in_proj(h, w_in_t, 2 * width_a, width_a, BF16, "in_proj_v")
    c = _in_proj(h, w_in_t, 3 * width_a, Q_LORA + KV_LORA, F32, "in_proj_lora")
    w_kpe = jnp.pad(w_in_t[3 * width_a + Q_LORA + KV_LORA:], ((0, LANES - ROPE_DIM), (0, 0)))
    kpe = _in_proj(h, w_kpe, 0, LANES, F32, "in_proj_kpe")

    o_a = _attn_a(qk, v_a, _band_bias(rel_bias), batch, seq)

    cos_t, sin_a, sin_b = _rope_tables(positions)
    w_uq_pad = jnp.pad(w_uq.reshape(Q_LORA, HEADS_B, QK_B),
                       ((0, 0), (0, 0), (0, 2 * LANES - QK_B))).reshape(Q_LORA, HEADS_B * 2 * LANES)
    gq = _pad_lanes(g_qb, 2 * LANES).reshape(1, 2 * LANES)
    gk = _pad_lanes(g_kb, 2 * LANES).reshape(1, 2 * LANES)
    qf = _mla_q(c, g_cq, w_uq_pad, gq, cos_t, sin_a, sin_b)
    kf, v_b = _mla_kv(c, g_ckv, kpe, w_ukv, gk, cos_t, sin_a, sin_b)
    o_b = _attn_b(qf, kf, v_b, batch, seq)

    x1 = _out_proj(o_a, o_b, w_o, x2d)

    hw = MEM_HEADS * MEM_HEAD_DIM
    kvx = _normed_proj(mem.reshape(batch * mem_tokens, d), g_mem, w_xkv,
                       jnp.concatenate([jnp.tile(g_kx, MEM_HEADS), jnp.ones((hw,), F32)]), hw, "cross_kv")
    x2 = _cross_attn(x1, g_cross, w_xq, jnp.tile(g_qx * MEM_HEAD_DIM ** -0.5, MEM_HEADS), kvx, w_xo,
                     batch, seq, mem_tokens)

    hp, eid, gate, rank, cnt = _router(x2, g_ffn, w_router, router_bias)
    counts = cnt[:, 0]
    nstep = (counts + STEP_ROWS - 1) // STEP_ROWS
    step_end = jnp.cumsum(nstep).astype(I32)
    step_start = step_end - nstep
    n_steps = n * TOP_K // STEP_ROWS + N_EXPERTS
    experts = jnp.arange(N_EXPERTS, dtype=I32)
    start_of = jnp.sum(jnp.where(eid[:, :, None] == experts, step_start, 0), axis=-1)
    dest = start_of * STEP_ROWS + rank
    steps = jnp.arange(n_steps, dtype=I32)
    step_e = jnp.minimum(jnp.sum((step_end[None, :] <= steps[:, None]).astype(I32), axis=1), N_EXPERTS - 1)
    mine = step_e[:, None] == experts[None, :]
    rows_left = jnp.sum(jnp.where(mine, counts[None, :] - STEP_ROWS * (steps[:, None] - step_start[None, :]), 0),
                        axis=1)
    used = steps < step_end[-1]
    rows_here = jnp.where(used, jnp.clip(rows_left, 0, STEP_ROWS), 0)
    nsub = (rows_here + EXPERT_ROWS - 1) // EXPERT_ROWS
    nonempty = counts > 0
    ring_slot = (jnp.cumsum(nonempty.astype(I32)) - 1) % 2
    later = (experts[None, :] > experts[:, None]) & nonempty[None, :]
    succ = jnp.min(jnp.where(later, experts[None, :], N_EXPERTS), axis=1)
    succ = jnp.where(succ == N_EXPERTS, -1, succ)
    per_step = lambda v: jnp.sum(jnp.where(mine, v[None, :], 0), axis=1)
    first = (used & (steps == per_step(step_start))).astype(I32)
    sched = jnp.stack([step_e, nsub, first, per_step(ring_slot), per_step(succ)]).astype(I32)
    n_used = step_end[-1:]
    pitch = d // 2 // LANES

    xs = _dispatch(hp, dest, counts, step_start, n_steps * STEP_ROWS, pitch)
    hs = _expert_up(xs, w_ex_gate, w_ex_up, sched, n_used, n_steps, pitch)
    ys = _expert_down(hs, w_ex_down, sched, n_used, n_steps, pitch)

    t_shared = _shared_up(hp, w_sh_gate, w_sh_up, pitch)
    out = _combine(dest, jnp.repeat(gate.T, 2, axis=0), x2, t_shared, w_sh_down, ys, pitch)
    return out.reshape(batch, seq, d)
```

```python
import functools

import jax
import jax.numpy as jnp
import numpy as np
from jax import lax
from jax.experimental import pallas as pl
from jax.experimental.pallas import tpu as pltpu

F32 = jnp.float32
BF16 = jnp.bfloat16
I32 = jnp.int32
U32 = jnp.uint32

CHUNK = 64
LEFT_CHUNKS = 8
REL_CLIP = 128
HEAD_DIM = 128
HEADS_A = 16
HEADS_B = 16
Q_LORA = 1024
KV_LORA = 512
NOPE_DIM = 128
ROPE_DIM = 64
V_DIM = 128
QK_B = NOPE_DIM + ROPE_DIM
ROPE_THETA = 10000.0
MEM_HEADS = 4
MEM_HEAD_DIM = 128
N_EXPERTS = 64
N_GROUPS = 8
TOPK_GROUPS = 4
TOP_K = 8
ROUTED_SCALE = 2.5
EPS = 1e-6

LANES = 128
SUBLANES = 8
ATT_BLOCK = 256
EXPERT_ROWS = 256
STEP_ROWS = 2 * EXPERT_ROWS
DMA_UNROLL = 8
COMBINE_GROUP = 16
NEG = -1e30
VMEM_LIMIT = 56 * 1024 * 1024

_NT = (((1,), (1,)), ((), ()))


def _cp(sem, vmem=VMEM_LIMIT):
    return pltpu.CompilerParams(dimension_semantics=sem, vmem_limit_bytes=vmem)


def _rms(x, g):
    return x * lax.rsqrt(jnp.mean(x * x, axis=-1, keepdims=True) + EPS) * g


def _pack_store(ref, first_token, x, scr):
    m, w = x.shape
    pitch = w // (2 * LANES)
    regions = scr.shape[0] // (2 * m)
    for s in range(pitch):
        base = (s % regions) * 2 * m
        scr[pl.ds(base, m, stride=2), :] = x[:, (2 * s) * LANES:(2 * s + 1) * LANES]
        scr[pl.ds(base + 1, m, stride=2), :] = x[:, (2 * s + 1) * LANES:(2 * s + 2) * LANES]
        z = scr[pl.ds(base, 2 * m), :].astype(BF16)
        ref[pl.ds(first_token * pitch + s, m, stride=pitch), :] = pltpu.bitcast(z, U32)


def _load_pairs(ref, first_token, m, pitch, s):
    w = ref[pl.ds(first_token * pitch + s, m, stride=pitch), :]
    return pltpu.bitcast(w, BF16).astype(F32)


def _split_pairs(z, scr, region):
    m = z.shape[0] // 2
    base = region * 2 * m
    scr[pl.ds(base, 2 * m), :] = z
    return scr[pl.ds(base, m, stride=2), :], scr[pl.ds(base + 1, m, stride=2), :]


def _load_unpacked(ref, first_token, m, pitch, scr, dtype=F32):
    regions = scr.shape[0] // (2 * m)
    cols = []
    for s in range(pitch):
        for piece in _split_pairs(_load_pairs(ref, first_token, m, pitch, s), scr, s % regions):
            cols.append(piece.astype(dtype))
    return jnp.concatenate(cols, axis=1)


def _norm_kernel(x_ref, g_ref, o_ref):
    o_ref[...] = _rms(x_ref[...], g_ref[...]).astype(o_ref.dtype)


def _norm_bf16(x, g, tm=256):
    n, d = x.shape
    return pl.pallas_call(
        _norm_kernel,
        grid=(n // tm,),
        in_specs=[pl.BlockSpec((tm, d), lambda i: (i, 0)), pl.BlockSpec((1, d), lambda i: (0, 0))],
        out_specs=pl.BlockSpec((tm, d), lambda i: (i, 0)),
        out_shape=jax.ShapeDtypeStruct((n, d), BF16),
        compiler_params=_cp(("parallel",)),
        name="norm_mix",
    )(x, g.reshape(1, d))


def _proj_t_kernel(h_ref, w_ref, o_ref):
    o_ref[...] = lax.dot_general(h_ref[...], w_ref[...].astype(BF16), _NT, preferred_element_type=F32)


def _proj_t(h, w_t, name, tm=1024):
    n, d = h.shape
    nc = w_t.shape[0]
    return pl.pallas_call(
        _proj_t_kernel,
        grid=(n // tm,),
        in_specs=[pl.BlockSpec((tm, d), lambda i: (i, 0)), pl.BlockSpec((nc, d), lambda i: (0, 0))],
        out_specs=pl.BlockSpec((tm, nc), lambda i: (i, 0)),
        out_shape=jax.ShapeDtypeStruct((n, nc), F32),
        compiler_params=_cp(("parallel",)),
        name=name,
    )(h, w_t)


def _in_proj_all_kernel(h_ref, w_ref, gs_ref, qk_ref, v_ref, c_ref, *, tn, n_qk, n_v):
    j = pl.program_id(1)
    acc = lax.dot_general(h_ref[...], w_ref[...].astype(BF16), _NT, preferred_element_type=F32)

    @pl.when(j < n_qk)
    def _():
        for c in range(tn // LANES):
            sl = slice(c * LANES, (c + 1) * LANES)
            qk_ref[:, sl] = _rms(acc[:, sl], gs_ref[:, sl]).astype(qk_ref.dtype)

    @pl.when((j >= n_qk) & (j < n_qk + n_v))
    def _():
        v_ref[...] = acc.astype(v_ref.dtype)

    @pl.when(j >= n_qk + n_v)
    def _():
        c_ref[...] = acc


def _in_proj_all(h, w_t, gs_qk, w_qk, w_v, w_c, tm=1024, tn=512):
    n, d = h.shape
    n_qk, n_v, n_c = w_qk // tn, w_v // tn, w_c // tn
    return pl.pallas_call(
        functools.partial(_in_proj_all_kernel, tn=tn, n_qk=n_qk, n_v=n_v),
        grid=(n // tm, n_qk + n_v + n_c),
        in_specs=[pl.BlockSpec((tm, d), lambda i, j: (i, 0)),
                  pl.BlockSpec((tn, d), lambda i, j: (j, 0)),
                  pl.BlockSpec((1, tn), lambda i, j: (0, jnp.minimum(j, n_qk - 1)))],
        out_specs=[pl.BlockSpec((tm, tn), lambda i, j: (i, jnp.minimum(j, n_qk - 1))),
                   pl.BlockSpec((tm, tn), lambda i, j: (i, jnp.clip(j - n_qk, 0, n_v - 1))),
                   pl.BlockSpec((tm, tn), lambda i, j: (i, jnp.clip(j - n_qk - n_v, 0, n_c - 1)))],
        out_shape=[jax.ShapeDtypeStruct((n, w_qk), BF16), jax.ShapeDtypeStruct((n, w_v), BF16),
                   jax.ShapeDtypeStruct((n, w_c), F32)],
        compiler_params=_cp(("parallel", "arbitrary")),
        name="in_proj",
    )(h, w_t, gs_qk.reshape(1, w_qk))


def _attn_a_kernel(q_ref, k0_ref, k1_ref, k2_ref, v0_ref, v1_ref, v2_ref, b_ref, o_ref, *, heads):
    i = pl.program_id(2)
    k_refs = (k0_ref, k1_ref, k2_ref)
    v_refs = (v0_ref, v1_ref, v2_ref)
    for hh in range(heads):
        sl = slice(hh * HEAD_DIM, (hh + 1) * HEAD_DIM)
        q = q_ref[:, sl]
        s = []
        for d in range(3):
            sd = lax.dot_general(q, k_refs[d][:, sl], _NT, preferred_element_type=F32)
            sd = sd + b_ref[hh, :, d * ATT_BLOCK:(d + 1) * ATT_BLOCK]
            if d > 0:
                sd = jnp.where(i >= d, sd, NEG)
            s.append(sd)
        m = jnp.maximum(jnp.maximum(s[0].max(-1, keepdims=True), s[1].max(-1, keepdims=True)),
                        s[2].max(-1, keepdims=True))
        l = jnp.zeros_like(m)
        o = jnp.zeros((q.shape[0], HEAD_DIM), F32)
        for d in range(3):
            p = jnp.exp(s[d] - m)
            l = l + p.sum(-1, keepdims=True)
            o = o + jnp.dot(p.astype(BF16), v_refs[d][:, sl], preferred_element_type=F32)
        o_ref[:, sl] = (o / l).astype(o_ref.dtype)


def _band_bias(rel_bias):
    blk = ATT_BLOCK
    period = 2 * blk + 1
    heads = rel_bias.shape[0]
    r = np.arange(blk)[:, None]
    c = np.arange(blk)[None, :]
    per_blk = blk // CHUNK
    k = np.arange(period)
    delta = np.where(k <= blk, k, k - period)
    tiles = []
    for d in range(3):
        idx = np.clip(blk * d - delta, -(CHUNK - 1), REL_CLIP) + CHUNK - 1
        w = rel_bias[:, idx].astype(F32)
        b = jnp.tile(w, (1, blk))[:, :blk * (period - 1)].reshape(heads, blk, period - 1)[:, :, :blk]
        cdiff = per_blk * d + r // CHUNK - c // CHUNK
        valid = (cdiff >= 0) & (cdiff <= LEFT_CHUNKS)
        tiles.append(jnp.where(valid[None], b, NEG))
    return jnp.concatenate(tiles, axis=-1)


def _attn_a(qk, v, bias, batch, seq, heads_per_step=4):
    n = qk.shape[0]
    nq = seq // ATT_BLOCK
    hw = heads_per_step * HEAD_DIM
    groups = HEADS_A // heads_per_step
    kcol0 = HEADS_A * HEAD_DIM // hw

    def kspec(d, col0):
        return pl.BlockSpec((ATT_BLOCK, hw), lambda g, b, i: (b * nq + jnp.maximum(i - d, 0), col0 + g))

    return pl.pallas_call(
        functools.partial(_attn_a_kernel, heads=heads_per_step),
        grid=(groups, batch, nq),
        in_specs=[pl.BlockSpec((ATT_BLOCK, hw), lambda g, b, i: (b * nq + i, g)),
                  kspec(0, kcol0), kspec(1, kcol0), kspec(2, kcol0),
                  kspec(0, 0), kspec(1, 0), kspec(2, 0),
                  pl.BlockSpec((heads_per_step, ATT_BLOCK, 3 * ATT_BLOCK), lambda g, b, i: (g, 0, 0))],
        out_specs=pl.BlockSpec((ATT_BLOCK, hw), lambda g, b, i: (b * nq + i, g)),
        out_shape=jax.ShapeDtypeStruct((n, HEADS_A * HEAD_DIM), BF16),
        compiler_params=_cp(("parallel", "parallel", "arbitrary")),
        name="attn_band",
    )(qk, qk, qk, qk, v, v, v, bias)


def _rope_pe(pe, g, cos_ref, sa_ref, sb_ref):
    ss = jnp.sum(pe * pe, axis=-1, keepdims=True) * (1.0 / ROPE_DIM)
    pn = pe * lax.rsqrt(ss + EPS) * g
    half = ROPE_DIM // 2
    return (pn * cos_ref[...] + pltpu.roll(pn, LANES - half, 1) * sa_ref[...]
            + pltpu.roll(pn, half, 1) * sb_ref[...])


def _mla_q_kernel(c_ref, gc_ref, w_ref, gq_ref, cos_ref, sa_ref, sb_ref, o_ref, xn_ref, *, heads, scale):
    @pl.when(pl.program_id(1) == 0)
    def _():
        xn_ref[...] = _rms(c_ref[...], gc_ref[...]).astype(BF16)

    acc = jnp.dot(xn_ref[...], w_ref[...].astype(BF16), preferred_element_type=F32)
    for hh in range(heads):
        base = hh * 2 * LANES
        nope = acc[:, base:base + LANES]
        pe = acc[:, base + LANES:base + 2 * LANES]
        o_ref[:, base:base + LANES] = (_rms(nope, gq_ref[:, :LANES]) * scale).astype(o_ref.dtype)
        o_ref[:, base + LANES:base + 2 * LANES] = (
            _rope_pe(pe, gq_ref[:, LANES:], cos_ref, sa_ref, sb_ref) * scale).astype(o_ref.dtype)


def _mla_q(c, g_cq, w_uq_pad, gq, cos_t, sin_a, sin_b, tm=512, heads_per_step=4):
    n = c.shape[0]
    tn = heads_per_step * 2 * LANES
    ncols = w_uq_pad.shape[1]
    row = lambda i, j: (i, 0)
    return pl.pallas_call(
        functools.partial(_mla_q_kernel, heads=heads_per_step, scale=QK_B ** -0.5),
        grid=(n // tm, ncols // tn),
        in_specs=[pl.BlockSpec((tm, Q_LORA), row),
                  pl.BlockSpec((1, Q_LORA), lambda i, j: (0, 0)),
                  pl.BlockSpec((Q_LORA, tn), lambda i, j: (0, j)),
                  pl.BlockSpec((1, 2 * LANES), lambda i, j: (0, 0)),
                  pl.BlockSpec((tm, LANES), row), pl.BlockSpec((tm, LANES), row), pl.BlockSpec((tm, LANES), row)],
        out_specs=pl.BlockSpec((tm, tn), lambda i, j: (i, j)),
        out_shape=jax.ShapeDtypeStruct((n, ncols), BF16),
        scratch_shapes=[pltpu.VMEM((tm, Q_LORA), BF16)],
        compiler_params=_cp(("parallel", "arbitrary")),
        name="mla_q_proj",
    )(c, g_cq.reshape(1, Q_LORA), w_uq_pad, gq, cos_t, sin_a, sin_b)


def _mla_kv_kernel(c_ref, gc_ref, kpe_ref, w_ref, gk_ref, cos_ref, sa_ref, sb_ref, k_ref, v_ref,
                   xn_ref, pe_ref, *, heads):
    @pl.when(pl.program_id(1) == 0)
    def _():
        xn_ref[...] = _rms(c_ref[...], gc_ref[...]).astype(BF16)
        pe_ref[...] = _rope_pe(kpe_ref[...], gk_ref[:, LANES:], cos_ref, sa_ref, sb_ref).astype(BF16)

    acc = jnp.dot(xn_ref[...], w_ref[...].astype(BF16), preferred_element_type=F32)
    for hh in range(heads):
        base = hh * 2 * LANES
        k_ref[:, base:base + LANES] = _rms(acc[:, base:base + LANES], gk_ref[:, :LANES]).astype(k_ref.dtype)
        k_ref[:, base + LANES:base + 2 * LANES] = pe_ref[...]
        v_ref[:, hh * LANES:(hh + 1) * LANES] = acc[:, base + LANES:base + 2 * LANES].astype(v_ref.dtype)


def _mla_kv(c, g_ckv, kpe, w_ukv, gk, cos_t, sin_a, sin_b, tm=512, heads_per_step=4):
    n = c.shape[0]
    tn = heads_per_step * 2 * LANES
    ncols = w_ukv.shape[1]
    cblk = Q_LORA // KV_LORA
    row = lambda i, j: (i, 0)
    return pl.pallas_call(
        functools.partial(_mla_kv_kernel, heads=heads_per_step),
        grid=(n // tm, ncols // tn),
        in_specs=[pl.BlockSpec((tm, KV_LORA), lambda i, j: (i, cblk)),
                  pl.BlockSpec((1, KV_LORA), lambda i, j: (0, 0)),
                  pl.BlockSpec((tm, LANES), row),
                  pl.BlockSpec((KV_LORA, tn), lambda i, j: (0, j)),
                  pl.BlockSpec((1, 2 * LANES), lambda i, j: (0, 0)),
                  pl.BlockSpec((tm, LANES), row), pl.BlockSpec((tm, LANES), row), pl.BlockSpec((tm, LANES), row)],
        out_specs=[pl.BlockSpec((tm, tn), lambda i, j: (i, j)),
                   pl.BlockSpec((tm, tn // 2), lambda i, j: (i, j))],
        out_shape=[jax.ShapeDtypeStruct((n, ncols), BF16),
                   jax.ShapeDtypeStruct((n, ncols // 2), BF16)],
        scratch_shapes=[pltpu.VMEM((tm, KV_LORA), BF16), pltpu.VMEM((tm, LANES), BF16)],
        compiler_params=_cp(("parallel", "arbitrary")),
        name="mla_kv_proj",
    )(c, g_ckv.reshape(1, KV_LORA), kpe, w_ukv, gk, cos_t, sin_a, sin_b)


def _attn_b_kernel(q_ref, k_ref, v_ref, o_ref, *, heads):
    i = pl.program_id(2)
    tq = q_ref.shape[0]
    qw = 2 * LANES
    qs = [q_ref[:, h * qw:(h + 1) * qw] for h in range(heads)]

    def step(first_blk, width, carry, masked):
        start = pl.multiple_of(first_blk * ATT_BLOCK, ATT_BLOCK)
        if masked:
            per_blk = ATT_BLOCK // CHUNK
            r = i * per_blk + lax.broadcasted_iota(I32, (tq, width), 0) // CHUNK
            c = first_blk * per_blk + lax.broadcasted_iota(I32, (tq, width), 1) // CHUNK
            keep = c <= r
        out = []
        for h in range(heads):
            m, l, acc = carry[h]
            s = lax.dot_general(qs[h], k_ref[pl.ds(start, width), h * qw:(h + 1) * qw], _NT,
                                preferred_element_type=F32)
            if masked:
                s = jnp.where(keep, s, NEG)
            m_new = jnp.maximum(m, s.max(-1, keepdims=True))
            alpha = jnp.exp(m - m_new)
            p = jnp.exp(s - m_new)
            l = alpha * l + p.sum(-1, keepdims=True)
            acc = alpha * acc + jnp.dot(p.astype(BF16), v_ref[pl.ds(start, width), h * V_DIM:(h + 1) * V_DIM],
                                        preferred_element_type=F32)
            out.append((m_new, l, acc))
        return tuple(out)

    def finish(final):
        for h in range(heads):
            _, l, acc = final[h]
            o_ref[:, h * V_DIM:(h + 1) * V_DIM] = (acc / l).astype(o_ref.dtype)

    init = tuple((jnp.full((tq, 1), NEG, F32), jnp.zeros((tq, 1), F32), jnp.zeros((tq, V_DIM), F32))
                 for _ in range(heads))
    carry = lax.fori_loop(0, i // 2, lambda j, c: step(2 * j, 2 * ATT_BLOCK, c, False), init)

    @pl.when(i % 2 == 1)
    def _():
        finish(step(i - 1, 2 * ATT_BLOCK, carry, True))

    @pl.when(i % 2 == 0)
    def _():
        finish(step(i, ATT_BLOCK, carry, True))


def _attn_b(qf, kf, vb, batch, seq, heads_per_step=4):
    n = qf.shape[0]
    nq = seq // ATT_BLOCK
    qw = heads_per_step * 2 * LANES
    vw = heads_per_step * V_DIM
    return pl.pallas_call(
        functools.partial(_attn_b_kernel, heads=heads_per_step),
        grid=(batch, HEADS_B // heads_per_step, nq),
        in_specs=[pl.BlockSpec((ATT_BLOCK, qw), lambda b, g, i: (b * nq + i, g)),
                  pl.BlockSpec((seq, qw), lambda b, g, i: (b, g)),
                  pl.BlockSpec((seq, vw), lambda b, g, i: (b, g))],
        out_specs=pl.BlockSpec((ATT_BLOCK, vw), lambda b, g, i: (b * nq + i, g)),
        out_shape=jax.ShapeDtypeStruct((n, HEADS_B * V_DIM), BF16),
        compiler_params=_cp(("parallel", "parallel", "arbitrary")),
        name="attn_latent",
    )(qf, kf, vb)


def _out_proj_kernel(oa_ref, ob_ref, w_ref, x_ref, o_ref, wb_ref):
    @pl.when(pl.program_id(1) == 0)
    def _():
        wb_ref[...] = w_ref[...].astype(BF16)

    ka = oa_ref.shape[1]
    acc = jnp.dot(oa_ref[...], wb_ref[:ka, :], preferred_element_type=F32)
    acc = acc + jnp.dot(ob_ref[...], wb_ref[ka:, :], preferred_element_type=F32)
    o_ref[...] = x_ref[...] + acc


def _out_proj(oa, ob, w_o, x, tm=256, tn=1024):
    n, ka = oa.shape
    kb = ob.shape[1]
    d = w_o.shape[1]
    return pl.pallas_call(
        _out_proj_kernel,
        grid=(d // tn, n // tm),
        in_specs=[pl.BlockSpec((tm, ka), lambda j, i: (i, 0)),
                  pl.BlockSpec((tm, kb), lambda j, i: (i, 0)),
                  pl.BlockSpec((ka + kb, tn), lambda j, i: (0, j)),
                  pl.BlockSpec((tm, tn), lambda j, i: (i, j))],
        out_specs=pl.BlockSpec((tm, tn), lambda j, i: (i, j)),
        out_shape=jax.ShapeDtypeStruct((n, d), F32),
        scratch_shapes=[pltpu.VMEM((ka + kb, tn), BF16)],
        compiler_params=_cp(("parallel", "arbitrary")),
        name="out_proj",
    )(oa, ob, w_o, x)


def _normed_proj_kernel(x_ref, g_ref, w_ref, gs_ref, o_ref, wb_ref, *, norm_cols):
    @pl.when(pl.program_id(0) == 0)
    def _():
        wb_ref[...] = w_ref[...].astype(BF16)

    h = _rms(x_ref[...], g_ref[...]).astype(BF16)
    acc = jnp.dot(h, wb_ref[...], preferred_element_type=F32)
    for c in range(acc.shape[1] // LANES):
        sl = slice(c * LANES, (c + 1) * LANES)
        if c * LANES < norm_cols:
            o_ref[:, sl] = _rms(acc[:, sl], gs_ref[:, sl]).astype(o_ref.dtype)
        else:
            o_ref[:, sl] = acc[:, sl].astype(o_ref.dtype)


def _normed_proj(x, g, w, gs, norm_cols, name, tm=256):
    n, d = x.shape
    nc = w.shape[1]
    return pl.pallas_call(
        functools.partial(_normed_proj_kernel, norm_cols=norm_cols),
        grid=(n // tm,),
        in_specs=[pl.BlockSpec((tm, d), lambda i: (i, 0)),
                  pl.BlockSpec((1, d), lambda i: (0, 0)),
                  pl.BlockSpec((d, nc), lambda i: (0, 0)),
                  pl.BlockSpec((1, nc), lambda i: (0, 0))],
        out_specs=pl.BlockSpec((tm, nc), lambda i: (i, 0)),
        out_shape=jax.ShapeDtypeStruct((n, nc), BF16),
        scratch_shapes=[pltpu.VMEM((d, nc), BF16)],
        compiler_params=_cp(("arbitrary",)),
        name=name,
    )(x, g.reshape(1, d), w, gs.reshape(1, nc))


def _cross_kernel(x_ref, g_ref, wq_ref, gq_ref, k_ref, v_ref, wo_ref, o_ref):
    x = x_ref[...]
    q = jnp.dot(_rms(x, g_ref[...]).astype(BF16), wq_ref[...], preferred_element_type=F32)
    outs = []
    for hh in range(MEM_HEADS):
        sl = slice(hh * MEM_HEAD_DIM, (hh + 1) * MEM_HEAD_DIM)
        qh = _rms(q[:, sl], gq_ref[:, sl]).astype(BF16)
        s = lax.dot_general(qh, k_ref[:, sl], _NT, preferred_element_type=F32)
        p = jnp.exp(s - s.max(-1, keepdims=True))
        o = jnp.dot(p.astype(BF16), v_ref[:, sl], preferred_element_type=F32)
        outs.append((o / p.sum(-1, keepdims=True)).astype(BF16))
    acc = x
    for hh in range(MEM_HEADS):
        sl = slice(hh * MEM_HEAD_DIM, (hh + 1) * MEM_HEAD_DIM)
        acc = acc + jnp.dot(outs[hh], wo_ref[sl, :], preferred_element_type=F32)
    o_ref[...] = acc


def _cross_attn(x, g, w_xq, gq, kvx, w_xo, batch, seq, mem_tokens, tm=256):
    n, d = x.shape
    per_b = seq // tm
    hw = MEM_HEADS * MEM_HEAD_DIM
    const = lambda b, i: (0, 0)
    return pl.pallas_call(
        _cross_kernel,
        grid=(batch, per_b),
        in_specs=[pl.BlockSpec((tm, d), lambda b, i: (b * per_b + i, 0)),
                  pl.BlockSpec((1, d), const),
                  pl.BlockSpec((d, hw), const),
                  pl.BlockSpec((1, hw), const),
                  pl.BlockSpec((mem_tokens, hw), lambda b, i: (b, 0)),
                  pl.BlockSpec((mem_tokens, hw), lambda b, i: (b, 1)),
                  pl.BlockSpec((hw, d), const)],
        out_specs=pl.BlockSpec((tm, d), lambda b, i: (b * per_b + i, 0)),
        out_shape=jax.ShapeDtypeStruct((n, d), F32),
        compiler_params=_cp(("parallel", "arbitrary")),
        name="cross_attn",
    )(x, g.reshape(1, d), w_xq.astype(BF16), gq.reshape(1, hw), kvx, kvx, w_xo.astype(BF16))


def _router_kernel(x_ref, g_ref, wr_ref, b_ref, tri_ref, wg_ref, wu_ref, hp_ref, eid_ref, gate_ref, rank_ref,
                   cnt_ref, ts_ref, run_ref, scr_ref):
    t = pl.program_id(0)

    @pl.when(t == 0)
    def _():
        run_ref[...] = jnp.zeros_like(run_ref)

    h = _rms(x_ref[...], g_ref[...])
    _pack_store(hp_ref, 0, h, scr_ref)
    hb = h.astype(BF16)
    ts_ref[...] = _silu_mul(jnp.dot(hb, wg_ref[...], preferred_element_type=F32),
                            jnp.dot(hb, wu_ref[...], preferred_element_type=F32)).astype(ts_ref.dtype)
    tm = h.shape[0]
    per_g = N_EXPERTS // N_GROUPS

    logits = lax.dot_general(wr_ref[...], h, _NT, precision=lax.Precision.HIGHEST,
                             preferred_element_type=F32)
    scores = 1.0 / (1.0 + jnp.exp(-logits))
    choice = scores + b_ref[...]

    sub = lax.broadcasted_iota(I32, (per_g, tm), 0).astype(F32)
    rows = []
    for g in range(N_GROUPS):
        c = choice[g * per_g:(g + 1) * per_g, :]
        m1 = c.max(0, keepdims=True)
        first = jnp.where(c == m1, sub, float(per_g)).min(0, keepdims=True)
        m2 = jnp.where(sub == first, -jnp.inf, c).max(0, keepdims=True)
        rows.append(m1 + m2)
    gs = jnp.concatenate(rows, axis=0)

    gsub = lax.broadcasted_iota(I32, (N_GROUPS, tm), 0).astype(F32)
    beaten = jnp.zeros((N_GROUPS, tm), F32)
    for g2 in range(N_GROUPS):
        row = gs[g2:g2 + 1, :]
        wins = (row > gs) | ((row == gs) & (gsub > float(g2)))
        beaten = beaten + jnp.where(wins, 1.0, 0.0)
    g_ok = jnp.where(beaten < TOPK_GROUPS, 1.0, 0.0)
    e_ok = jnp.concatenate(
        [jnp.broadcast_to(g_ok[g:g + 1, :], (per_g, tm)) for g in range(N_GROUPS)], axis=0)

    eiota = lax.broadcasted_iota(I32, (N_EXPERTS, tm), 0).astype(F32)
    masked = jnp.where(e_ok > 0.5, choice, -jnp.inf)
    chosen = jnp.zeros((N_EXPERTS, tm), F32)
    eids, ws = [], []
    for _ in range(TOP_K):
        m = masked.max(0, keepdims=True)
        idx = jnp.where(masked == m, eiota, float(N_EXPERTS)).min(0, keepdims=True)
        sel = eiota == idx
        ws.append(jnp.where(sel, scores, 0.0).sum(0, keepdims=True))
        masked = jnp.where(sel, -jnp.inf, masked)
        chosen = jnp.where(sel, 1.0, chosen)
        eids.append(idx)
    wsum = ws[0]
    for w in ws[1:]:
        wsum = wsum + w
    denom = wsum + 1e-20

    pos = jnp.dot(chosen.astype(BF16), tri_ref[...], preferred_element_type=F32) + run_ref[:, 0:1]
    run_ref[...] = run_ref[...] + chosen.sum(1, keepdims=True)
    cnt_ref[...] = run_ref[...].astype(I32)

    for r in range(TOP_K):
        eid_ref[r:r + 1, :] = eids[r].astype(I32)
        gate_ref[r:r + 1, :] = ws[r] / denom * ROUTED_SCALE
        rank_ref[r:r + 1, :] = jnp.where(eiota == eids[r], pos, 0.0).sum(0, keepdims=True).astype(I32)


def _router(x, g, w_router, router_bias, w_sh_gate, w_sh_up, tm=256):
    n, d = x.shape
    ne = w_router.shape[1]
    ff = w_sh_gate.shape[1]
    tri = (jnp.arange(tm)[:, None] < jnp.arange(tm)[None, :]).astype(BF16)
    row8 = lambda i: (0, i)
    pitch = d // 2 // LANES
    return pl.pallas_call(
        _router_kernel,
        grid=(n // tm,),
        in_specs=[pl.BlockSpec((tm, d), lambda i: (i, 0)),
                  pl.BlockSpec((1, d), lambda i: (0, 0)),
                  pl.BlockSpec((ne, d), lambda i: (0, 0)),
                  pl.BlockSpec((ne, 1), lambda i: (0, 0)),
                  pl.BlockSpec((tm, tm), lambda i: (0, 0)),
                  pl.BlockSpec((d, ff), lambda i: (0, 0)), pl.BlockSpec((d, ff), lambda i: (0, 0))],
        out_specs=[pl.BlockSpec((tm * pitch, LANES), lambda i: (i, 0)),
                   pl.BlockSpec((TOP_K, tm), row8), pl.BlockSpec((TOP_K, tm), row8),
                   pl.BlockSpec((TOP_K, tm), row8),
                   pl.BlockSpec((ne, LANES), lambda i: (0, 0)),
                   pl.BlockSpec((tm, ff), lambda i: (i, 0))],
        out_shape=[jax.ShapeDtypeStruct((n * pitch, LANES), U32),
                   jax.ShapeDtypeStruct((TOP_K, n), I32), jax.ShapeDtypeStruct((TOP_K, n), F32),
                   jax.ShapeDtypeStruct((TOP_K, n), I32),
                   jax.ShapeDtypeStruct((ne, LANES), I32),
                   jax.ShapeDtypeStruct((n, ff), BF16)],
        scratch_shapes=[pltpu.VMEM((ne, LANES), F32), pltpu.VMEM((2 * tm * pitch, LANES), F32)],
        compiler_params=_cp(("arbitrary",)),
        name="router",
    )(x, g.reshape(1, d), w_router.T, router_bias.reshape(ne, 1), tri,
      w_sh_gate.astype(BF16), w_sh_up.astype(BF16))


def _dispatch_kernel(cnt_ref, start_ref, dest_ref, hp_ref, xs_ref, sem, *, tm, pitch):
    t = pl.program_id(0)
    pairs = TOP_K * tm

    def row_copy(r, slot):
        src = hp_ref.at[pl.ds(pl.multiple_of(r * pitch, pitch), pitch), :]
        dst = xs_ref.at[pl.ds(pl.multiple_of(slot * pitch, pitch), pitch), :]
        return pltpu.make_async_copy(src, dst, sem)

    def issue(g, c):
        for u in range(DMA_UNROLL):
            p = g * DMA_UNROLL + u
            row_copy(p & (tm - 1), dest_ref[0, 0, p]).start(priority=u % 2)
        return c

    lax.fori_loop(0, pairs // DMA_UNROLL, issue, 0)
    for _ in range(TOP_K):
        pltpu.make_async_copy(hp_ref, xs_ref.at[pl.ds(0, tm * pitch), :], sem).wait()

    @pl.when(t == pl.num_programs(0) - 1)
    def _():
        def run_copy(first, rows):
            dst = xs_ref.at[pl.ds(pl.multiple_of(first * pitch, pitch), rows * pitch), :]
            return pltpu.make_async_copy(hp_ref.at[pl.ds(0, rows * pitch), :], dst, sem)

        def per_expert(e, c):
            used = cnt_ref[e]
            npad = (-used) & (EXPERT_ROWS - 1)
            first = start_ref[e] * STEP_ROWS + used
            sizes = [1 << k for k in range(EXPERT_ROWS.bit_length() - 1)]
            offs = []
            off = first
            for size in sizes:
                offs.append(off)
                off = off + (npad & size)
            for size, o in zip(sizes, offs):
                pl.when((npad & size) != 0)(lambda size=size, o=o: run_copy(o, size).start())
            for size in sizes:
                pl.when((npad & size) != 0)(lambda size=size: run_copy(0, size).wait())
            return c

        lax.fori_loop(0, N_EXPERTS, per_expert, 0)


def _dispatch(hp, dest, counts, step_start, p_rows, pitch, tm=256):
    n = hp.shape[0] // pitch
    tiles = n // tm
    dest_t = dest.reshape(TOP_K, tiles, tm).transpose(1, 0, 2).reshape(tiles, 1, TOP_K * tm)
    grid_spec = pltpu.PrefetchScalarGridSpec(
        num_scalar_prefetch=2,
        grid=(tiles,),
        in_specs=[pl.BlockSpec((1, 1, TOP_K * tm), lambda i, *_: (i, 0, 0), memory_space=pltpu.SMEM),
                  pl.BlockSpec((tm * pitch, LANES), lambda i, *_: (i, 0))],
        out_specs=pl.BlockSpec(memory_space=pl.ANY),
        scratch_shapes=[pltpu.SemaphoreType.DMA(())],
    )
    return pl.pallas_call(
        functools.partial(_dispatch_kernel, tm=tm, pitch=pitch),
        grid_spec=grid_spec,
        out_shape=jax.ShapeDtypeStruct((p_rows * pitch, LANES), U32),
        compiler_params=_cp(("arbitrary",)),
        name="moe_dispatch",
    )(counts, step_start, dest_t, hp)


def _silu_mul(g, u):
    return g / (1.0 + jnp.exp(-g)) * u


def _expert_weights(sched_ref, w_hbm_refs, wbuf_ref, sem):
    b = pl.program_id(0)

    def copies(e, slot):
        return [pltpu.make_async_copy(w.at[e], wbuf_ref.at[slot, i], sem.at[slot])
                for i, w in enumerate(w_hbm_refs)]

    @pl.when(b == 0)
    def _():
        for c in copies(sched_ref[0, 0], 0):
            c.start()

    slot = sched_ref[3, b]

    @pl.when(sched_ref[2, b] == 1)
    def _():
        for c in copies(0, slot):
            c.wait()
        nxt = sched_ref[4, b]

        @pl.when(nxt >= 0)
        def _():
            for c in copies(nxt, 1 - slot):
                c.start()

    return slot


def _expert_up_kernel(sched_ref, nu_ref, xs_ref, wg_hbm, wu_hbm, o_ref, wbuf_ref, scr_ref, sem):
    b = pl.program_id(0)
    pitch = xs_ref.shape[0] // STEP_ROWS
    slot = _expert_weights(sched_ref, (wg_hbm, wu_hbm), wbuf_ref, sem)

    def sub_block(j):
        x = _load_unpacked(xs_ref, j * EXPERT_ROWS, EXPERT_ROWS, pitch, scr_ref)
        g = jnp.dot(x, wbuf_ref[slot, 0], preferred_element_type=F32)
        u = jnp.dot(x, wbuf_ref[slot, 1], preferred_element_type=F32)
        o_ref[j * EXPERT_ROWS:(j + 1) * EXPERT_ROWS, :] = _silu_mul(g, u).astype(o_ref.dtype)

    for j in range(STEP_ROWS // EXPERT_ROWS):
        pl.when(sched_ref[1, b] > j)(functools.partial(sub_block, j))


def _row_block(b, sched, nu):
    return (jnp.minimum(b, nu[0] - 1), 0)


def _expert_up(xs, w_gate, w_up, sched, n_used, n_steps, pitch):
    _, d, ff = w_gate.shape
    grid_spec = pltpu.PrefetchScalarGridSpec(
        num_scalar_prefetch=2,
        grid=(n_steps,),
        in_specs=[pl.BlockSpec((STEP_ROWS * pitch, LANES), _row_block),
                  pl.BlockSpec(memory_space=pl.ANY), pl.BlockSpec(memory_space=pl.ANY)],
        out_specs=pl.BlockSpec((STEP_ROWS, ff), _row_block),
        scratch_shapes=[pltpu.VMEM((2, 2, d, ff), F32),
                        pltpu.VMEM((2 * EXPERT_ROWS * pitch, LANES), F32),
                        pltpu.SemaphoreType.DMA((2,))],
    )
    return pl.pallas_call(
        _expert_up_kernel,
        grid_spec=grid_spec,
        out_shape=jax.ShapeDtypeStruct((n_steps * STEP_ROWS, ff), BF16),
        compiler_params=_cp(("arbitrary",)),
        name="expert_up",
    )(sched, n_used, xs, w_gate, w_up)


def _expert_down_kernel(sched_ref, nu_ref, h_ref, wd_hbm, o_ref, wbuf_ref, scr_ref, sem):
    b = pl.program_id(0)
    slot = _expert_weights(sched_ref, (wd_hbm,), wbuf_ref, sem)

    def sub_block(j):
        y = jnp.dot(h_ref[j * EXPERT_ROWS:(j + 1) * EXPERT_ROWS, :], wbuf_ref[slot, 0].astype(BF16),
                    preferred_element_type=F32)
        _pack_store(o_ref, j * EXPERT_ROWS, y, scr_ref)

    for j in range(STEP_ROWS // EXPERT_ROWS):
        pl.when(sched_ref[1, b] > j)(functools.partial(sub_block, j))


def _expert_down(hs, w_down, sched, n_used, n_steps, pitch):
    p_rows, ff = hs.shape
    d = w_down.shape[2]
    grid_spec = pltpu.PrefetchScalarGridSpec(
        num_scalar_prefetch=2,
        grid=(n_steps,),
        in_specs=[pl.BlockSpec((STEP_ROWS, ff), _row_block), pl.BlockSpec(memory_space=pl.ANY)],
        out_specs=pl.BlockSpec((STEP_ROWS * pitch, LANES), _row_block),
        scratch_shapes=[pltpu.VMEM((2, 1, ff, d), F32),
                        pltpu.VMEM((2 * EXPERT_ROWS * pitch, LANES), F32),
                        pltpu.SemaphoreType.DMA((2,))],
    )
    return pl.pallas_call(
        _expert_down_kernel,
        grid_spec=grid_spec,
        out_shape=jax.ShapeDtypeStruct((p_rows * pitch, LANES), U32),
        compiler_params=_cp(("arbitrary",)),
        name="expert_down",
    )(sched, n_used, hs, w_down)


def _combine_kernel(dfirst_ref, dnext_ref, gate_ref, x_ref, t_ref, wsd_ref, ys_ref, o_ref, buf_ref, scr_ref,
                    base_ref, sem, *, tm, pitch):
    t = pl.program_id(0)
    pairs = TOP_K * tm
    bpitch = buf_ref.shape[0] // (2 * pairs)

    def row_copy(dref, slot, p):
        src = ys_ref.at[pl.ds(pl.multiple_of(dref[0, 0, p] * pitch, pitch), pitch), :]
        dst = buf_ref.at[pl.ds(pl.multiple_of((slot * pairs + p) * bpitch, SUBLANES), pitch), :]
        return pltpu.make_async_copy(src, dst, sem.at[slot])

    def wait_slot(slot):
        for _ in range(TOP_K):
            pltpu.make_async_copy(ys_ref.at[pl.ds(0, tm * pitch), :], buf_ref.at[pl.ds(0, tm * pitch), :],
                                  sem.at[slot]).wait()

    def issue_tile(dref, slot):
        def body(g, c):
            for u in range(DMA_UNROLL):
                row_copy(dref, slot, g * DMA_UNROLL + u).start(priority=u % 2)
            return c

        lax.fori_loop(0, pairs // DMA_UNROLL, body, 0)

    @pl.when(t == 0)
    def _():
        issue_tile(dfirst_ref, 0)

    @pl.when(t + 1 < pl.num_programs(0))
    def _():
        issue_tile(dnext_ref, (t + 1) % 2)

    base_ref[...] = x_ref[...] + jnp.dot(t_ref[...], wsd_ref[...], preferred_element_type=F32)

    cur = t % 2
    wait_slot(cur)

    for grp in range(tm // COMBINE_GROUP):
        r0 = grp * COMBINE_GROUP
        rows = slice(r0, r0 + COMBINE_GROUP)
        gates = [jnp.broadcast_to(gate_ref[2 * r0:2 * (r0 + COMBINE_GROUP), k:k + 1], (2 * COMBINE_GROUP, LANES))
                 for k in range(TOP_K)]
        for s in range(pitch):
            acc = None
            for k in range(TOP_K):
                term = gates[k] * _load_pairs(buf_ref, cur * pairs + k * tm + r0, COMBINE_GROUP, bpitch, s)
                acc = term if acc is None else acc + term
            even, odd = _split_pairs(acc, scr_ref, grp * pitch + s)
            c0 = slice((2 * s) * LANES, (2 * s + 1) * LANES)
            c1 = slice((2 * s + 1) * LANES, (2 * s + 2) * LANES)
            o_ref[rows, c0] = base_ref[rows, c0] + even
            o_ref[rows, c1] = base_ref[rows, c1] + odd


def _combine(dest, gate_rows, x, t_shared, w_sh_down, ys, pitch, tm=128):
    n, d = x.shape
    ff = t_shared.shape[1]
    tiles = n // tm
    dest_t = dest.reshape(TOP_K, tiles, tm).transpose(1, 0, 2).reshape(tiles, 1, TOP_K * tm)
    dspec = lambda imap: pl.BlockSpec((1, 1, TOP_K * tm), imap, memory_space=pltpu.SMEM)
    bpitch = pitch if (pitch // SUBLANES) % 2 else pitch + SUBLANES
    return pl.pallas_call(
        functools.partial(_combine_kernel, tm=tm, pitch=pitch),
        grid=(tiles,),
        in_specs=[dspec(lambda i: (0, 0, 0)),
                  dspec(lambda i: (jnp.minimum(i + 1, tiles - 1), 0, 0)),
                  pl.BlockSpec((2 * tm, TOP_K), lambda i: (i, 0)),
                  pl.BlockSpec((tm, d), lambda i: (i, 0)),
                  pl.BlockSpec((tm, ff), lambda i: (i, 0)),
                  pl.BlockSpec((ff, d), lambda i: (0, 0)),
                  pl.BlockSpec(memory_space=pl.ANY)],
        out_specs=pl.BlockSpec((tm, d), lambda i: (i, 0)),
        out_shape=jax.ShapeDtypeStruct((n, d), F32),
        scratch_shapes=[pltpu.VMEM((2 * TOP_K * tm * bpitch, LANES), U32),
                        pltpu.VMEM((2 * tm * pitch, LANES), F32),
                        pltpu.VMEM((tm, d), F32),
                        pltpu.SemaphoreType.DMA((2,))],
        compiler_params=_cp(("arbitrary",)),
        name="moe_combine",
    )(dest_t, dest_t, gate_rows, x, t_shared, w_sh_down.astype(BF16), ys)


def _rope_tables(positions):
    half = ROPE_DIM // 2
    inv_freq = ROPE_THETA ** (-jnp.arange(0, ROPE_DIM, 2, dtype=F32) / ROPE_DIM)
    ang = positions.reshape(-1).astype(F32)[:, None] * inv_freq
    cos, sin = jnp.cos(ang), jnp.sin(ang)
    n = ang.shape[0]
    z = lambda w: jnp.zeros((n, w), F32)
    cos_t = jnp.concatenate([cos, cos, z(LANES - ROPE_DIM)], axis=1)
    sin_a = jnp.concatenate([-sin, z(LANES - half)], axis=1)
    sin_b = jnp.concatenate([z(half), sin, z(LANES - ROPE_DIM)], axis=1)
    return cos_t, sin_a, sin_b


def _pad_lanes(v, width):
    return jnp.concatenate([v, jnp.zeros((width - v.shape[0],), v.dtype)])


def kernel(x, mem, positions, g_mix, w_in, g_qa, g_ka, rel_bias, g_cq, w_uq, g_ckv, w_ukv, g_qb, g_kb, w_o, g_cross, g_mem, w_xq, w_xkv, g_qx, g_kx, w_xo, g_ffn, w_router, router_bias, w_sh_gate, w_sh_up, w_sh_down, w_ex_gate, w_ex_up, w_ex_down):
    batch, seq, d = x.shape
    n = batch * seq
    mem_tokens = mem.shape[1]
    width_a = HEADS_A * HEAD_DIM
    x2d = x.reshape(n, d)

    h = _norm_bf16(x2d, g_mix)
    gs_qk = jnp.concatenate([jnp.tile(g_qa * HEAD_DIM ** -0.5, HEADS_A), jnp.tile(g_ka, HEADS_A)])
    w_in_t = w_in.T
    qk, v_a, c = _in_proj_all(h, w_in_t, gs_qk, 2 * width_a, width_a, Q_LORA + KV_LORA)
    w_kpe = jnp.pad(w_in_t[3 * width_a + Q_LORA + KV_LORA:], ((0, LANES - ROPE_DIM), (0, 0)))
    kpe = _proj_t(h, w_kpe, "in_proj_kpe")

    o_a = _attn_a(qk, v_a, _band_bias(rel_bias), batch, seq)

    cos_t, sin_a, sin_b = _rope_tables(positions)
    w_uq_pad = jnp.pad(w_uq.reshape(Q_LORA, HEADS_B, QK_B),
                       ((0, 0), (0, 0), (0, 2 * LANES - QK_B))).reshape(Q_LORA, HEADS_B * 2 * LANES)
    gq = _pad_lanes(g_qb, 2 * LANES).reshape(1, 2 * LANES)
    gk = _pad_lanes(g_kb, 2 * LANES).reshape(1, 2 * LANES)
    qf = _mla_q(c, g_cq, w_uq_pad, gq, cos_t, sin_a, sin_b)
    kf, v_b = _mla_kv(c, g_ckv, kpe, w_ukv, gk, cos_t, sin_a, sin_b)
    o_b = _attn_b(qf, kf, v_b, batch, seq)

    x1 = _out_proj(o_a, o_b, w_o, x2d)

    hw = MEM_HEADS * MEM_HEAD_DIM
    kvx = _normed_proj(mem.reshape(batch * mem_tokens, d), g_mem, w_xkv,
                       jnp.concatenate([jnp.tile(g_kx, MEM_HEADS), jnp.ones((hw,), F32)]), hw, "cross_kv")
    x2 = _cross_attn(x1, g_cross, w_xq, jnp.tile(g_qx * MEM_HEAD_DIM ** -0.5, MEM_HEADS), kvx, w_xo,
                     batch, seq, mem_tokens)

    hp, eid, gate, rank, cnt, t_shared = _router(x2, g_ffn, w_router, router_bias, w_sh_gate, w_sh_up)
    counts = cnt[:, 0]
    nstep = (counts + STEP_ROWS - 1) // STEP_ROWS
    step_end = jnp.cumsum(nstep).astype(I32)
    step_start = step_end - nstep
    n_steps = n * TOP_K // STEP_ROWS + N_EXPERTS
    experts = jnp.arange(N_EXPERTS, dtype=I32)
    start_of = jnp.sum(jnp.where(eid[:, :, None] == experts, step_start, 0), axis=-1)
    dest = start_of * STEP_ROWS + rank
    steps = jnp.arange(n_steps, dtype=I32)
    step_e = jnp.minimum(jnp.sum((step_end[None, :] <= steps[:, None]).astype(I32), axis=1), N_EXPERTS - 1)
    mine = step_e[:, None] == experts[None, :]
    rows_left = jnp.sum(jnp.where(mine, counts[None, :] - STEP_ROWS * (steps[:, None] - step_start[None, :]), 0),
                        axis=1)
    used = steps < step_end[-1]
    rows_here = jnp.where(used, jnp.clip(rows_left, 0, STEP_ROWS), 0)
    nsub = (rows_here + EXPERT_ROWS - 1) // EXPERT_ROWS
    nonempty = counts > 0
    ring_slot = (jnp.cumsum(nonempty.astype(I32)) - 1) % 2
    later = (experts[None, :] > experts[:, None]) & nonempty[None, :]
    succ = jnp.min(jnp.where(later, experts[None, :], N_EXPERTS), axis=1)
    succ = jnp.where(succ == N_EXPERTS, -1, succ)
    per_step = lambda v: jnp.sum(jnp.where(mine, v[None, :], 0), axis=1)
    first = (used & (steps == per_step(step_start))).astype(I32)
    sched = jnp.stack([step_e, nsub, first, per_step(ring_slot), per_step(succ)]).astype(I32)
    n_used = step_end[-1:]
    pitch = d // 2 // LANES

    xs = _dispatch(hp, dest, counts, step_start, n_steps * STEP_ROWS, pitch)
    hs = _expert_up(xs, w_ex_gate, w_ex_up, sched, n_used, n_steps, pitch)
    ys = _expert_down(hs, w_ex_down, sched, n_used, n_steps, pitch)

    out = _combine(dest, jnp.repeat(gate.T, 2, axis=0), x2, t_shared, w_sh_down, ys, pitch)
    return out.reshape(batch, seq, d)
```

```python
import functools

import jax
import jax.numpy as jnp
import numpy as np
from jax import lax
from jax.experimental import pallas as pl
from jax.experimental.pallas import tpu as pltpu

F32 = jnp.float32
BF16 = jnp.bfloat16
I32 = jnp.int32
U32 = jnp.uint32

CHUNK = 64
LEFT_CHUNKS = 8
REL_CLIP = 128
HEAD_DIM = 128
HEADS_A = 16
HEADS_B = 16
Q_LORA = 1024
KV_LORA = 512
NOPE_DIM = 128
ROPE_DIM = 64
V_DIM = 128
QK_B = NOPE_DIM + ROPE_DIM
ROPE_THETA = 10000.0
MEM_HEADS = 4
MEM_HEAD_DIM = 128
N_EXPERTS = 64
N_GROUPS = 8
TOPK_GROUPS = 4
TOP_K = 8
ROUTED_SCALE = 2.5
EPS = 1e-6

LANES = 128
SUBLANES = 8
ATT_BLOCK = 256
EXPERT_ROWS = 256
STEP_ROWS = 2 * EXPERT_ROWS
DMA_UNROLL = 8
COMBINE_GROUP = 16
NEG = -1e30
VMEM_LIMIT = 56 * 1024 * 1024

_NT = (((1,), (1,)), ((), ()))


def _cp(sem, vmem=VMEM_LIMIT):
    return pltpu.CompilerParams(dimension_semantics=sem, vmem_limit_bytes=vmem)


def _rms(x, g):
    return x * lax.rsqrt(jnp.mean(x * x, axis=-1, keepdims=True) + EPS) * g


def _pack_store(ref, first_token, x, scr):
    m, w = x.shape
    pitch = w // (2 * LANES)
    regions = scr.shape[0] // (2 * m)
    for s in range(pitch):
        base = (s % regions) * 2 * m
        scr[pl.ds(base, m, stride=2), :] = x[:, (2 * s) * LANES:(2 * s + 1) * LANES]
        scr[pl.ds(base + 1, m, stride=2), :] = x[:, (2 * s + 1) * LANES:(2 * s + 2) * LANES]
        z = scr[pl.ds(base, 2 * m), :].astype(BF16)
        ref[pl.ds(first_token * pitch + s, m, stride=pitch), :] = pltpu.bitcast(z, U32)


def _load_pairs(ref, first_token, m, pitch, s):
    w = ref[pl.ds(first_token * pitch + s, m, stride=pitch), :]
    return pltpu.bitcast(w, BF16).astype(F32)


def _split_pairs(z, scr, region):
    m = z.shape[0] // 2
    base = region * 2 * m
    scr[pl.ds(base, 2 * m), :] = z
    return scr[pl.ds(base, m, stride=2), :], scr[pl.ds(base + 1, m, stride=2), :]


def _load_unpacked(ref, first_token, m, pitch, scr, dtype=F32):
    regions = scr.shape[0] // (2 * m)
    cols = []
    for s in range(pitch):
        for piece in _split_pairs(_load_pairs(ref, first_token, m, pitch, s), scr, s % regions):
            cols.append(piece.astype(dtype))
    return jnp.concatenate(cols, axis=1)


def _norm_kernel(x_ref, g_ref, o_ref):
    o_ref[...] = _rms(x_ref[...], g_ref[...]).astype(o_ref.dtype)


def _norm_bf16(x, g, tm=256):
    n, d = x.shape
    return pl.pallas_call(
        _norm_kernel,
        grid=(n // tm,),
        in_specs=[pl.BlockSpec((tm, d), lambda i: (i, 0)), pl.BlockSpec((1, d), lambda i: (0, 0))],
        out_specs=pl.BlockSpec((tm, d), lambda i: (i, 0)),
        out_shape=jax.ShapeDtypeStruct((n, d), BF16),
        compiler_params=_cp(("parallel",)),
        name="norm_mix",
    )(x, g.reshape(1, d))


def _proj_t_kernel(h_ref, w_ref, o_ref):
    o_ref[...] = lax.dot_general(h_ref[...], w_ref[...].astype(BF16), _NT, preferred_element_type=F32)


def _proj_t(h, w_t, name, tm=1024):
    n, d = h.shape
    nc = w_t.shape[0]
    return pl.pallas_call(
        _proj_t_kernel,
        grid=(n // tm,),
        in_specs=[pl.BlockSpec((tm, d), lambda i: (i, 0)), pl.BlockSpec((nc, d), lambda i: (0, 0))],
        out_specs=pl.BlockSpec((tm, nc), lambda i: (i, 0)),
        out_shape=jax.ShapeDtypeStruct((n, nc), F32),
        compiler_params=_cp(("parallel",)),
        name=name,
    )(h, w_t)


def _in_proj_all_kernel(h_ref, w_ref, gs_ref, qk_ref, v_ref, c_ref, *, tn, n_qk, n_v):
    j = pl.program_id(1)
    acc = lax.dot_general(h_ref[...], w_ref[...].astype(BF16), _NT, preferred_element_type=F32)

    @pl.when(j < n_qk)
    def _():
        for c in range(tn // LANES):
            sl = slice(c * LANES, (c + 1) * LANES)
            qk_ref[:, sl] = _rms(acc[:, sl], gs_ref[:, sl]).astype(qk_ref.dtype)

    @pl.when((j >= n_qk) & (j < n_qk + n_v))
    def _():
        v_ref[...] = acc.astype(v_ref.dtype)

    @pl.when(j >= n_qk + n_v)
    def _():
        c_ref[...] = acc


def _in_proj_all(h, w_t, gs_qk, w_qk, w_v, w_c, tm=1024, tn=512):
    n, d = h.shape
    n_qk, n_v, n_c = w_qk // tn, w_v // tn, w_c // tn
    return pl.pallas_call(
        functools.partial(_in_proj_all_kernel, tn=tn, n_qk=n_qk, n_v=n_v),
        grid=(n // tm, n_qk + n_v + n_c),
        in_specs=[pl.BlockSpec((tm, d), lambda i, j: (i, 0)),
                  pl.BlockSpec((tn, d), lambda i, j: (j, 0)),
                  pl.BlockSpec((1, tn), lambda i, j: (0, jnp.minimum(j, n_qk - 1)))],
        out_specs=[pl.BlockSpec((tm, tn), lambda i, j: (i, jnp.minimum(j, n_qk - 1))),
                   pl.BlockSpec((tm, tn), lambda i, j: (i, jnp.clip(j - n_qk, 0, n_v - 1))),
                   pl.BlockSpec((tm, tn), lambda i, j: (i, jnp.clip(j - n_qk - n_v, 0, n_c - 1)))],
        out_shape=[jax.ShapeDtypeStruct((n, w_qk), BF16), jax.ShapeDtypeStruct((n, w_v), BF16),
                   jax.ShapeDtypeStruct((n, w_c), F32)],
        compiler_params=_cp(("parallel", "arbitrary")),
        name="in_proj",
    )(h, w_t, gs_qk.reshape(1, w_qk))


def _attn_a_kernel(q_ref, k0_ref, k1_ref, k2_ref, v0_ref, v1_ref, v2_ref, b_ref, o_ref, *, heads):
    i = pl.program_id(2)
    k_refs = (k0_ref, k1_ref, k2_ref)
    v_refs = (v0_ref, v1_ref, v2_ref)
    for hh in range(heads):
        sl = slice(hh * HEAD_DIM, (hh + 1) * HEAD_DIM)
        q = q_ref[:, sl]
        s = []
        for d in range(3):
            sd = lax.dot_general(q, k_refs[d][:, sl], _NT, preferred_element_type=F32)
            sd = sd + b_ref[hh, :, d * ATT_BLOCK:(d + 1) * ATT_BLOCK]
            if d > 0:
                sd = jnp.where(i >= d, sd, NEG)
            s.append(sd)
        m = jnp.maximum(jnp.maximum(s[0].max(-1, keepdims=True), s[1].max(-1, keepdims=True)),
                        s[2].max(-1, keepdims=True))
        l = jnp.zeros_like(m)
        o = jnp.zeros((q.shape[0], HEAD_DIM), F32)
        for d in range(3):
            p = jnp.exp(s[d] - m)
            l = l + p.sum(-1, keepdims=True)
            o = o + jnp.dot(p.astype(BF16), v_refs[d][:, sl], preferred_element_type=F32)
        o_ref[:, sl] = (o / l).astype(o_ref.dtype)


def _band_bias(rel_bias):
    blk = ATT_BLOCK
    period = 2 * blk + 1
    heads = rel_bias.shape[0]
    r = np.arange(blk)[:, None]
    c = np.arange(blk)[None, :]
    per_blk = blk // CHUNK
    k = np.arange(period)
    delta = np.where(k <= blk, k, k - period)
    tiles = []
    for d in range(3):
        idx = np.clip(blk * d - delta, -(CHUNK - 1), REL_CLIP) + CHUNK - 1
        w = rel_bias[:, idx].astype(F32)
        b = jnp.tile(w, (1, blk))[:, :blk * (period - 1)].reshape(heads, blk, period - 1)[:, :, :blk]
        cdiff = per_blk * d + r // CHUNK - c // CHUNK
        valid = (cdiff >= 0) & (cdiff <= LEFT_CHUNKS)
        tiles.append(jnp.where(valid[None], b, NEG))
    return jnp.concatenate(tiles, axis=-1)


def _attn_a(qk, v, bias, batch, seq, heads_per_step=8):
    n = qk.shape[0]
    nq = seq // ATT_BLOCK
    hw = heads_per_step * HEAD_DIM
    groups = HEADS_A // heads_per_step
    kcol0 = HEADS_A * HEAD_DIM // hw

    def kspec(d, col0):
        return pl.BlockSpec((ATT_BLOCK, hw), lambda g, b, i: (b * nq + jnp.maximum(i - d, 0), col0 + g))

    return pl.pallas_call(
        functools.partial(_attn_a_kernel, heads=heads_per_step),
        grid=(groups, batch, nq),
        in_specs=[pl.BlockSpec((ATT_BLOCK, hw), lambda g, b, i: (b * nq + i, g)),
                  kspec(0, kcol0), kspec(1, kcol0), kspec(2, kcol0),
                  kspec(0, 0), kspec(1, 0), kspec(2, 0),
                  pl.BlockSpec((heads_per_step, ATT_BLOCK, 3 * ATT_BLOCK), lambda g, b, i: (g, 0, 0))],
        out_specs=pl.BlockSpec((ATT_BLOCK, hw), lambda g, b, i: (b * nq + i, g)),
        out_shape=jax.ShapeDtypeStruct((n, HEADS_A * HEAD_DIM), BF16),
        compiler_params=_cp(("parallel", "parallel", "arbitrary")),
        name="attn_band",
    )(qk, qk, qk, qk, v, v, v, bias)


def _rope_pe(pe, g, cos_ref, sa_ref, sb_ref):
    ss = jnp.sum(pe * pe, axis=-1, keepdims=True) * (1.0 / ROPE_DIM)
    pn = pe * lax.rsqrt(ss + EPS) * g
    half = ROPE_DIM // 2
    return (pn * cos_ref[...] + pltpu.roll(pn, LANES - half, 1) * sa_ref[...]
            + pltpu.roll(pn, half, 1) * sb_ref[...])


def _mla_q_kernel(c_ref, gc_ref, w_ref, gq_ref, cos_ref, sa_ref, sb_ref, o_ref, xn_ref, *, heads, scale):
    @pl.when(pl.program_id(1) == 0)
    def _():
        xn_ref[...] = _rms(c_ref[...], gc_ref[...]).astype(BF16)

    acc = jnp.dot(xn_ref[...], w_ref[...].astype(BF16), preferred_element_type=F32)
    for hh in range(heads):
        base = hh * 2 * LANES
        nope = acc[:, base:base + LANES]
        pe = acc[:, base + LANES:base + 2 * LANES]
        o_ref[:, base:base + LANES] = (_rms(nope, gq_ref[:, :LANES]) * scale).astype(o_ref.dtype)
        o_ref[:, base + LANES:base + 2 * LANES] = (
            _rope_pe(pe, gq_ref[:, LANES:], cos_ref, sa_ref, sb_ref) * scale).astype(o_ref.dtype)


def _mla_q(c, g_cq, w_uq_pad, gq, cos_t, sin_a, sin_b, tm=512, heads_per_step=4):
    n = c.shape[0]
    tn = heads_per_step * 2 * LANES
    ncols = w_uq_pad.shape[1]
    row = lambda i, j: (i, 0)
    return pl.pallas_call(
        functools.partial(_mla_q_kernel, heads=heads_per_step, scale=QK_B ** -0.5),
        grid=(n // tm, ncols // tn),
        in_specs=[pl.BlockSpec((tm, Q_LORA), row),
                  pl.BlockSpec((1, Q_LORA), lambda i, j: (0, 0)),
                  pl.BlockSpec((Q_LORA, tn), lambda i, j: (0, j)),
                  pl.BlockSpec((1, 2 * LANES), lambda i, j: (0, 0)),
                  pl.BlockSpec((tm, LANES), row), pl.BlockSpec((tm, LANES), row), pl.BlockSpec((tm, LANES), row)],
        out_specs=pl.BlockSpec((tm, tn), lambda i, j: (i, j)),
        out_shape=jax.ShapeDtypeStruct((n, ncols), BF16),
        scratch_shapes=[pltpu.VMEM((tm, Q_LORA), BF16)],
        compiler_params=_cp(("parallel", "arbitrary")),
        name="mla_q_proj",
    )(c, g_cq.reshape(1, Q_LORA), w_uq_pad, gq, cos_t, sin_a, sin_b)


def _mla_kv_kernel(c_ref, gc_ref, kpe_ref, w_ref, gk_ref, cos_ref, sa_ref, sb_ref, k_ref, v_ref,
                   xn_ref, pe_ref, *, heads):
    @pl.when(pl.program_id(1) == 0)
    def _():
        xn_ref[...] = _rms(c_ref[...], gc_ref[...]).astype(BF16)
        pe_ref[...] = _rope_pe(kpe_ref[...], gk_ref[:, LANES:], cos_ref, sa_ref, sb_ref).astype(BF16)

    acc = jnp.dot(xn_ref[...], w_ref[...].astype(BF16), preferred_element_type=F32)
    for hh in range(heads):
        base = hh * 2 * LANES
        k_ref[:, base:base + LANES] = _rms(acc[:, base:base + LANES], gk_ref[:, :LANES]).astype(k_ref.dtype)
        k_ref[:, base + LANES:base + 2 * LANES] = pe_ref[...]
        v_ref[:, hh * LANES:(hh + 1) * LANES] = acc[:, base + LANES:base + 2 * LANES].astype(v_ref.dtype)


def _mla_kv(c, g_ckv, kpe, w_ukv, gk, cos_t, sin_a, sin_b, tm=512, heads_per_step=4):
    n = c.shape[0]
    tn = heads_per_step * 2 * LANES
    ncols = w_ukv.shape[1]
    cblk = Q_LORA // KV_LORA
    row = lambda i, j: (i, 0)
    return pl.pallas_call(
        functools.partial(_mla_kv_kernel, heads=heads_per_step),
        grid=(n // tm, ncols // tn),
        in_specs=[pl.BlockSpec((tm, KV_LORA), lambda i, j: (i, cblk)),
                  pl.BlockSpec((1, KV_LORA), lambda i, j: (0, 0)),
                  pl.BlockSpec((tm, LANES), row),
                  pl.BlockSpec((KV_LORA, tn), lambda i, j: (0, j)),
                  pl.BlockSpec((1, 2 * LANES), lambda i, j: (0, 0)),
                  pl.BlockSpec((tm, LANES), row), pl.BlockSpec((tm, LANES), row), pl.BlockSpec((tm, LANES), row)],
        out_specs=[pl.BlockSpec((tm, tn), lambda i, j: (i, j)),
                   pl.BlockSpec((tm, tn // 2), lambda i, j: (i, j))],
        out_shape=[jax.ShapeDtypeStruct((n, ncols), BF16),
                   jax.ShapeDtypeStruct((n, ncols // 2), BF16)],
        scratch_shapes=[pltpu.VMEM((tm, KV_LORA), BF16), pltpu.VMEM((tm, LANES), BF16)],
        compiler_params=_cp(("parallel", "arbitrary")),
        name="mla_kv_proj",
    )(c, g_ckv.reshape(1, KV_LORA), kpe, w_ukv, gk, cos_t, sin_a, sin_b)


def _attn_b_kernel(q_ref, k_ref, v_ref, o_ref, *, heads):
    i = pl.program_id(2)
    tq = q_ref.shape[0]
    qw = 2 * LANES
    qs = [q_ref[:, h * qw:(h + 1) * qw] for h in range(heads)]

    def step(first_blk, width, carry, masked):
        start = pl.multiple_of(first_blk * ATT_BLOCK, ATT_BLOCK)
        if masked:
            per_blk = ATT_BLOCK // CHUNK
            r = i * per_blk + lax.broadcasted_iota(I32, (tq, width), 0) // CHUNK
            c = first_blk * per_blk + lax.broadcasted_iota(I32, (tq, width), 1) // CHUNK
            keep = c <= r
        out = []
        for h in range(heads):
            m, l, acc = carry[h]
            s = lax.dot_general(qs[h], k_ref[pl.ds(start, width), h * qw:(h + 1) * qw], _NT,
                                preferred_element_type=F32)
            if masked:
                s = jnp.where(keep, s, NEG)
            m_new = jnp.maximum(m, s.max(-1, keepdims=True))
            alpha = jnp.exp(m - m_new)
            p = jnp.exp(s - m_new)
            l = alpha * l + p.sum(-1, keepdims=True)
            acc = alpha * acc + jnp.dot(p.astype(BF16), v_ref[pl.ds(start, width), h * V_DIM:(h + 1) * V_DIM],
                                        preferred_element_type=F32)
            out.append((m_new, l, acc))
        return tuple(out)

    def finish(final):
        for h in range(heads):
            _, l, acc = final[h]
            o_ref[:, h * V_DIM:(h + 1) * V_DIM] = (acc / l).astype(o_ref.dtype)

    init = tuple((jnp.full((tq, 1), NEG, F32), jnp.zeros((tq, 1), F32), jnp.zeros((tq, V_DIM), F32))
                 for _ in range(heads))
    carry = lax.fori_loop(0, i // 2, lambda j, c: step(2 * j, 2 * ATT_BLOCK, c, False), init)

    @pl.when(i % 2 == 1)
    def _():
        finish(step(i - 1, 2 * ATT_BLOCK, carry, True))

    @pl.when(i % 2 == 0)
    def _():
        finish(step(i, ATT_BLOCK, carry, True))


def _attn_b(qf, kf, vb, batch, seq, heads_per_step=8):
    n = qf.shape[0]
    nq = seq // ATT_BLOCK
    qw = heads_per_step * 2 * LANES
    vw = heads_per_step * V_DIM
    return pl.pallas_call(
        functools.partial(_attn_b_kernel, heads=heads_per_step),
        grid=(batch, HEADS_B // heads_per_step, nq),
        in_specs=[pl.BlockSpec((ATT_BLOCK, qw), lambda b, g, i: (b * nq + i, g)),
                  pl.BlockSpec((seq, qw), lambda b, g, i: (b, g)),
                  pl.BlockSpec((seq, vw), lambda b, g, i: (b, g))],
        out_specs=pl.BlockSpec((ATT_BLOCK, vw), lambda b, g, i: (b * nq + i, g)),
        out_shape=jax.ShapeDtypeStruct((n, HEADS_B * V_DIM), BF16),
        compiler_params=_cp(("parallel", "parallel", "arbitrary")),
        name="attn_latent",
    )(qf, kf, vb)


def _out_proj_kernel(oa_ref, ob_ref, w_ref, x_ref, o_ref, wb_ref):
    @pl.when(pl.program_id(1) == 0)
    def _():
        wb_ref[...] = w_ref[...].astype(BF16)

    ka = oa_ref.shape[1]
    acc = jnp.dot(oa_ref[...], wb_ref[:ka, :], preferred_element_type=F32)
    acc = acc + jnp.dot(ob_ref[...], wb_ref[ka:, :], preferred_element_type=F32)
    o_ref[...] = x_ref[...] + acc


def _out_proj(oa, ob, w_o, x, tm=256, tn=1024):
    n, ka = oa.shape
    kb = ob.shape[1]
    d = w_o.shape[1]
    return pl.pallas_call(
        _out_proj_kernel,
        grid=(d // tn, n // tm),
        in_specs=[pl.BlockSpec((tm, ka), lambda j, i: (i, 0)),
                  pl.BlockSpec((tm, kb), lambda j, i: (i, 0)),
                  pl.BlockSpec((ka + kb, tn), lambda j, i: (0, j)),
                  pl.BlockSpec((tm, tn), lambda j, i: (i, j))],
        out_specs=pl.BlockSpec((tm, tn), lambda j, i: (i, j)),
        out_shape=jax.ShapeDtypeStruct((n, d), F32),
        scratch_shapes=[pltpu.VMEM((ka + kb, tn), BF16)],
        compiler_params=_cp(("parallel", "arbitrary")),
        name="out_proj",
    )(oa, ob, w_o, x)


def _normed_proj_kernel(x_ref, g_ref, w_ref, gs_ref, o_ref, wb_ref, *, norm_cols):
    @pl.when(pl.program_id(0) == 0)
    def _():
        wb_ref[...] = w_ref[...].astype(BF16)

    h = _rms(x_ref[...], g_ref[...]).astype(BF16)
    acc = jnp.dot(h, wb_ref[...], preferred_element_type=F32)
    for c in range(acc.shape[1] // LANES):
        sl = slice(c * LANES, (c + 1) * LANES)
        if c * LANES < norm_cols:
            o_ref[:, sl] = _rms(acc[:, sl], gs_ref[:, sl]).astype(o_ref.dtype)
        else:
            o_ref[:, sl] = acc[:, sl].astype(o_ref.dtype)


def _normed_proj(x, g, w, gs, norm_cols, name, tm=256):
    n, d = x.shape
    nc = w.shape[1]
    return pl.pallas_call(
        functools.partial(_normed_proj_kernel, norm_cols=norm_cols),
        grid=(n // tm,),
        in_specs=[pl.BlockSpec((tm, d), lambda i: (i, 0)),
                  pl.BlockSpec((1, d), lambda i: (0, 0)),
                  pl.BlockSpec((d, nc), lambda i: (0, 0)),
                  pl.BlockSpec((1, nc), lambda i: (0, 0))],
        out_specs=pl.BlockSpec((tm, nc), lambda i: (i, 0)),
        out_shape=jax.ShapeDtypeStruct((n, nc), BF16),
        scratch_shapes=[pltpu.VMEM((d, nc), BF16)],
        compiler_params=_cp(("arbitrary",)),
        name=name,
    )(x, g.reshape(1, d), w, gs.reshape(1, nc))


def _cross_kernel(x_ref, g_ref, wq_ref, gq_ref, k_ref, v_ref, wo_ref, o_ref):
    x = x_ref[...]
    q = jnp.dot(_rms(x, g_ref[...]).astype(BF16), wq_ref[...], preferred_element_type=F32)
    outs = []
    for hh in range(MEM_HEADS):
        sl = slice(hh * MEM_HEAD_DIM, (hh + 1) * MEM_HEAD_DIM)
        qh = _rms(q[:, sl], gq_ref[:, sl]).astype(BF16)
        s = lax.dot_general(qh, k_ref[:, sl], _NT, preferred_element_type=F32)
        p = jnp.exp(s - s.max(-1, keepdims=True))
        o = jnp.dot(p.astype(BF16), v_ref[:, sl], preferred_element_type=F32)
        outs.append((o / p.sum(-1, keepdims=True)).astype(BF16))
    acc = x
    for hh in range(MEM_HEADS):
        sl = slice(hh * MEM_HEAD_DIM, (hh + 1) * MEM_HEAD_DIM)
        acc = acc + jnp.dot(outs[hh], wo_ref[sl, :], preferred_element_type=F32)
    o_ref[...] = acc


def _cross_attn(x, g, w_xq, gq, kvx, w_xo, batch, seq, mem_tokens, tm=256):
    n, d = x.shape
    per_b = seq // tm
    hw = MEM_HEADS * MEM_HEAD_DIM
    const = lambda b, i: (0, 0)
    return pl.pallas_call(
        _cross_kernel,
        grid=(batch, per_b),
        in_specs=[pl.BlockSpec((tm, d), lambda b, i: (b * per_b + i, 0)),
                  pl.BlockSpec((1, d), const),
                  pl.BlockSpec((d, hw), const),
                  pl.BlockSpec((1, hw), const),
                  pl.BlockSpec((mem_tokens, hw), lambda b, i: (b, 0)),
                  pl.BlockSpec((mem_tokens, hw), lambda b, i: (b, 1)),
                  pl.BlockSpec((hw, d), const)],
        out_specs=pl.BlockSpec((tm, d), lambda b, i: (b * per_b + i, 0)),
        out_shape=jax.ShapeDtypeStruct((n, d), F32),
        compiler_params=_cp(("parallel", "arbitrary")),
        name="cross_attn",
    )(x, g.reshape(1, d), w_xq.astype(BF16), gq.reshape(1, hw), kvx, kvx, w_xo.astype(BF16))


def _router_kernel(x_ref, g_ref, wr_ref, b_ref, tri_ref, wg_ref, wu_ref, hp_ref, eid_ref, gate_ref, rank_ref,
                   cnt_ref, ts_ref, run_ref, scr_ref):
    t = pl.program_id(0)

    @pl.when(t == 0)
    def _():
        run_ref[...] = jnp.zeros_like(run_ref)

    h = _rms(x_ref[...], g_ref[...])
    _pack_store(hp_ref, 0, h, scr_ref)
    hb = h.astype(BF16)
    ts_ref[...] = _silu_mul(jnp.dot(hb, wg_ref[...], preferred_element_type=F32),
                            jnp.dot(hb, wu_ref[...], preferred_element_type=F32)).astype(ts_ref.dtype)
    tm = h.shape[0]
    per_g = N_EXPERTS // N_GROUPS

    logits = lax.dot_general(wr_ref[...], h, _NT, precision=lax.Precision.HIGHEST,
                             preferred_element_type=F32)
    scores = 1.0 / (1.0 + jnp.exp(-logits))
    choice = scores + b_ref[...]

    sub = lax.broadcasted_iota(I32, (per_g, tm), 0).astype(F32)
    rows = []
    for g in range(N_GROUPS):
        c = choice[g * per_g:(g + 1) * per_g, :]
        m1 = c.max(0, keepdims=True)
        first = jnp.where(c == m1, sub, float(per_g)).min(0, keepdims=True)
        m2 = jnp.where(sub == first, -jnp.inf, c).max(0, keepdims=True)
        rows.append(m1 + m2)
    gs = jnp.concatenate(rows, axis=0)

    gsub = lax.broadcasted_iota(I32, (N_GROUPS, tm), 0).astype(F32)
    beaten = jnp.zeros((N_GROUPS, tm), F32)
    for g2 in range(N_GROUPS):
        row = gs[g2:g2 + 1, :]
        wins = (row > gs) | ((row == gs) & (gsub > float(g2)))
        beaten = beaten + jnp.where(wins, 1.0, 0.0)
    g_ok = jnp.where(beaten < TOPK_GROUPS, 1.0, 0.0)
    e_ok = jnp.concatenate(
        [jnp.broadcast_to(g_ok[g:g + 1, :], (per_g, tm)) for g in range(N_GROUPS)], axis=0)

    eiota = lax.broadcasted_iota(I32, (N_EXPERTS, tm), 0).astype(F32)
    masked = jnp.where(e_ok > 0.5, choice, -jnp.inf)
    chosen = jnp.zeros((N_EXPERTS, tm), F32)
    eids, ws = [], []
    for _ in range(TOP_K):
        m = masked.max(0, keepdims=True)
        idx = jnp.where(masked == m, eiota, float(N_EXPERTS)).min(0, keepdims=True)
        sel = eiota == idx
        ws.append(jnp.where(sel, scores, 0.0).sum(0, keepdims=True))
        masked = jnp.where(sel, -jnp.inf, masked)
        chosen = jnp.where(sel, 1.0, chosen)
        eids.append(idx)
    wsum = ws[0]
    for w in ws[1:]:
        wsum = wsum + w
    denom = wsum + 1e-20

    pos = jnp.dot(chosen.astype(BF16), tri_ref[...], preferred_element_type=F32) + run_ref[:, 0:1]
    run_ref[...] = run_ref[...] + chosen.sum(1, keepdims=True)
    cnt_ref[...] = run_ref[...].astype(I32)

    for r in range(TOP_K):
        eid_ref[r:r + 1, :] = eids[r].astype(I32)
        gate_ref[r:r + 1, :] = ws[r] / denom * ROUTED_SCALE
        rank_ref[r:r + 1, :] = jnp.where(eiota == eids[r], pos, 0.0).sum(0, keepdims=True).astype(I32)


def _router(x, g, w_router, router_bias, w_sh_gate, w_sh_up, tm=256):
    n, d = x.shape
    ne = w_router.shape[1]
    ff = w_sh_gate.shape[1]
    tri = (jnp.arange(tm)[:, None] < jnp.arange(tm)[None, :]).astype(BF16)
    row8 = lambda i: (0, i)
    pitch = d // 2 // LANES
    return pl.pallas_call(
        _router_kernel,
        grid=(n // tm,),
        in_specs=[pl.BlockSpec((tm, d), lambda i: (i, 0)),
                  pl.BlockSpec((1, d), lambda i: (0, 0)),
                  pl.BlockSpec((ne, d), lambda i: (0, 0)),
                  pl.BlockSpec((ne, 1), lambda i: (0, 0)),
                  pl.BlockSpec((tm, tm), lambda i: (0, 0)),
                  pl.BlockSpec((d, ff), lambda i: (0, 0)), pl.BlockSpec((d, ff), lambda i: (0, 0))],
        out_specs=[pl.BlockSpec((tm * pitch, LANES), lambda i: (i, 0)),
                   pl.BlockSpec((TOP_K, tm), row8), pl.BlockSpec((TOP_K, tm), row8),
                   pl.BlockSpec((TOP_K, tm), row8),
                   pl.BlockSpec((ne, LANES), lambda i: (0, 0)),
                   pl.BlockSpec((tm, ff), lambda i: (i, 0))],
        out_shape=[jax.ShapeDtypeStruct((n * pitch, LANES), U32),
                   jax.ShapeDtypeStruct((TOP_K, n), I32), jax.ShapeDtypeStruct((TOP_K, n), F32),
                   jax.ShapeDtypeStruct((TOP_K, n), I32),
                   jax.ShapeDtypeStruct((ne, LANES), I32),
                   jax.ShapeDtypeStruct((n, ff), BF16)],
        scratch_shapes=[pltpu.VMEM((ne, LANES), F32), pltpu.VMEM((2 * tm * pitch, LANES), F32)],
        compiler_params=_cp(("arbitrary",)),
        name="router",
    )(x, g.reshape(1, d), w_router.T, router_bias.reshape(ne, 1), tri,
      w_sh_gate.astype(BF16), w_sh_up.astype(BF16))


def _dispatch_kernel(cnt_ref, start_ref, dest_ref, hp_ref, xs_ref, sem, *, tm, pitch):
    t = pl.program_id(0)
    pairs = TOP_K * tm

    def row_copy(r, slot):
        src = hp_ref.at[pl.ds(pl.multiple_of(r * pitch, pitch), pitch), :]
        dst = xs_ref.at[pl.ds(pl.multiple_of(slot * pitch, pitch), pitch), :]
        return pltpu.make_async_copy(src, dst, sem)

    def issue(g, c):
        for u in range(DMA_UNROLL):
            p = g * DMA_UNROLL + u
            row_copy(p & (tm - 1), dest_ref[0, 0, p]).start(priority=u % 2)
        return c

    lax.fori_loop(0, pairs // DMA_UNROLL, issue, 0)
    for _ in range(TOP_K):
        pltpu.make_async_copy(hp_ref, xs_ref.at[pl.ds(0, tm * pitch), :], sem).wait()

    @pl.when(t == pl.num_programs(0) - 1)
    def _():
        def run_copy(first, rows):
            dst = xs_ref.at[pl.ds(pl.multiple_of(first * pitch, pitch), rows * pitch), :]
            return pltpu.make_async_copy(hp_ref.at[pl.ds(0, rows * pitch), :], dst, sem)

        def per_expert(e, c):
            used = cnt_ref[e]
            npad = (-used) & (EXPERT_ROWS - 1)
            first = start_ref[e] * STEP_ROWS + used
            sizes = [1 << k for k in range(EXPERT_ROWS.bit_length() - 1)]
            offs = []
            off = first
            for size in sizes:
                offs.append(off)
                off = off + (npad & size)
            for size, o in zip(sizes, offs):
                pl.when((npad & size) != 0)(lambda size=size, o=o: run_copy(o, size).start())
            for size in sizes:
                pl.when((npad & size) != 0)(lambda size=size: run_copy(0, size).wait())
            return c

        lax.fori_loop(0, N_EXPERTS, per_expert, 0)


def _dispatch(hp, dest, counts, step_start, p_rows, pitch, tm=256):
    n = hp.shape[0] // pitch
    tiles = n // tm
    dest_t = dest.reshape(TOP_K, tiles, tm).transpose(1, 0, 2).reshape(tiles, 1, TOP_K * tm)
    grid_spec = pltpu.PrefetchScalarGridSpec(
        num_scalar_prefetch=2,
        grid=(tiles,),
        in_specs=[pl.BlockSpec((1, 1, TOP_K * tm), lambda i, *_: (i, 0, 0), memory_space=pltpu.SMEM),
                  pl.BlockSpec((tm * pitch, LANES), lambda i, *_: (i, 0))],
        out_specs=pl.BlockSpec(memory_space=pl.ANY),
        scratch_shapes=[pltpu.SemaphoreType.DMA(())],
    )
    return pl.pallas_call(
        functools.partial(_dispatch_kernel, tm=tm, pitch=pitch),
        grid_spec=grid_spec,
        out_shape=jax.ShapeDtypeStruct((p_rows * pitch, LANES), U32),
        compiler_params=_cp(("arbitrary",)),
        name="moe_dispatch",
    )(counts, step_start, dest_t, hp)


def _silu_mul(g, u):
    return g / (1.0 + jnp.exp(-g)) * u


def _expert_weights(sched_ref, w_hbm_refs, wbuf_ref, sem):
    b = pl.program_id(0)

    def copies(e, slot):
        return [pltpu.make_async_copy(w.at[e], wbuf_ref.at[slot, i], sem.at[slot])
                for i, w in enumerate(w_hbm_refs)]

    @pl.when(b == 0)
    def _():
        for c in copies(sched_ref[0, 0], 0):
            c.start(priority=1)

    slot = sched_ref[3, b]

    @pl.when(sched_ref[2, b] == 1)
    def _():
        for c in copies(0, slot):
            c.wait()
        nxt = sched_ref[4, b]

        @pl.when(nxt >= 0)
        def _():
            for c in copies(nxt, 1 - slot):
                c.start(priority=1)

    return slot


def _expert_up_kernel(sched_ref, nu_ref, xs_ref, wg_hbm, wu_hbm, o_ref, wbuf_ref, scr_ref, sem):
    b = pl.program_id(0)
    pitch = xs_ref.shape[0] // STEP_ROWS
    slot = _expert_weights(sched_ref, (wg_hbm, wu_hbm), wbuf_ref, sem)

    def sub_block(j):
        x = _load_unpacked(xs_ref, j * EXPERT_ROWS, EXPERT_ROWS, pitch, scr_ref)
        g = jnp.dot(x, wbuf_ref[slot, 0], preferred_element_type=F32)
        u = jnp.dot(x, wbuf_ref[slot, 1], preferred_element_type=F32)
        o_ref[j * EXPERT_ROWS:(j + 1) * EXPERT_ROWS, :] = _silu_mul(g, u).astype(o_ref.dtype)

    for j in range(STEP_ROWS // EXPERT_ROWS):
        pl.when(sched_ref[1, b] > j)(functools.partial(sub_block, j))


def _row_block(b, sched, nu):
    return (jnp.minimum(b, nu[0] - 1), 0)


def _expert_up(xs, w_gate, w_up, sched, n_used, n_steps, pitch):
    _, d, ff = w_gate.shape
    grid_spec = pltpu.PrefetchScalarGridSpec(
        num_scalar_prefetch=2,
        grid=(n_steps,),
        in_specs=[pl.BlockSpec((STEP_ROWS * pitch, LANES), _row_block),
                  pl.BlockSpec(memory_space=pl.ANY), pl.BlockSpec(memory_space=pl.ANY)],
        out_specs=pl.BlockSpec((STEP_ROWS, ff), _row_block),
        scratch_shapes=[pltpu.VMEM((2, 2, d, ff), F32),
                        pltpu.VMEM((2 * EXPERT_ROWS * pitch, LANES), F32),
                        pltpu.SemaphoreType.DMA((2,))],
    )
    return pl.pallas_call(
        _expert_up_kernel,
        grid_spec=grid_spec,
        out_shape=jax.ShapeDtypeStruct((n_steps * STEP_ROWS, ff), BF16),
        compiler_params=_cp(("arbitrary",)),
        name="expert_up",
    )(sched, n_used, xs, w_gate, w_up)


def _expert_down_kernel(sched_ref, nu_ref, h_ref, wd_hbm, o_ref, wbuf_ref, scr_ref, sem):
    b = pl.program_id(0)
    slot = _expert_weights(sched_ref, (wd_hbm,), wbuf_ref, sem)

    def sub_block(j):
        y = jnp.dot(h_ref[j * EXPERT_ROWS:(j + 1) * EXPERT_ROWS, :], wbuf_ref[slot, 0].astype(BF16),
                    preferred_element_type=F32)
        _pack_store(o_ref, j * EXPERT_ROWS, y, scr_ref)

    for j in range(STEP_ROWS // EXPERT_ROWS):
        pl.when(sched_ref[1, b] > j)(functools.partial(sub_block, j))


def _expert_down(hs, w_down, sched, n_used, n_steps, pitch):
    p_rows, ff = hs.shape
    d = w_down.shape[2]
    grid_spec = pltpu.PrefetchScalarGridSpec(
        num_scalar_prefetch=2,
        grid=(n_steps,),
        in_specs=[pl.BlockSpec((STEP_ROWS, ff), _row_block), pl.BlockSpec(memory_space=pl.ANY)],
        out_specs=pl.BlockSpec((STEP_ROWS * pitch, LANES), _row_block),
        scratch_shapes=[pltpu.VMEM((2, 1, ff, d), F32),
                        pltpu.VMEM((2 * EXPERT_ROWS * pitch, LANES), F32),
                        pltpu.SemaphoreType.DMA((2,))],
    )
    return pl.pallas_call(
        _expert_down_kernel,
        grid_spec=grid_spec,
        out_shape=jax.ShapeDtypeStruct((p_rows * pitch, LANES), U32),
        compiler_params=_cp(("arbitrary",)),
        name="expert_down",
    )(sched, n_used, hs, w_down)


def _combine_kernel(dfirst_ref, dnext_ref, gate_ref, x_ref, t_ref, wsd_ref, ys_ref, o_ref, buf_ref, scr_ref,
                    base_ref, sem, *, tm, pitch):
    t = pl.program_id(0)
    pairs = TOP_K * tm
    bpitch = buf_ref.shape[0] // (2 * pairs)

    def row_copy(dref, slot, p):
        src = ys_ref.at[pl.ds(pl.multiple_of(dref[0, 0, p] * pitch, pitch), pitch), :]
        dst = buf_ref.at[pl.ds(pl.multiple_of((slot * pairs + p) * bpitch, SUBLANES), pitch), :]
        return pltpu.make_async_copy(src, dst, sem.at[slot])

    def wait_slot(slot):
        for _ in range(TOP_K):
            pltpu.make_async_copy(ys_ref.at[pl.ds(0, tm * pitch), :], buf_ref.at[pl.ds(0, tm * pitch), :],
                                  sem.at[slot]).wait()

    def issue_tile(dref, slot):
        def body(g, c):
            for u in range(DMA_UNROLL):
                row_copy(dref, slot, g * DMA_UNROLL + u).start(priority=u % 2)
            return c

        lax.fori_loop(0, pairs // DMA_UNROLL, body, 0)

    @pl.when(t == 0)
    def _():
        issue_tile(dfirst_ref, 0)

    @pl.when(t + 1 < pl.num_programs(0))
    def _():
        issue_tile(dnext_ref, (t + 1) % 2)

    base_ref[...] = x_ref[...] + jnp.dot(t_ref[...], wsd_ref[...], preferred_element_type=F32)

    cur = t % 2
    wait_slot(cur)

    for grp in range(tm // COMBINE_GROUP):
        r0 = grp * COMBINE_GROUP
        rows = slice(r0, r0 + COMBINE_GROUP)
        gates = [jnp.broadcast_to(gate_ref[2 * r0:2 * (r0 + COMBINE_GROUP), k:k + 1], (2 * COMBINE_GROUP, LANES))
                 for k in range(TOP_K)]
        for s in range(pitch):
            acc = None
            for k in range(TOP_K):
                term = gates[k] * _load_pairs(buf_ref, cur * pairs + k * tm + r0, COMBINE_GROUP, bpitch, s)
                acc = term if acc is None else acc + term
            even, odd = _split_pairs(acc, scr_ref, grp * pitch + s)
            c0 = slice((2 * s) * LANES, (2 * s + 1) * LANES)
            c1 = slice((2 * s + 1) * LANES, (2 * s + 2) * LANES)
            o_ref[rows, c0] = base_ref[rows, c0] + even
            o_ref[rows, c1] = base_ref[rows, c1] + odd


def _combine(dest, gate_rows, x, t_shared, w_sh_down, ys, pitch, tm=128):
    n, d = x.shape
    ff = t_shared.shape[1]
    tiles = n // tm
    dest_t = dest.reshape(TOP_K, tiles, tm).transpose(1, 0, 2).reshape(tiles, 1, TOP_K * tm)
    dspec = lambda imap: pl.BlockSpec((1, 1, TOP_K * tm), imap, memory_space=pltpu.SMEM)
    bpitch = pitch if (pitch // SUBLANES) % 2 else pitch + SUBLANES
    return pl.pallas_call(
        functools.partial(_combine_kernel, tm=tm, pitch=pitch),
        grid=(tiles,),
        in_specs=[dspec(lambda i: (0, 0, 0)),
                  dspec(lambda i: (jnp.minimum(i + 1, tiles - 1), 0, 0)),
                  pl.BlockSpec((2 * tm, TOP_K), lambda i: (i, 0)),
                  pl.BlockSpec((tm, d), lambda i: (i, 0)),
                  pl.BlockSpec((tm, ff), lambda i: (i, 0)),
                  pl.BlockSpec((ff, d), lambda i: (0, 0)),
                  pl.BlockSpec(memory_space=pl.ANY)],
        out_specs=pl.BlockSpec((tm, d), lambda i: (i, 0)),
        out_shape=jax.ShapeDtypeStruct((n, d), F32),
        scratch_shapes=[pltpu.VMEM((2 * TOP_K * tm * bpitch, LANES), U32),
                        pltpu.VMEM((2 * tm * pitch, LANES), F32),
                        pltpu.VMEM((tm, d), F32),
                        pltpu.SemaphoreType.DMA((2,))],
        compiler_params=_cp(("arbitrary",)),
        name="moe_combine",
    )(dest_t, dest_t, gate_rows, x, t_shared, w_sh_down.astype(BF16), ys)


def _rope_tables(positions):
    half = ROPE_DIM // 2
    inv_freq = ROPE_THETA ** (-jnp.arange(0, ROPE_DIM, 2, dtype=F32) / ROPE_DIM)
    ang = positions.reshape(-1).astype(F32)[:, None] * inv_freq
    cos, sin = jnp.cos(ang), jnp.sin(ang)
    n = ang.shape[0]
    z = lambda w: jnp.zeros((n, w), F32)
    cos_t = jnp.concatenate([cos, cos, z(LANES - ROPE_DIM)], axis=1)
    sin_a = jnp.concatenate([-sin, z(LANES - half)], axis=1)
    sin_b = jnp.concatenate([z(half), sin, z(LANES - ROPE_DIM)], axis=1)
    return cos_t, sin_a, sin_b


def _pad_lanes(v, width):
    return jnp.concatenate([v, jnp.zeros((width - v.shape[0],), v.dtype)])


def kernel(x, mem, positions, g_mix, w_in, g_qa, g_ka, rel_bias, g_cq, w_uq, g_ckv, w_ukv, g_qb, g_kb, w_o, g_cross, g_mem, w_xq, w_xkv, g_qx, g_kx, w_xo, g_ffn, w_router, router_bias, w_sh_gate, w_sh_up, w_sh_down, w_ex_gate, w_ex_up, w_ex_down):
    batch, seq, d = x.shape
    n = batch * seq
    mem_tokens = mem.shape[1]
    width_a = HEADS_A * HEAD_DIM
    x2d = x.reshape(n, d)

    h = _norm_bf16(x2d, g_mix)
    gs_qk = jnp.concatenate([jnp.tile(g_qa * HEAD_DIM ** -0.5, HEADS_A), jnp.tile(g_ka, HEADS_A)])
    w_in_t = w_in.T
    qk, v_a, c = _in_proj_all(h, w_in_t, gs_qk, 2 * width_a, width_a, Q_LORA + KV_LORA)
    w_kpe = jnp.pad(w_in_t[3 * width_a + Q_LORA + KV_LORA:], ((0, LANES - ROPE_DIM), (0, 0)))
    kpe = _proj_t(h, w_kpe, "in_proj_kpe")

    o_a = _attn_a(qk, v_a, _band_bias(rel_bias), batch, seq)

    cos_t, sin_a, sin_b = _rope_tables(positions)
    w_uq_pad = jnp.pad(w_uq.reshape(Q_LORA, HEADS_B, QK_B),
                       ((0, 0), (0, 0), (0, 2 * LANES - QK_B))).reshape(Q_LORA, HEADS_B * 2 * LANES)
    gq = _pad_lanes(g_qb, 2 * LANES).reshape(1, 2 * LANES)
    gk = _pad_lanes(g_kb, 2 * LANES).reshape(1, 2 * LANES)
    qf = _mla_q(c, g_cq, w_uq_pad, gq, cos_t, sin_a, sin_b)
    kf, v_b = _mla_kv(c, g_ckv, kpe, w_ukv, gk, cos_t, sin_a, sin_b)
    o_b = _attn_b(qf, kf, v_b, batch, seq)

    x1 = _out_proj(o_a, o_b, w_o, x2d)

    hw = MEM_HEADS * MEM_HEAD_DIM
    kvx = _normed_proj(mem.reshape(batch * mem_tokens, d), g_mem, w_xkv,
                       jnp.concatenate([jnp.tile(g_kx, MEM_HEADS), jnp.ones((hw,), F32)]), hw, "cross_kv")
    x2 = _cross_attn(x1, g_cross, w_xq, jnp.tile(g_qx * MEM_HEAD_DIM ** -0.5, MEM_HEADS), kvx, w_xo,
                     batch, seq, mem_tokens)

    hp, eid, gate, rank, cnt, t_shared = _router(x2, g_ffn, w_router, router_bias, w_sh_gate, w_sh_up)
    counts = cnt[:, 0]
    nstep = (counts + STEP_ROWS - 1) // STEP_ROWS
    step_end = jnp.cumsum(nstep).astype(I32)
    step_start = step_end - nstep
    n_steps = n * TOP_K // STEP_ROWS + N_EXPERTS
    experts = jnp.arange(N_EXPERTS, dtype=I32)
    start_of = jnp.sum(jnp.where(eid[:, :, None] == experts, step_start, 0), axis=-1)
    dest = start_of * STEP_ROWS + rank
    steps = jnp.arange(n_steps, dtype=I32)
    step_e = jnp.minimum(jnp.sum((step_end[None, :] <= steps[:, None]).astype(I32), axis=1), N_EXPERTS - 1)
    mine = step_e[:, None] == experts[None, :]
    rows_left = jnp.sum(jnp.where(mine, counts[None, :] - STEP_ROWS * (steps[:, None] - step_start[None, :]), 0),
                        axis=1)
    used = steps < step_end[-1]
    rows_here = jnp.where(used, jnp.clip(rows_left, 0, STEP_ROWS), 0)
    nsub = (rows_here + EXPERT_ROWS - 1) // EXPERT_ROWS
    nonempty = counts > 0
    ring_slot = (jnp.cumsum(nonempty.astype(I32)) - 1) % 2
    later = (experts[None, :] > experts[:, None]) & nonempty[None, :]
    succ = jnp.min(jnp.where(later, experts[None, :], N_EXPERTS), axis=1)
    succ = jnp.where(succ == N_EXPERTS, -1, succ)
    per_step = lambda v: jnp.sum(jnp.where(mine, v[None, :], 0), axis=1)
    first = (used & (steps == per_step(step_start))).astype(I32)
    sched = jnp.stack([step_e, nsub, first, per_step(ring_slot), per_step(succ)]).astype(I32)
    n_used = step_end[-1:]
    pitch = d // 2 // LANES

    xs = _dispatch(hp, dest, counts, step_start, n_steps * STEP_ROWS, pitch)
    hs = _expert_up(xs, w_ex_gate, w_ex_up, sched, n_used, n_steps, pitch)
    ys = _expert_down(hs, w_ex_down, sched, n_used, n_steps, pitch)

    out = _combine(dest, jnp.repeat(gate.T, 2, axis=0), x2, t_shared, w_sh_down, ys, pitch)
    return out.reshape(batch, seq, d)
```

```python
import functools

import jax
import jax.numpy as jnp
import numpy as np
from jax import lax
from jax.experimental import pallas as pl
from jax.experimental.pallas import tpu as pltpu

F32 = jnp.float32
BF16 = jnp.bfloat16
I32 = jnp.int32
U32 = jnp.uint32

CHUNK = 64
LEFT_CHUNKS = 8
REL_CLIP = 128
HEAD_DIM = 128
HEADS_A = 16
HEADS_B = 16
Q_LORA = 1024
KV_LORA = 512
NOPE_DIM = 128
ROPE_DIM = 64
V_DIM = 128
QK_B = NOPE_DIM + ROPE_DIM
ROPE_THETA = 10000.0
MEM_HEADS = 4
MEM_HEAD_DIM = 128
N_EXPERTS = 64
N_GROUPS = 8
TOPK_GROUPS = 4
TOP_K = 8
ROUTED_SCALE = 2.5
EPS = 1e-6

LANES = 128
SUBLANES = 8
ATT_BLOCK = 256
EXPERT_ROWS = 256
STEP_ROWS = 2 * EXPERT_ROWS
DMA_UNROLL = 8
COMBINE_GROUP = 16
NEG = -1e30
VMEM_LIMIT = 56 * 1024 * 1024

_NT = (((1,), (1,)), ((), ()))


def _cp(sem, vmem=VMEM_LIMIT):
    return pltpu.CompilerParams(dimension_semantics=sem, vmem_limit_bytes=vmem)


def _rms(x, g):
    return x * lax.rsqrt(jnp.mean(x * x, axis=-1, keepdims=True) + EPS) * g


def _pack_store(ref, first_token, x, scr):
    m, w = x.shape
    pitch = w // (2 * LANES)
    regions = scr.shape[0] // (2 * m)
    for s in range(pitch):
        base = (s % regions) * 2 * m
        scr[pl.ds(base, m, stride=2), :] = x[:, (2 * s) * LANES:(2 * s + 1) * LANES]
        scr[pl.ds(base + 1, m, stride=2), :] = x[:, (2 * s + 1) * LANES:(2 * s + 2) * LANES]
        z = scr[pl.ds(base, 2 * m), :].astype(BF16)
        ref[pl.ds(first_token * pitch + s, m, stride=pitch), :] = pltpu.bitcast(z, U32)


def _load_pairs(ref, first_token, m, pitch, s):
    w = ref[pl.ds(first_token * pitch + s, m, stride=pitch), :]
    return pltpu.bitcast(w, BF16).astype(F32)


def _split_pairs(z, scr, region):
    m = z.shape[0] // 2
    base = region * 2 * m
    scr[pl.ds(base, 2 * m), :] = z
    return scr[pl.ds(base, m, stride=2), :], scr[pl.ds(base + 1, m, stride=2), :]


def _load_unpacked(ref, first_token, m, pitch, scr, dtype=F32):
    regions = scr.shape[0] // (2 * m)
    cols = []
    for s in range(pitch):
        for piece in _split_pairs(_load_pairs(ref, first_token, m, pitch, s), scr, s % regions):
            cols.append(piece.astype(dtype))
    return jnp.concatenate(cols, axis=1)


def _norm_kernel(x_ref, g_ref, o_ref):
    o_ref[...] = _rms(x_ref[...], g_ref[...]).astype(o_ref.dtype)


def _norm_bf16(x, g, tm=256):
    n, d = x.shape
    return pl.pallas_call(
        _norm_kernel,
        grid=(n // tm,),
        in_specs=[pl.BlockSpec((tm, d), lambda i: (i, 0)), pl.BlockSpec((1, d), lambda i: (0, 0))],
        out_specs=pl.BlockSpec((tm, d), lambda i: (i, 0)),
        out_shape=jax.ShapeDtypeStruct((n, d), BF16),
        compiler_params=_cp(("parallel",)),
        name="norm_mix",
    )(x, g.reshape(1, d))


def _proj_t_kernel(h_ref, w_ref, o_ref):
    o_ref[...] = lax.dot_general(h_ref[...], w_ref[...].astype(BF16), _NT, preferred_element_type=F32)


def _proj_t(h, w_t, name, tm=1024):
    n, d = h.shape
    nc = w_t.shape[0]
    return pl.pallas_call(
        _proj_t_kernel,
        grid=(n // tm,),
        in_specs=[pl.BlockSpec((tm, d), lambda i: (i, 0)), pl.BlockSpec((nc, d), lambda i: (0, 0))],
        out_specs=pl.BlockSpec((tm, nc), lambda i: (i, 0)),
        out_shape=jax.ShapeDtypeStruct((n, nc), F32),
        compiler_params=_cp(("parallel",)),
        name=name,
    )(h, w_t)


def _in_proj_all_kernel(h_ref, w_ref, gs_ref, qk_ref, v_ref, c_ref, *, tn, n_qk, n_v):
    j = pl.program_id(1)
    acc = lax.dot_general(h_ref[...], w_ref[...].astype(BF16), _NT, preferred_element_type=F32)

    @pl.when(j < n_qk)
    def _():
        for c in range(tn // LANES):
            sl = slice(c * LANES, (c + 1) * LANES)
            qk_ref[:, sl] = _rms(acc[:, sl], gs_ref[:, sl]).astype(qk_ref.dtype)

    @pl.when((j >= n_qk) & (j < n_qk + n_v))
    def _():
        v_ref[...] = acc.astype(v_ref.dtype)

    @pl.when(j >= n_qk + n_v)
    def _():
        c_ref[...] = acc


def _in_proj_all(h, w_t, gs_qk, w_qk, w_v, w_c, tm=1024, tn=512):
    n, d = h.shape
    n_qk, n_v, n_c = w_qk // tn, w_v // tn, w_c // tn
    return pl.pallas_call(
        functools.partial(_in_proj_all_kernel, tn=tn, n_qk=n_qk, n_v=n_v),
        grid=(n // tm, n_qk + n_v + n_c),
        in_specs=[pl.BlockSpec((tm, d), lambda i, j: (i, 0)),
                  pl.BlockSpec((tn, d), lambda i, j: (j, 0)),
                  pl.BlockSpec((1, tn), lambda i, j: (0, jnp.minimum(j, n_qk - 1)))],
        out_specs=[pl.BlockSpec((tm, tn), lambda i, j: (i, jnp.minimum(j, n_qk - 1))),
                   pl.BlockSpec((tm, tn), lambda i, j: (i, jnp.clip(j - n_qk, 0, n_v - 1))),
                   pl.BlockSpec((tm, tn), lambda i, j: (i, jnp.clip(j - n_qk - n_v, 0, n_c - 1)))],
        out_shape=[jax.ShapeDtypeStruct((n, w_qk), BF16), jax.ShapeDtypeStruct((n, w_v), BF16),
                   jax.ShapeDtypeStruct((n, w_c), F32)],
        compiler_params=_cp(("parallel", "arbitrary")),
        name="in_proj",
    )(h, w_t, gs_qk.reshape(1, w_qk))


def _attn_a_kernel(q_ref, k0_ref, k1_ref, k2_ref, v0_ref, v1_ref, v2_ref, b_ref, o_ref, *, heads):
    i = pl.program_id(2)
    k_refs = (k0_ref, k1_ref, k2_ref)
    v_refs = (v0_ref, v1_ref, v2_ref)
    for hh in range(heads):
        sl = slice(hh * HEAD_DIM, (hh + 1) * HEAD_DIM)
        q = q_ref[:, sl]
        s = []
        for d in range(3):
            sd = lax.dot_general(q, k_refs[d][:, sl], _NT, preferred_element_type=F32)
            sd = sd + b_ref[hh, :, d * ATT_BLOCK:(d + 1) * ATT_BLOCK]
            if d > 0:
                sd = jnp.where(i >= d, sd, NEG)
            s.append(sd)
        m = jnp.maximum(jnp.maximum(s[0].max(-1, keepdims=True), s[1].max(-1, keepdims=True)),
                        s[2].max(-1, keepdims=True))
        l = jnp.zeros_like(m)
        o = jnp.zeros((q.shape[0], HEAD_DIM), F32)
        for d in range(3):
            p = jnp.exp(s[d] - m)
            l = l + p.sum(-1, keepdims=True)
            o = o + jnp.dot(p.astype(BF16), v_refs[d][:, sl], preferred_element_type=F32)
        o_ref[:, sl] = (o / l).astype(o_ref.dtype)


def _band_bias(rel_bias):
    blk = ATT_BLOCK
    period = 2 * blk + 1
    heads = rel_bias.shape[0]
    r = np.arange(blk)[:, None]
    c = np.arange(blk)[None, :]
    per_blk = blk // CHUNK
    k = np.arange(period)
    delta = np.where(k <= blk, k, k - period)
    tiles = []
    for d in range(3):
        idx = np.clip(blk * d - delta, -(CHUNK - 1), REL_CLIP) + CHUNK - 1
        w = rel_bias[:, idx].astype(F32)
        b = jnp.tile(w, (1, blk))[:, :blk * (period - 1)].reshape(heads, blk, period - 1)[:, :, :blk]
        cdiff = per_blk * d + r // CHUNK - c // CHUNK
        valid = (cdiff >= 0) & (cdiff <= LEFT_CHUNKS)
        tiles.append(jnp.where(valid[None], b, NEG))
    return jnp.concatenate(tiles, axis=-1)


def _attn_a(qk, v, bias, batch, seq, heads_per_step=8):
    n = qk.shape[0]
    nq = seq // ATT_BLOCK
    hw = heads_per_step * HEAD_DIM
    groups = HEADS_A // heads_per_step
    kcol0 = HEADS_A * HEAD_DIM // hw

    def kspec(d, col0):
        return pl.BlockSpec((ATT_BLOCK, hw), lambda g, b, i: (b * nq + jnp.maximum(i - d, 0), col0 + g))

    return pl.pallas_call(
        functools.partial(_attn_a_kernel, heads=heads_per_step),
        grid=(groups, batch, nq),
        in_specs=[pl.BlockSpec((ATT_BLOCK, hw), lambda g, b, i: (b * nq + i, g)),
                  kspec(0, kcol0), kspec(1, kcol0), kspec(2, kcol0),
                  kspec(0, 0), kspec(1, 0), kspec(2, 0),
                  pl.BlockSpec((heads_per_step, ATT_BLOCK, 3 * ATT_BLOCK), lambda g, b, i: (g, 0, 0))],
        out_specs=pl.BlockSpec((ATT_BLOCK, hw), lambda g, b, i: (b * nq + i, g)),
        out_shape=jax.ShapeDtypeStruct((n, HEADS_A * HEAD_DIM), BF16),
        compiler_params=_cp(("parallel", "parallel", "arbitrary")),
        name="attn_band",
    )(qk, qk, qk, qk, v, v, v, bias)


def _rope_pe(pe, g, cos_ref, sa_ref, sb_ref):
    ss = jnp.sum(pe * pe, axis=-1, keepdims=True) * (1.0 / ROPE_DIM)
    pn = pe * lax.rsqrt(ss + EPS) * g
    half = ROPE_DIM // 2
    return (pn * cos_ref[...] + pltpu.roll(pn, LANES - half, 1) * sa_ref[...]
            + pltpu.roll(pn, half, 1) * sb_ref[...])


def _mla_q_kernel(c_ref, gc_ref, w_ref, gq_ref, cos_ref, sa_ref, sb_ref, o_ref, xn_ref, *, heads, scale):
    @pl.when(pl.program_id(1) == 0)
    def _():
        xn_ref[...] = _rms(c_ref[...], gc_ref[...]).astype(BF16)

    acc = jnp.dot(xn_ref[...], w_ref[...].astype(BF16), preferred_element_type=F32)
    for hh in range(heads):
        base = hh * 2 * LANES
        nope = acc[:, base:base + LANES]
        pe = acc[:, base + LANES:base + 2 * LANES]
        o_ref[:, base:base + LANES] = (_rms(nope, gq_ref[:, :LANES]) * scale).astype(o_ref.dtype)
        o_ref[:, base + LANES:base + 2 * LANES] = (
            _rope_pe(pe, gq_ref[:, LANES:], cos_ref, sa_ref, sb_ref) * scale).astype(o_ref.dtype)


def _mla_q(c, g_cq, w_uq_pad, gq, cos_t, sin_a, sin_b, tm=512, heads_per_step=8):
    n = c.shape[0]
    tn = heads_per_step * 2 * LANES
    ncols = w_uq_pad.shape[1]
    row = lambda i, j: (i, 0)
    return pl.pallas_call(
        functools.partial(_mla_q_kernel, heads=heads_per_step, scale=QK_B ** -0.5),
        grid=(n // tm, ncols // tn),
        in_specs=[pl.BlockSpec((tm, Q_LORA), row),
                  pl.BlockSpec((1, Q_LORA), lambda i, j: (0, 0)),
                  pl.BlockSpec((Q_LORA, tn), lambda i, j: (0, j)),
                  pl.BlockSpec((1, 2 * LANES), lambda i, j: (0, 0)),
                  pl.BlockSpec((tm, LANES), row), pl.BlockSpec((tm, LANES), row), pl.BlockSpec((tm, LANES), row)],
        out_specs=pl.BlockSpec((tm, tn), lambda i, j: (i, j)),
        out_shape=jax.ShapeDtypeStruct((n, ncols), BF16),
        scratch_shapes=[pltpu.VMEM((tm, Q_LORA), BF16)],
        compiler_params=_cp(("parallel", "arbitrary")),
        name="mla_q_proj",
    )(c, g_cq.reshape(1, Q_LORA), w_uq_pad, gq, cos_t, sin_a, sin_b)


def _mla_kv_kernel(c_ref, gc_ref, kpe_ref, w_ref, gk_ref, cos_ref, sa_ref, sb_ref, k_ref, v_ref,
                   xn_ref, pe_ref, *, heads):
    @pl.when(pl.program_id(1) == 0)
    def _():
        xn_ref[...] = _rms(c_ref[...], gc_ref[...]).astype(BF16)
        pe_ref[...] = _rope_pe(kpe_ref[...], gk_ref[:, LANES:], cos_ref, sa_ref, sb_ref).astype(BF16)

    acc = jnp.dot(xn_ref[...], w_ref[...].astype(BF16), preferred_element_type=F32)
    for hh in range(heads):
        base = hh * 2 * LANES
        k_ref[:, base:base + LANES] = _rms(acc[:, base:base + LANES], gk_ref[:, :LANES]).astype(k_ref.dtype)
        k_ref[:, base + LANES:base + 2 * LANES] = pe_ref[...]
        v_ref[:, hh * LANES:(hh + 1) * LANES] = acc[:, base + LANES:base + 2 * LANES].astype(v_ref.dtype)


def _mla_kv(c, g_ckv, kpe, w_ukv, gk, cos_t, sin_a, sin_b, tm=512, heads_per_step=8):
    n = c.shape[0]
    tn = heads_per_step * 2 * LANES
    ncols = w_ukv.shape[1]
    cblk = Q_LORA // KV_LORA
    row = lambda i, j: (i, 0)
    return pl.pallas_call(
        functools.partial(_mla_kv_kernel, heads=heads_per_step),
        grid=(n // tm, ncols // tn),
        in_specs=[pl.BlockSpec((tm, KV_LORA), lambda i, j: (i, cblk)),
                  pl.BlockSpec((1, KV_LORA), lambda i, j: (0, 0)),
                  pl.BlockSpec((tm, LANES), row),
                  pl.BlockSpec((KV_LORA, tn), lambda i, j: (0, j)),
                  pl.BlockSpec((1, 2 * LANES), lambda i, j: (0, 0)),
                  pl.BlockSpec((tm, LANES), row), pl.BlockSpec((tm, LANES), row), pl.BlockSpec((tm, LANES), row)],
        out_specs=[pl.BlockSpec((tm, tn), lambda i, j: (i, j)),
                   pl.BlockSpec((tm, tn // 2), lambda i, j: (i, j))],
        out_shape=[jax.ShapeDtypeStruct((n, ncols), BF16),
                   jax.ShapeDtypeStruct((n, ncols // 2), BF16)],
        scratch_shapes=[pltpu.VMEM((tm, KV_LORA), BF16), pltpu.VMEM((tm, LANES), BF16)],
        compiler_params=_cp(("parallel", "arbitrary")),
        name="mla_kv_proj",
    )(c, g_ckv.reshape(1, KV_LORA), kpe, w_ukv, gk, cos_t, sin_a, sin_b)


def _attn_b_kernel(q_ref, k_ref, v_ref, o_ref, *, heads):
    i = pl.program_id(2)
    tq = q_ref.shape[0]
    qw = 2 * LANES
    qs = [q_ref[:, h * qw:(h + 1) * qw] for h in range(heads)]

    def step(first_blk, width, carry, masked):
        start = pl.multiple_of(first_blk * ATT_BLOCK, ATT_BLOCK)
        if masked:
            per_blk = ATT_BLOCK // CHUNK
            r = i * per_blk + lax.broadcasted_iota(I32, (tq, width), 0) // CHUNK
            c = first_blk * per_blk + lax.broadcasted_iota(I32, (tq, width), 1) // CHUNK
            keep = c <= r
        out = []
        for h in range(heads):
            m, l, acc = carry[h]
            s = lax.dot_general(qs[h], k_ref[pl.ds(start, width), h * qw:(h + 1) * qw], _NT,
                                preferred_element_type=F32)
            if masked:
                s = jnp.where(keep, s, NEG)
            m_new = jnp.maximum(m, s.max(-1, keepdims=True))
            alpha = jnp.exp(m - m_new)
            p = jnp.exp(s - m_new)
            l = alpha * l + p.sum(-1, keepdims=True)
            acc = alpha * acc + jnp.dot(p.astype(BF16), v_ref[pl.ds(start, width), h * V_DIM:(h + 1) * V_DIM],
                                        preferred_element_type=F32)
            out.append((m_new, l, acc))
        return tuple(out)

    def finish(final):
        for h in range(heads):
            _, l, acc = final[h]
            o_ref[:, h * V_DIM:(h + 1) * V_DIM] = (acc / l).astype(o_ref.dtype)

    init = tuple((jnp.full((tq, 1), NEG, F32), jnp.zeros((tq, 1), F32), jnp.zeros((tq, V_DIM), F32))
                 for _ in range(heads))
    carry = lax.fori_loop(0, i // 2, lambda j, c: step(2 * j, 2 * ATT_BLOCK, c, False), init)

    @pl.when(i % 2 == 1)
    def _():
        finish(step(i - 1, 2 * ATT_BLOCK, carry, True))

    @pl.when(i % 2 == 0)
    def _():
        finish(step(i, ATT_BLOCK, carry, True))


def _attn_b(qf, kf, vb, batch, seq, heads_per_step=8):
    n = qf.shape[0]
    nq = seq // ATT_BLOCK
    qw = heads_per_step * 2 * LANES
    vw = heads_per_step * V_DIM
    return pl.pallas_call(
        functools.partial(_attn_b_kernel, heads=heads_per_step),
        grid=(batch, HEADS_B // heads_per_step, nq),
        in_specs=[pl.BlockSpec((ATT_BLOCK, qw), lambda b, g, i: (b * nq + i, g)),
                  pl.BlockSpec((seq, qw), lambda b, g, i: (b, g)),
                  pl.BlockSpec((seq, vw), lambda b, g, i: (b, g))],
        out_specs=pl.BlockSpec((ATT_BLOCK, vw), lambda b, g, i: (b * nq + i, g)),
        out_shape=jax.ShapeDtypeStruct((n, HEADS_B * V_DIM), BF16),
        compiler_params=_cp(("parallel", "parallel", "arbitrary")),
        name="attn_latent",
    )(qf, kf, vb)


def _out_proj_kernel(oa_ref, ob_ref, w_ref, x_ref, o_ref, wb_ref):
    @pl.when(pl.program_id(1) == 0)
    def _():
        wb_ref[...] = w_ref[...].astype(BF16)

    ka = oa_ref.shape[1]
    acc = jnp.dot(oa_ref[...], wb_ref[:ka, :], preferred_element_type=F32)
    acc = acc + jnp.dot(ob_ref[...], wb_ref[ka:, :], preferred_element_type=F32)
    o_ref[...] = x_ref[...] + acc


def _out_proj(oa, ob, w_o, x, tm=256, tn=1024):
    n, ka = oa.shape
    kb = ob.shape[1]
    d = w_o.shape[1]
    return pl.pallas_call(
        _out_proj_kernel,
        grid=(d // tn, n // tm),
        in_specs=[pl.BlockSpec((tm, ka), lambda j, i: (i, 0)),
                  pl.BlockSpec((tm, kb), lambda j, i: (i, 0)),
                  pl.BlockSpec((ka + kb, tn), lambda j, i: (0, j)),
                  pl.BlockSpec((tm, tn), lambda j, i: (i, j))],
        out_specs=pl.BlockSpec((tm, tn), lambda j, i: (i, j)),
        out_shape=jax.ShapeDtypeStruct((n, d), F32),
        scratch_shapes=[pltpu.VMEM((ka + kb, tn), BF16)],
        compiler_params=_cp(("parallel", "arbitrary")),
        name="out_proj",
    )(oa, ob, w_o, x)


def _normed_proj_kernel(x_ref, g_ref, w_ref, gs_ref, o_ref, wb_ref, *, norm_cols):
    @pl.when(pl.program_id(0) == 0)
    def _():
        wb_ref[...] = w_ref[...].astype(BF16)

    h = _rms(x_ref[...], g_ref[...]).astype(BF16)
    acc = jnp.dot(h, wb_ref[...], preferred_element_type=F32)
    for c in range(acc.shape[1] // LANES):
        sl = slice(c * LANES, (c + 1) * LANES)
        if c * LANES < norm_cols:
            o_ref[:, sl] = _rms(acc[:, sl], gs_ref[:, sl]).astype(o_ref.dtype)
        else:
            o_ref[:, sl] = acc[:, sl].astype(o_ref.dtype)


def _normed_proj(x, g, w, gs, norm_cols, name, tm=256):
    n, d = x.shape
    nc = w.shape[1]
    return pl.pallas_call(
        functools.partial(_normed_proj_kernel, norm_cols=norm_cols),
        grid=(n // tm,),
        in_specs=[pl.BlockSpec((tm, d), lambda i: (i, 0)),
                  pl.BlockSpec((1, d), lambda i: (0, 0)),
                  pl.BlockSpec((d, nc), lambda i: (0, 0)),
                  pl.BlockSpec((1, nc), lambda i: (0, 0))],
        out_specs=pl.BlockSpec((tm, nc), lambda i: (i, 0)),
        out_shape=jax.ShapeDtypeStruct((n, nc), BF16),
        scratch_shapes=[pltpu.VMEM((d, nc), BF16)],
        compiler_params=_cp(("arbitrary",)),
        name=name,
    )(x, g.reshape(1, d), w, gs.reshape(1, nc))


def _cross_kernel(x_ref, g_ref, wq_ref, gq_ref, k_ref, v_ref, wo_ref, o_ref):
    x = x_ref[...]
    q = jnp.dot(_rms(x, g_ref[...]).astype(BF16), wq_ref[...], preferred_element_type=F32)
    outs = []
    for hh in range(MEM_HEADS):
        sl = slice(hh * MEM_HEAD_DIM, (hh + 1) * MEM_HEAD_DIM)
        qh = _rms(q[:, sl], gq_ref[:, sl]).astype(BF16)
        s = lax.dot_general(qh, k_ref[:, sl], _NT, preferred_element_type=F32)
        p = jnp.exp(s - s.max(-1, keepdims=True))
        o = jnp.dot(p.astype(BF16), v_ref[:, sl], preferred_element_type=F32)
        outs.append((o / p.sum(-1, keepdims=True)).astype(BF16))
    acc = x
    for hh in range(MEM_HEADS):
        sl = slice(hh * MEM_HEAD_DIM, (hh + 1) * MEM_HEAD_DIM)
        acc = acc + jnp.dot(outs[hh], wo_ref[sl, :], preferred_element_type=F32)
    o_ref[...] = acc


def _cross_attn(x, g, w_xq, gq, kvx, w_xo, batch, seq, mem_tokens, tm=256):
    n, d = x.shape
    per_b = seq // tm
    hw = MEM_HEADS * MEM_HEAD_DIM
    const = lambda b, i: (0, 0)
    return pl.pallas_call(
        _cross_kernel,
        grid=(batch, per_b),
        in_specs=[pl.BlockSpec((tm, d), lambda b, i: (b * per_b + i, 0)),
                  pl.BlockSpec((1, d), const),
                  pl.BlockSpec((d, hw), const),
                  pl.BlockSpec((1, hw), const),
                  pl.BlockSpec((mem_tokens, hw), lambda b, i: (b, 0)),
                  pl.BlockSpec((mem_tokens, hw), lambda b, i: (b, 1)),
                  pl.BlockSpec((hw, d), const)],
        out_specs=pl.BlockSpec((tm, d), lambda b, i: (b * per_b + i, 0)),
        out_shape=jax.ShapeDtypeStruct((n, d), F32),
        compiler_params=_cp(("parallel", "arbitrary")),
        name="cross_attn",
    )(x, g.reshape(1, d), w_xq.astype(BF16), gq.reshape(1, hw), kvx, kvx, w_xo.astype(BF16))


def _router_kernel(x_ref, g_ref, wr_ref, b_ref, tri_ref, wg_ref, wu_ref, hp_ref, eid_ref, gate_ref, rank_ref,
                   cnt_ref, ts_ref, run_ref, scr_ref):
    t = pl.program_id(0)

    @pl.when(t == 0)
    def _():
        run_ref[...] = jnp.zeros_like(run_ref)

    h = _rms(x_ref[...], g_ref[...])
    _pack_store(hp_ref, 0, h, scr_ref)
    hb = h.astype(BF16)
    ts_ref[...] = _silu_mul(jnp.dot(hb, wg_ref[...], preferred_element_type=F32),
                            jnp.dot(hb, wu_ref[...], preferred_element_type=F32)).astype(ts_ref.dtype)
    tm = h.shape[0]
    per_g = N_EXPERTS // N_GROUPS

    logits = lax.dot_general(wr_ref[...], h, _NT, precision=lax.Precision.HIGHEST,
                             preferred_element_type=F32)
    scores = 1.0 / (1.0 + jnp.exp(-logits))
    choice = scores + b_ref[...]

    sub = lax.broadcasted_iota(I32, (per_g, tm), 0).astype(F32)
    rows = []
    for g in range(N_GROUPS):
        c = choice[g * per_g:(g + 1) * per_g, :]
        m1 = c.max(0, keepdims=True)
        first = jnp.where(c == m1, sub, float(per_g)).min(0, keepdims=True)
        m2 = jnp.where(sub == first, -jnp.inf, c).max(0, keepdims=True)
        rows.append(m1 + m2)
    gs = jnp.concatenate(rows, axis=0)

    gsub = lax.broadcasted_iota(I32, (N_GROUPS, tm), 0).astype(F32)
    beaten = jnp.zeros((N_GROUPS, tm), F32)
    for g2 in range(N_GROUPS):
        row = gs[g2:g2 + 1, :]
        wins = (row > gs) | ((row == gs) & (gsub > float(g2)))
        beaten = beaten + jnp.where(wins, 1.0, 0.0)
    g_ok = jnp.where(beaten < TOPK_GROUPS, 1.0, 0.0)
    e_ok = jnp.concatenate(
        [jnp.broadcast_to(g_ok[g:g + 1, :], (per_g, tm)) for g in range(N_GROUPS)], axis=0)

    eiota = lax.broadcasted_iota(I32, (N_EXPERTS, tm), 0).astype(F32)
    masked = jnp.where(e_ok > 0.5, choice, -jnp.inf)
    chosen = jnp.zeros((N_EXPERTS, tm), F32)
    eids, ws = [], []
    for _ in range(TOP_K):
        m = masked.max(0, keepdims=True)
        idx = jnp.where(masked == m, eiota, float(N_EXPERTS)).min(0, keepdims=True)
        sel = eiota == idx
        ws.append(jnp.where(sel, scores, 0.0).sum(0, keepdims=True))
        masked = jnp.where(sel, -jnp.inf, masked)
        chosen = jnp.where(sel, 1.0, chosen)
        eids.append(idx)
    wsum = ws[0]
    for w in ws[1:]:
        wsum = wsum + w
    denom = wsum + 1e-20

    pos = jnp.dot(chosen.astype(BF16), tri_ref[...], preferred_element_type=F32) + run_ref[:, 0:1]
    run_ref[...] = run_ref[...] + chosen.sum(1, keepdims=True)
    cnt_ref[...] = run_ref[...].astype(I32)

    for r in range(TOP_K):
        eid_ref[r:r + 1, :] = eids[r].astype(I32)
        gate_ref[r:r + 1, :] = ws[r] / denom * ROUTED_SCALE
        rank_ref[r:r + 1, :] = jnp.where(eiota == eids[r], pos, 0.0).sum(0, keepdims=True).astype(I32)


def _router(x, g, w_router, router_bias, w_sh_gate, w_sh_up, tm=256):
    n, d = x.shape
    ne = w_router.shape[1]
    ff = w_sh_gate.shape[1]
    tri = (jnp.arange(tm)[:, None] < jnp.arange(tm)[None, :]).astype(BF16)
    row8 = lambda i: (0, i)
    pitch = d // 2 // LANES
    return pl.pallas_call(
        _router_kernel,
        grid=(n // tm,),
        in_specs=[pl.BlockSpec((tm, d), lambda i: (i, 0)),
                  pl.BlockSpec((1, d), lambda i: (0, 0)),
                  pl.BlockSpec((ne, d), lambda i: (0, 0)),
                  pl.BlockSpec((ne, 1), lambda i: (0, 0)),
                  pl.BlockSpec((tm, tm), lambda i: (0, 0)),
                  pl.BlockSpec((d, ff), lambda i: (0, 0)), pl.BlockSpec((d, ff), lambda i: (0, 0))],
        out_specs=[pl.BlockSpec((tm * pitch, LANES), lambda i: (i, 0)),
                   pl.BlockSpec((TOP_K, tm), row8), pl.BlockSpec((TOP_K, tm), row8),
                   pl.BlockSpec((TOP_K, tm), row8),
                   pl.BlockSpec((ne, LANES), lambda i: (0, 0)),
                   pl.BlockSpec((tm, ff), lambda i: (i, 0))],
        out_shape=[jax.ShapeDtypeStruct((n * pitch, LANES), U32),
                   jax.ShapeDtypeStruct((TOP_K, n), I32), jax.ShapeDtypeStruct((TOP_K, n), F32),
                   jax.ShapeDtypeStruct((TOP_K, n), I32),
                   jax.ShapeDtypeStruct((ne, LANES), I32),
                   jax.ShapeDtypeStruct((n, ff), BF16)],
        scratch_shapes=[pltpu.VMEM((ne, LANES), F32), pltpu.VMEM((2 * tm * pitch, LANES), F32)],
        compiler_params=_cp(("arbitrary",)),
        name="router",
    )(x, g.reshape(1, d), w_router.T, router_bias.reshape(ne, 1), tri,
      w_sh_gate.astype(BF16), w_sh_up.astype(BF16))


def _dispatch_kernel(cnt_ref, start_ref, dest_ref, hp_ref, xs_ref, sem, *, tm, pitch):
    t = pl.program_id(0)
    pairs = TOP_K * tm

    def row_copy(r, slot):
        src = hp_ref.at[pl.ds(pl.multiple_of(r * pitch, pitch), pitch), :]
        dst = xs_ref.at[pl.ds(pl.multiple_of(slot * pitch, pitch), pitch), :]
        return pltpu.make_async_copy(src, dst, sem)

    def issue(g, c):
        for u in range(DMA_UNROLL):
            p = g * DMA_UNROLL + u
            row_copy(p & (tm - 1), dest_ref[0, 0, p]).start(priority=u % 2)
        return c

    lax.fori_loop(0, pairs // DMA_UNROLL, issue, 0)
    for _ in range(TOP_K):
        pltpu.make_async_copy(hp_ref, xs_ref.at[pl.ds(0, tm * pitch), :], sem).wait()

    @pl.when(t == pl.num_programs(0) - 1)
    def _():
        def run_copy(first, rows):
            dst = xs_ref.at[pl.ds(pl.multiple_of(first * pitch, pitch), rows * pitch), :]
            return pltpu.make_async_copy(hp_ref.at[pl.ds(0, rows * pitch), :], dst, sem)

        def per_expert(e, c):
            used = cnt_ref[e]
            npad = (-used) & (EXPERT_ROWS - 1)
            first = start_ref[e] * STEP_ROWS + used
            sizes = [1 << k for k in range(EXPERT_ROWS.bit_length() - 1)]
            offs = []
            off = first
            for size in sizes:
                offs.append(off)
                off = off + (npad & size)
            for size, o in zip(sizes, offs):
                pl.when((npad & size) != 0)(lambda size=size, o=o: run_copy(o, size).start())
            for size in sizes:
                pl.when((npad & size) != 0)(lambda size=size: run_copy(0, size).wait())
            return c

        lax.fori_loop(0, N_EXPERTS, per_expert, 0)


def _dispatch(hp, dest, counts, step_start, p_rows, pitch, tm=256):
    n = hp.shape[0] // pitch
    tiles = n // tm
    dest_t = dest.reshape(TOP_K, tiles, tm).transpose(1, 0, 2).reshape(tiles, 1, TOP_K * tm)
    grid_spec = pltpu.PrefetchScalarGridSpec(
        num_scalar_prefetch=2,
        grid=(tiles,),
        in_specs=[pl.BlockSpec((1, 1, TOP_K * tm), lambda i, *_: (i, 0, 0), memory_space=pltpu.SMEM),
                  pl.BlockSpec((tm * pitch, LANES), lambda i, *_: (i, 0))],
        out_specs=pl.BlockSpec(memory_space=pl.ANY),
        scratch_shapes=[pltpu.SemaphoreType.DMA(())],
    )
    return pl.pallas_call(
        functools.partial(_dispatch_kernel, tm=tm, pitch=pitch),
        grid_spec=grid_spec,
        out_shape=jax.ShapeDtypeStruct((p_rows * pitch, LANES), U32),
        compiler_params=_cp(("arbitrary",)),
        name="moe_dispatch",
    )(counts, step_start, dest_t, hp)


def _silu_mul(g, u):
    return g / (1.0 + jnp.exp(-g)) * u


def _expert_weights(sched_ref, w_hbm_refs, wbuf_ref, sem):
    b = pl.program_id(0)

    def copies(e, slot):
        return [pltpu.make_async_copy(w.at[e], wbuf_ref.at[slot, i], sem.at[slot])
                for i, w in enumerate(w_hbm_refs)]

    @pl.when(b == 0)
    def _():
        for c in copies(sched_ref[0, 0], 0):
            c.start(priority=1)

    slot = sched_ref[3, b]

    @pl.when(sched_ref[2, b] == 1)
    def _():
        for c in copies(0, slot):
            c.wait()
        nxt = sched_ref[4, b]

        @pl.when(nxt >= 0)
        def _():
            for c in copies(nxt, 1 - slot):
                c.start(priority=1)

    return slot


def _expert_up_kernel(sched_ref, nu_ref, xs_ref, wg_hbm, wu_hbm, o_ref, wbuf_ref, scr_ref, sem):
    b = pl.program_id(0)
    pitch = xs_ref.shape[0] // STEP_ROWS
    slot = _expert_weights(sched_ref, (wg_hbm, wu_hbm), wbuf_ref, sem)

    def sub_block(j):
        x = _load_unpacked(xs_ref, j * EXPERT_ROWS, EXPERT_ROWS, pitch, scr_ref)
        g = jnp.dot(x, wbuf_ref[slot, 0], preferred_element_type=F32)
        u = jnp.dot(x, wbuf_ref[slot, 1], preferred_element_type=F32)
        o_ref[j * EXPERT_ROWS:(j + 1) * EXPERT_ROWS, :] = _silu_mul(g, u).astype(o_ref.dtype)

    for j in range(STEP_ROWS // EXPERT_ROWS):
        pl.when(sched_ref[1, b] > j)(functools.partial(sub_block, j))


def _row_block(b, sched, nu):
    return (jnp.minimum(b, nu[0] - 1), 0)


def _expert_up(xs, w_gate, w_up, sched, n_used, n_steps, pitch):
    _, d, ff = w_gate.shape
    grid_spec = pltpu.PrefetchScalarGridSpec(
        num_scalar_prefetch=2,
        grid=(n_steps,),
        in_specs=[pl.BlockSpec((STEP_ROWS * pitch, LANES), _row_block),
                  pl.BlockSpec(memory_space=pl.ANY), pl.BlockSpec(memory_space=pl.ANY)],
        out_specs=pl.BlockSpec((STEP_ROWS, ff), _row_block),
        scratch_shapes=[pltpu.VMEM((2, 2, d, ff), F32),
                        pltpu.VMEM((2 * EXPERT_ROWS * pitch, LANES), F32),
                        pltpu.SemaphoreType.DMA((2,))],
    )
    return pl.pallas_call(
        _expert_up_kernel,
        grid_spec=grid_spec,
        out_shape=jax.ShapeDtypeStruct((n_steps * STEP_ROWS, ff), BF16),
        compiler_params=_cp(("arbitrary",)),
        name="expert_up",
    )(sched, n_used, xs, w_gate, w_up)


def _expert_down_kernel(sched_ref, nu_ref, h_ref, wd_hbm, o_ref, wbuf_ref, scr_ref, sem):
    b = pl.program_id(0)
    slot = _expert_weights(sched_ref, (wd_hbm,), wbuf_ref, sem)

    def sub_block(j):
        y = jnp.dot(h_ref[j * EXPERT_ROWS:(j + 1) * EXPERT_ROWS, :], wbuf_ref[slot, 0].astype(BF16),
                    preferred_element_type=F32)
        _pack_store(o_ref, j * EXPERT_ROWS, y, scr_ref)

    for j in range(STEP_ROWS // EXPERT_ROWS):
        pl.when(sched_ref[1, b] > j)(functools.partial(sub_block, j))


def _expert_down(hs, w_down, sched, n_used, n_steps, pitch):
    p_rows, ff = hs.shape
    d = w_down.shape[2]
    grid_spec = pltpu.PrefetchScalarGridSpec(
        num_scalar_prefetch=2,
        grid=(n_steps,),
        in_specs=[pl.BlockSpec((STEP_ROWS, ff), _row_block), pl.BlockSpec(memory_space=pl.ANY)],
        out_specs=pl.BlockSpec((STEP_ROWS * pitch, LANES), _row_block),
        scratch_shapes=[pltpu.VMEM((2, 1, ff, d), F32),
                        pltpu.VMEM((2 * EXPERT_ROWS * pitch, LANES), F32),
                        pltpu.SemaphoreType.DMA((2,))],
    )
    return pl.pallas_call(
        _expert_down_kernel,
        grid_spec=grid_spec,
        out_shape=jax.ShapeDtypeStruct((p_rows * pitch, LANES), U32),
        compiler_params=_cp(("arbitrary",)),
        name="expert_down",
    )(sched, n_used, hs, w_down)


def _combine_kernel(dfirst_ref, dnext_ref, gate_ref, x_ref, t_ref, wsd_ref, ys_ref, o_ref, buf_ref, scr_ref,
                    base_ref, sem, *, tm, pitch):
    t = pl.program_id(0)
    pairs = TOP_K * tm
    bpitch = buf_ref.shape[0] // (2 * pairs)

    def row_copy(dref, slot, p):
        src = ys_ref.at[pl.ds(pl.multiple_of(dref[0, 0, p] * pitch, pitch), pitch), :]
        dst = buf_ref.at[pl.ds(pl.multiple_of((slot * pairs + p) * bpitch, SUBLANES), pitch), :]
        return pltpu.make_async_copy(src, dst, sem.at[slot])

    def wait_slot(slot):
        for _ in range(TOP_K):
            pltpu.make_async_copy(ys_ref.at[pl.ds(0, tm * pitch), :], buf_ref.at[pl.ds(0, tm * pitch), :],
                                  sem.at[slot]).wait()

    def issue_tile(dref, slot):
        def body(g, c):
            for u in range(DMA_UNROLL):
                row_copy(dref, slot, g * DMA_UNROLL + u).start(priority=u % 2)
            return c

        lax.fori_loop(0, pairs // DMA_UNROLL, body, 0)

    @pl.when(t == 0)
    def _():
        issue_tile(dfirst_ref, 0)

    @pl.when(t + 1 < pl.num_programs(0))
    def _():
        issue_tile(dnext_ref, (t + 1) % 2)

    base_ref[...] = x_ref[...] + jnp.dot(t_ref[...], wsd_ref[...], preferred_element_type=F32)

    cur = t % 2
    wait_slot(cur)

    for grp in range(tm // COMBINE_GROUP):
        r0 = grp * COMBINE_GROUP
        rows = slice(r0, r0 + COMBINE_GROUP)
        gates = [jnp.broadcast_to(gate_ref[2 * r0:2 * (r0 + COMBINE_GROUP), k:k + 1], (2 * COMBINE_GROUP, LANES))
                 for k in range(TOP_K)]
        for s in range(pitch):
            acc = None
            for k in range(TOP_K):
                term = gates[k] * _load_pairs(buf_ref, cur * pairs + k * tm + r0, COMBINE_GROUP, bpitch, s)
                acc = term if acc is None else acc + term
            even, odd = _split_pairs(acc, scr_ref, grp * pitch + s)
            c0 = slice((2 * s) * LANES, (2 * s + 1) * LANES)
            c1 = slice((2 * s + 1) * LANES, (2 * s + 2) * LANES)
            o_ref[rows, c0] = base_ref[rows, c0] + even
            o_ref[rows, c1] = base_ref[rows, c1] + odd


def _combine(dest, gate_rows, x, t_shared, w_sh_down, ys, pitch, tm=128):
    n, d = x.shape
    ff = t_shared.shape[1]
    tiles = n // tm
    dest_t = dest.reshape(TOP_K, tiles, tm).transpose(1, 0, 2).reshape(tiles, 1, TOP_K * tm)
    dspec = lambda imap: pl.BlockSpec((1, 1, TOP_K * tm), imap, memory_space=pltpu.SMEM)
    bpitch = pitch if (pitch // SUBLANES) % 2 else pitch + SUBLANES
    return pl.pallas_call(
        functools.partial(_combine_kernel, tm=tm, pitch=pitch),
        grid=(tiles,),
        in_specs=[dspec(lambda i: (0, 0, 0)),
                  dspec(lambda i: (jnp.minimum(i + 1, tiles - 1), 0, 0)),
                  pl.BlockSpec((2 * tm, TOP_K), lambda i: (i, 0)),
                  pl.BlockSpec((tm, d), lambda i: (i, 0)),
                  pl.BlockSpec((tm, ff), lambda i: (i, 0)),
                  pl.BlockSpec((ff, d), lambda i: (0, 0)),
                  pl.BlockSpec(memory_space=pl.ANY)],
        out_specs=pl.BlockSpec((tm, d), lambda i: (i, 0)),
        out_shape=jax.ShapeDtypeStruct((n, d), F32),
        scratch_shapes=[pltpu.VMEM((2 * TOP_K * tm * bpitch, LANES), U32),
                        pltpu.VMEM((2 * tm * pitch, LANES), F32),
                        pltpu.VMEM((tm, d), F32),
                        pltpu.SemaphoreType.DMA((2,))],
        compiler_params=_cp(("arbitrary",)),
        name="moe_combine",
    )(dest_t, dest_t, gate_rows, x, t_shared, w_sh_down.astype(BF16), ys)


def _rope_tables(positions):
    half = ROPE_DIM // 2
    inv_freq = ROPE_THETA ** (-jnp.arange(0, ROPE_DIM, 2, dtype=F32) / ROPE_DIM)
    lane = np.arange(LANES)
    ang = positions.reshape(-1).astype(F32)[:, None] * jnp.tile(inv_freq, LANES // half)[None, :]
    cos, sin = jnp.cos(ang), jnp.sin(ang)
    cos_t = cos * (lane < ROPE_DIM).astype(np.float32)
    sin_a = sin * np.where(lane < half, -1.0, 0.0).astype(np.float32)
    sin_b = sin * ((lane >= half) & (lane < ROPE_DIM)).astype(np.float32)
    return cos_t, sin_a, sin_b


def _pad_lanes(v, width):
    return jnp.concatenate([v, jnp.zeros((width - v.shape[0],), v.dtype)])


def kernel(x, mem, positions, g_mix, w_in, g_qa, g_ka, rel_bias, g_cq, w_uq, g_ckv, w_ukv, g_qb, g_kb, w_o, g_cross, g_mem, w_xq, w_xkv, g_qx, g_kx, w_xo, g_ffn, w_router, router_bias, w_sh_gate, w_sh_up, w_sh_down, w_ex_gate, w_ex_up, w_ex_down):
    batch, seq, d = x.shape
    n = batch * seq
    mem_tokens = mem.shape[1]
    width_a = HEADS_A * HEAD_DIM
    x2d = x.reshape(n, d)

    h = _norm_bf16(x2d, g_mix)
    gs_qk = jnp.concatenate([jnp.tile(g_qa * HEAD_DIM ** -0.5, HEADS_A), jnp.tile(g_ka, HEADS_A)])
    w_in_t = w_in.T
    qk, v_a, c = _in_proj_all(h, w_in_t, gs_qk, 2 * width_a, width_a, Q_LORA + KV_LORA)
    w_kpe = jnp.pad(w_in_t[3 * width_a + Q_LORA + KV_LORA:], ((0, LANES - ROPE_DIM), (0, 0)))
    kpe = _proj_t(h, w_kpe, "in_proj_kpe")

    o_a = _attn_a(qk, v_a, _band_bias(rel_bias), batch, seq)

    cos_t, sin_a, sin_b = _rope_tables(positions)
    w_uq_pad = jnp.pad(w_uq.reshape(Q_LORA, HEADS_B, QK_B),
                       ((0, 0), (0, 0), (0, 2 * LANES - QK_B))).reshape(Q_LORA, HEADS_B * 2 * LANES)
    gq = _pad_lanes(g_qb, 2 * LANES).reshape(1, 2 * LANES)
    gk = _pad_lanes(g_kb, 2 * LANES).reshape(1, 2 * LANES)
    qf = _mla_q(c, g_cq, w_uq_pad, gq, cos_t, sin_a, sin_b)
    kf, v_b = _mla_kv(c, g_ckv, kpe, w_ukv, gk, cos_t, sin_a, sin_b)
    o_b = _attn_b(qf, kf, v_b, batch, seq)

    x1 = _out_proj(o_a, o_b, w_o, x2d)

    hw = MEM_HEADS * MEM_HEAD_DIM
    kvx = _normed_proj(mem.reshape(batch * mem_tokens, d), g_mem, w_xkv,
                       jnp.concatenate([jnp.tile(g_kx, MEM_HEADS), jnp.ones((hw,), F32)]), hw, "cross_kv")
    x2 = _cross_attn(x1, g_cross, w_xq, jnp.tile(g_qx * MEM_HEAD_DIM ** -0.5, MEM_HEADS), kvx, w_xo,
                     batch, seq, mem_tokens)

    hp, eid, gate, rank, cnt, t_shared = _router(x2, g_ffn, w_router, router_bias, w_sh_gate, w_sh_up)
    counts = cnt[:, 0]
    nstep = (counts + STEP_ROWS - 1) // STEP_ROWS
    step_end = jnp.cumsum(nstep).astype(I32)
    step_start = step_end - nstep
    n_steps = n * TOP_K // STEP_ROWS + N_EXPERTS
    experts = jnp.arange(N_EXPERTS, dtype=I32)
    start_of = jnp.sum(jnp.where(eid[:, :, None] == experts, step_start, 0), axis=-1)
    dest = start_of * STEP_ROWS + rank
    steps = jnp.arange(n_steps, dtype=I32)
    step_e = jnp.minimum(jnp.sum((step_end[None, :] <= steps[:, None]).astype(I32), axis=1), N_EXPERTS - 1)
    mine = step_e[:, None] == experts[None, :]
    rows_left = jnp.sum(jnp.where(mine, counts[None, :] - STEP_ROWS * (steps[:, None] - step_start[None, :]), 0),
                        axis=1)
    used = steps < step_end[-1]
    rows_here = jnp.where(used, jnp.clip(rows_left, 0, STEP_ROWS), 0)
    nsub = (rows_here + EXPERT_ROWS - 1) // EXPERT_ROWS
    nonempty = counts > 0
    ring_slot = (jnp.cumsum(nonempty.astype(I32)) - 1) % 2
    later = (experts[None, :] > experts[:, None]) & nonempty[None, :]
    succ = jnp.min(jnp.where(later, experts[None, :], N_EXPERTS), axis=1)
    succ = jnp.where(succ == N_EXPERTS, -1, succ)
    per_step = lambda v: jnp.sum(jnp.where(mine, v[None, :], 0), axis=1)
    first = (used & (steps == per_step(step_start))).astype(I32)
    sched = jnp.stack([step_e, nsub, first, per_step(ring_slot), per_step(succ)]).astype(I32)
    n_used = step_end[-1:]
    pitch = d // 2 // LANES

    xs = _dispatch(hp, dest, counts, step_start, n_steps * STEP_ROWS, pitch)
    hs = _expert_up(xs, w_ex_gate, w_ex_up, sched, n_used, n_steps, pitch)
    ys = _expert_down(hs, w_ex_down, sched, n_used, n_steps, pitch)

    out = _combine(dest, jnp.repeat(gate.T, 2, axis=0), x2, t_shared, w_sh_down, ys, pitch)
    return out.reshape(batch, seq, d)
```

```python
import functools

import jax
import jax.numpy as jnp
import numpy as np
from jax import lax
from jax.experimental import pallas as pl
from jax.experimental.pallas import tpu as pltpu

F32 = jnp.float32
BF16 = jnp.bfloat16
I32 = jnp.int32
U32 = jnp.uint32

CHUNK = 64
LEFT_CHUNKS = 8
REL_CLIP = 128
HEAD_DIM = 128
HEADS_A = 16
HEADS_B = 16
Q_LORA = 1024
KV_LORA = 512
NOPE_DIM = 128
ROPE_DIM = 64
V_DIM = 128
QK_B = NOPE_DIM + ROPE_DIM
ROPE_THETA = 10000.0
MEM_HEADS = 4
MEM_HEAD_DIM = 128
N_EXPERTS = 64
N_GROUPS = 8
TOPK_GROUPS = 4
TOP_K = 8
ROUTED_SCALE = 2.5
EPS = 1e-6

LANES = 128
SUBLANES = 8
ATT_BLOCK = 256
EXPERT_ROWS = 256
STEP_ROWS = 2 * EXPERT_ROWS
DMA_UNROLL = 8
COMBINE_GROUP = 16
NEG = -1e30
VMEM_LIMIT = 56 * 1024 * 1024

_NT = (((1,), (1,)), ((), ()))


def _cp(sem, vmem=VMEM_LIMIT):
    return pltpu.CompilerParams(dimension_semantics=sem, vmem_limit_bytes=vmem)


def _rms(x, g):
    return x * lax.rsqrt(jnp.mean(x * x, axis=-1, keepdims=True) + EPS) * g


def _pack_store(ref, first_token, x, scr):
    m, w = x.shape
    pitch = w // (2 * LANES)
    regions = scr.shape[0] // (2 * m)
    for s in range(pitch):
        base = (s % regions) * 2 * m
        scr[pl.ds(base, m, stride=2), :] = x[:, (2 * s) * LANES:(2 * s + 1) * LANES]
        scr[pl.ds(base + 1, m, stride=2), :] = x[:, (2 * s + 1) * LANES:(2 * s + 2) * LANES]
        z = scr[pl.ds(base, 2 * m), :].astype(BF16)
        ref[pl.ds(first_token * pitch + s, m, stride=pitch), :] = pltpu.bitcast(z, U32)


def _load_pairs(ref, first_token, m, pitch, s):
    w = ref[pl.ds(first_token * pitch + s, m, stride=pitch), :]
    return pltpu.bitcast(w, BF16).astype(F32)


def _split_pairs(z, scr, region):
    m = z.shape[0] // 2
    base = region * 2 * m
    scr[pl.ds(base, 2 * m), :] = z
    return scr[pl.ds(base, m, stride=2), :], scr[pl.ds(base + 1, m, stride=2), :]


def _load_unpacked(ref, first_token, m, pitch, scr, dtype=F32):
    regions = scr.shape[0] // (2 * m)
    cols = []
    for s in range(pitch):
        for piece in _split_pairs(_load_pairs(ref, first_token, m, pitch, s), scr, s % regions):
            cols.append(piece.astype(dtype))
    return jnp.concatenate(cols, axis=1)


def _norm_kernel(x_ref, g_ref, o_ref):
    o_ref[...] = _rms(x_ref[...], g_ref[...]).astype(o_ref.dtype)


def _norm_bf16(x, g, tm=256):
    n, d = x.shape
    return pl.pallas_call(
        _norm_kernel,
        grid=(n // tm,),
        in_specs=[pl.BlockSpec((tm, d), lambda i: (i, 0)), pl.BlockSpec((1, d), lambda i: (0, 0))],
        out_specs=pl.BlockSpec((tm, d), lambda i: (i, 0)),
        out_shape=jax.ShapeDtypeStruct((n, d), BF16),
        compiler_params=_cp(("parallel",)),
        name="norm_mix",
    )(x, g.reshape(1, d))


def _proj_t_kernel(h_ref, w_ref, o_ref):
    o_ref[...] = lax.dot_general(h_ref[...], w_ref[...].astype(BF16), _NT, preferred_element_type=F32)


def _proj_t(h, w_t, name, tm=1024):
    n, d = h.shape
    nc = w_t.shape[0]
    return pl.pallas_call(
        _proj_t_kernel,
        grid=(n // tm,),
        in_specs=[pl.BlockSpec((tm, d), lambda i: (i, 0)), pl.BlockSpec((nc, d), lambda i: (0, 0))],
        out_specs=pl.BlockSpec((tm, nc), lambda i: (i, 0)),
        out_shape=jax.ShapeDtypeStruct((n, nc), F32),
        compiler_params=_cp(("parallel",)),
        name=name,
    )(h, w_t)


def _in_proj_all_kernel(h_ref, w_ref, gs_ref, qk_ref, v_ref, c_ref, *, tn, n_qk, n_v):
    j = pl.program_id(1)
    acc = lax.dot_general(h_ref[...], w_ref[...].astype(BF16), _NT, preferred_element_type=F32)

    @pl.when(j < n_qk)
    def _():
        for c in range(tn // LANES):
            sl = slice(c * LANES, (c + 1) * LANES)
            qk_ref[:, sl] = _rms(acc[:, sl], gs_ref[:, sl]).astype(qk_ref.dtype)

    @pl.when((j >= n_qk) & (j < n_qk + n_v))
    def _():
        v_ref[...] = acc.astype(v_ref.dtype)

    @pl.when(j >= n_qk + n_v)
    def _():
        c_ref[...] = acc


def _in_proj_all(h, w_t, gs_qk, w_qk, w_v, w_c, tm=1024, tn=512):
    n, d = h.shape
    n_qk, n_v, n_c = w_qk // tn, w_v // tn, w_c // tn
    return pl.pallas_call(
        functools.partial(_in_proj_all_kernel, tn=tn, n_qk=n_qk, n_v=n_v),
        grid=(n // tm, n_qk + n_v + n_c),
        in_specs=[pl.BlockSpec((tm, d), lambda i, j: (i, 0)),
                  pl.BlockSpec((tn, d), lambda i, j: (j, 0)),
                  pl.BlockSpec((1, tn), lambda i, j: (0, jnp.minimum(j, n_qk - 1)))],
        out_specs=[pl.BlockSpec((tm, tn), lambda i, j: (i, jnp.minimum(j, n_qk - 1))),
                   pl.BlockSpec((tm, tn), lambda i, j: (i, jnp.clip(j - n_qk, 0, n_v - 1))),
                   pl.BlockSpec((tm, tn), lambda i, j: (i, jnp.clip(j - n_qk - n_v, 0, n_c - 1)))],
        out_shape=[jax.ShapeDtypeStruct((n, w_qk), BF16), jax.ShapeDtypeStruct((n, w_v), BF16),
                   jax.ShapeDtypeStruct((n, w_c), F32)],
        compiler_params=_cp(("parallel", "arbitrary")),
        name="in_proj",
    )(h, w_t, gs_qk.reshape(1, w_qk))


def _attn_a_kernel(q_ref, k0_ref, k1_ref, k2_ref, v0_ref, v1_ref, v2_ref, b_ref, o_ref, *, heads):
    i = pl.program_id(2)
    k_refs = (k0_ref, k1_ref, k2_ref)
    v_refs = (v0_ref, v1_ref, v2_ref)
    for hh in range(heads):
        sl = slice(hh * HEAD_DIM, (hh + 1) * HEAD_DIM)
        q = q_ref[:, sl]
        s = []
        for d in range(3):
            sd = lax.dot_general(q, k_refs[d][:, sl], _NT, preferred_element_type=F32)
            sd = sd + b_ref[hh, :, d * ATT_BLOCK:(d + 1) * ATT_BLOCK]
            if d > 0:
                sd = jnp.where(i >= d, sd, NEG)
            s.append(sd)
        m = jnp.maximum(jnp.maximum(s[0].max(-1, keepdims=True), s[1].max(-1, keepdims=True)),
                        s[2].max(-1, keepdims=True))
        l = jnp.zeros_like(m)
        o = jnp.zeros((q.shape[0], HEAD_DIM), F32)
        for d in range(3):
            p = jnp.exp(s[d] - m)
            l = l + p.sum(-1, keepdims=True)
            o = o + jnp.dot(p.astype(BF16), v_refs[d][:, sl], preferred_element_type=F32)
        o_ref[:, sl] = (o / l).astype(o_ref.dtype)


def _band_bias(rel_bias):
    blk = ATT_BLOCK
    period = 2 * blk + 1
    heads = rel_bias.shape[0]
    r = np.arange(blk)[:, None]
    c = np.arange(blk)[None, :]
    per_blk = blk // CHUNK
    k = np.arange(period)
    delta = np.where(k <= blk, k, k - period)
    tiles = []
    for d in range(3):
        idx = np.clip(blk * d - delta, -(CHUNK - 1), REL_CLIP) + CHUNK - 1
        w = rel_bias[:, idx].astype(F32)
        b = jnp.tile(w, (1, blk))[:, :blk * (period - 1)].reshape(heads, blk, period - 1)[:, :, :blk]
        cdiff = per_blk * d + r // CHUNK - c // CHUNK
        valid = (cdiff >= 0) & (cdiff <= LEFT_CHUNKS)
        tiles.append(jnp.where(valid[None], b, NEG))
    return jnp.concatenate(tiles, axis=-1)


def _attn_a(qk, v, bias, batch, seq, heads_per_step=8):
    n = qk.shape[0]
    nq = seq // ATT_BLOCK
    hw = heads_per_step * HEAD_DIM
    groups = HEADS_A // heads_per_step
    kcol0 = HEADS_A * HEAD_DIM // hw

    def kspec(d, col0):
        return pl.BlockSpec((ATT_BLOCK, hw), lambda g, b, i: (b * nq + jnp.maximum(i - d, 0), col0 + g))

    return pl.pallas_call(
        functools.partial(_attn_a_kernel, heads=heads_per_step),
        grid=(groups, batch, nq),
        in_specs=[pl.BlockSpec((ATT_BLOCK, hw), lambda g, b, i: (b * nq + i, g)),
                  kspec(0, kcol0), kspec(1, kcol0), kspec(2, kcol0),
                  kspec(0, 0), kspec(1, 0), kspec(2, 0),
                  pl.BlockSpec((heads_per_step, ATT_BLOCK, 3 * ATT_BLOCK), lambda g, b, i: (g, 0, 0))],
        out_specs=pl.BlockSpec((ATT_BLOCK, hw), lambda g, b, i: (b * nq + i, g)),
        out_shape=jax.ShapeDtypeStruct((n, HEADS_A * HEAD_DIM), BF16),
        compiler_params=_cp(("parallel", "parallel", "arbitrary")),
        name="attn_band",
    )(qk, qk, qk, qk, v, v, v, bias)


def _rope_pe(pe, g, cos_ref, sa_ref, sb_ref):
    ss = jnp.sum(pe * pe, axis=-1, keepdims=True) * (1.0 / ROPE_DIM)
    pn = pe * lax.rsqrt(ss + EPS) * g
    half = ROPE_DIM // 2
    return (pn * cos_ref[...] + pltpu.roll(pn, LANES - half, 1) * sa_ref[...]
            + pltpu.roll(pn, half, 1) * sb_ref[...])


def _mla_q_kernel(c_ref, gc_ref, w_ref, gq_ref, cos_ref, sa_ref, sb_ref, o_ref, xn_ref, *, heads, scale):
    @pl.when(pl.program_id(1) == 0)
    def _():
        xn_ref[...] = _rms(c_ref[...], gc_ref[...]).astype(BF16)

    acc = jnp.dot(xn_ref[...], w_ref[...].astype(BF16), preferred_element_type=F32)
    for hh in range(heads):
        base = hh * 2 * LANES
        nope = acc[:, base:base + LANES]
        pe = acc[:, base + LANES:base + 2 * LANES]
        o_ref[:, base:base + LANES] = (_rms(nope, gq_ref[:, :LANES]) * scale).astype(o_ref.dtype)
        o_ref[:, base + LANES:base + 2 * LANES] = (
            _rope_pe(pe, gq_ref[:, LANES:], cos_ref, sa_ref, sb_ref) * scale).astype(o_ref.dtype)


def _mla_q(c, g_cq, w_uq_pad, gq, cos_t, sin_a, sin_b, tm=512, heads_per_step=8):
    n = c.shape[0]
    tn = heads_per_step * 2 * LANES
    ncols = w_uq_pad.shape[1]
    row = lambda i, j: (i, 0)
    return pl.pallas_call(
        functools.partial(_mla_q_kernel, heads=heads_per_step, scale=QK_B ** -0.5),
        grid=(n // tm, ncols // tn),
        in_specs=[pl.BlockSpec((tm, Q_LORA), row),
                  pl.BlockSpec((1, Q_LORA), lambda i, j: (0, 0)),
                  pl.BlockSpec((Q_LORA, tn), lambda i, j: (0, j)),
                  pl.BlockSpec((1, 2 * LANES), lambda i, j: (0, 0)),
                  pl.BlockSpec((tm, LANES), row), pl.BlockSpec((tm, LANES), row), pl.BlockSpec((tm, LANES), row)],
        out_specs=pl.BlockSpec((tm, tn), lambda i, j: (i, j)),
        out_shape=jax.ShapeDtypeStruct((n, ncols), BF16),
        scratch_shapes=[pltpu.VMEM((tm, Q_LORA), BF16)],
        compiler_params=_cp(("parallel", "arbitrary")),
        name="mla_q_proj",
    )(c, g_cq.reshape(1, Q_LORA), w_uq_pad, gq, cos_t, sin_a, sin_b)


def _mla_kv_kernel(c_ref, gc_ref, kpe_ref, w_ref, gk_ref, cos_ref, sa_ref, sb_ref, k_ref, v_ref,
                   xn_ref, pe_ref, *, heads):
    @pl.when(pl.program_id(1) == 0)
    def _():
        xn_ref[...] = _rms(c_ref[...], gc_ref[...]).astype(BF16)
        pe_ref[...] = _rope_pe(kpe_ref[...], gk_ref[:, LANES:], cos_ref, sa_ref, sb_ref).astype(BF16)

    acc = jnp.dot(xn_ref[...], w_ref[...].astype(BF16), preferred_element_type=F32)
    for hh in range(heads):
        base = hh * 2 * LANES
        k_ref[:, base:base + LANES] = _rms(acc[:, base:base + LANES], gk_ref[:, :LANES]).astype(k_ref.dtype)
        k_ref[:, base + LANES:base + 2 * LANES] = pe_ref[...]
        v_ref[:, hh * LANES:(hh + 1) * LANES] = acc[:, base + LANES:base + 2 * LANES].astype(v_ref.dtype)


def _mla_kv(c, g_ckv, kpe, w_ukv, gk, cos_t, sin_a, sin_b, tm=512, heads_per_step=8):
    n = c.shape[0]
    tn = heads_per_step * 2 * LANES
    ncols = w_ukv.shape[1]
    cblk = Q_LORA // KV_LORA
    row = lambda i, j: (i, 0)
    return pl.pallas_call(
        functools.partial(_mla_kv_kernel, heads=heads_per_step),
        grid=(n // tm, ncols // tn),
        in_specs=[pl.BlockSpec((tm, KV_LORA), lambda i, j: (i, cblk)),
                  pl.BlockSpec((1, KV_LORA), lambda i, j: (0, 0)),
                  pl.BlockSpec((tm, LANES), row),
                  pl.BlockSpec((KV_LORA, tn), lambda i, j: (0, j)),
                  pl.BlockSpec((1, 2 * LANES), lambda i, j: (0, 0)),
                  pl.BlockSpec((tm, LANES), row), pl.BlockSpec((tm, LANES), row), pl.BlockSpec((tm, LANES), row)],
        out_specs=[pl.BlockSpec((tm, tn), lambda i, j: (i, j)),
                   pl.BlockSpec((tm, tn // 2), lambda i, j: (i, j))],
        out_shape=[jax.ShapeDtypeStruct((n, ncols), BF16),
                   jax.ShapeDtypeStruct((n, ncols // 2), BF16)],
        scratch_shapes=[pltpu.VMEM((tm, KV_LORA), BF16), pltpu.VMEM((tm, LANES), BF16)],
        compiler_params=_cp(("parallel", "arbitrary")),
        name="mla_kv_proj",
    )(c, g_ckv.reshape(1, KV_LORA), kpe, w_ukv, gk, cos_t, sin_a, sin_b)


def _attn_b_kernel(q_ref, k_ref, v_ref, o_ref, *, heads):
    i = pl.program_id(2)
    tq = q_ref.shape[0]
    qw = 2 * LANES
    qs = [q_ref[:, h * qw:(h + 1) * qw] for h in range(heads)]

    def step(first_blk, width, carry, masked):
        start = pl.multiple_of(first_blk * ATT_BLOCK, ATT_BLOCK)
        if masked:
            per_blk = ATT_BLOCK // CHUNK
            r = i * per_blk + lax.broadcasted_iota(I32, (tq, width), 0) // CHUNK
            c = first_blk * per_blk + lax.broadcasted_iota(I32, (tq, width), 1) // CHUNK
            keep = c <= r
        out = []
        for h in range(heads):
            m, l, acc = carry[h]
            s = lax.dot_general(qs[h], k_ref[pl.ds(start, width), h * qw:(h + 1) * qw], _NT,
                                preferred_element_type=F32)
            if masked:
                s = jnp.where(keep, s, NEG)
            m_new = jnp.maximum(m, s.max(-1, keepdims=True))
            alpha = jnp.exp(m - m_new)
            p = jnp.exp(s - m_new)
            l = alpha * l + p.sum(-1, keepdims=True)
            acc = alpha * acc + jnp.dot(p.astype(BF16), v_ref[pl.ds(start, width), h * V_DIM:(h + 1) * V_DIM],
                                        preferred_element_type=F32)
            out.append((m_new, l, acc))
        return tuple(out)

    def finish(final):
        for h in range(heads):
            _, l, acc = final[h]
            o_ref[:, h * V_DIM:(h + 1) * V_DIM] = (acc / l).astype(o_ref.dtype)

    init = tuple((jnp.full((tq, 1), NEG, F32), jnp.zeros((tq, 1), F32), jnp.zeros((tq, V_DIM), F32))
                 for _ in range(heads))
    carry = lax.fori_loop(0, i // 2, lambda j, c: step(2 * j, 2 * ATT_BLOCK, c, False), init)

    @pl.when(i % 2 == 1)
    def _():
        finish(step(i - 1, 2 * ATT_BLOCK, carry, True))

    @pl.when(i % 2 == 0)
    def _():
        finish(step(i, ATT_BLOCK, carry, True))


def _attn_b(qf, kf, vb, batch, seq, heads_per_step=8):
    n = qf.shape[0]
    nq = seq // ATT_BLOCK
    qw = heads_per_step * 2 * LANES
    vw = heads_per_step * V_DIM
    return pl.pallas_call(
        functools.partial(_attn_b_kernel, heads=heads_per_step),
        grid=(batch, HEADS_B // heads_per_step, nq),
        in_specs=[pl.BlockSpec((ATT_BLOCK, qw), lambda b, g, i: (b * nq + i, g)),
                  pl.BlockSpec((seq, qw), lambda b, g, i: (b, g)),
                  pl.BlockSpec((seq, vw), lambda b, g, i: (b, g))],
        out_specs=pl.BlockSpec((ATT_BLOCK, vw), lambda b, g, i: (b * nq + i, g)),
        out_shape=jax.ShapeDtypeStruct((n, HEADS_B * V_DIM), BF16),
        compiler_params=_cp(("parallel", "parallel", "arbitrary")),
        name="attn_latent",
    )(qf, kf, vb)


def _out_proj_kernel(oa_ref, ob_ref, w_ref, x_ref, o_ref, wb_ref):
    @pl.when(pl.program_id(1) == 0)
    def _():
        wb_ref[...] = w_ref[...].astype(BF16)

    ka = oa_ref.shape[1]
    acc = jnp.dot(oa_ref[...], wb_ref[:ka, :], preferred_element_type=F32)
    acc = acc + jnp.dot(ob_ref[...], wb_ref[ka:, :], preferred_element_type=F32)
    o_ref[...] = x_ref[...] + acc


def _out_proj(oa, ob, w_o, x, tm=256, tn=1024):
    n, ka = oa.shape
    kb = ob.shape[1]
    d = w_o.shape[1]
    return pl.pallas_call(
        _out_proj_kernel,
        grid=(d // tn, n // tm),
        in_specs=[pl.BlockSpec((tm, ka), lambda j, i: (i, 0)),
                  pl.BlockSpec((tm, kb), lambda j, i: (i, 0)),
                  pl.BlockSpec((ka + kb, tn), lambda j, i: (0, j)),
                  pl.BlockSpec((tm, tn), lambda j, i: (i, j))],
        out_specs=pl.BlockSpec((tm, tn), lambda j, i: (i, j)),
        out_shape=jax.ShapeDtypeStruct((n, d), F32),
        scratch_shapes=[pltpu.VMEM((ka + kb, tn), BF16)],
        compiler_params=_cp(("parallel", "arbitrary")),
        name="out_proj",
    )(oa, ob, w_o, x)


def _normed_proj_kernel(x_ref, g_ref, w_ref, gs_ref, o_ref, wb_ref, *, norm_cols):
    @pl.when(pl.program_id(0) == 0)
    def _():
        wb_ref[...] = w_ref[...].astype(BF16)

    h = _rms(x_ref[...], g_ref[...]).astype(BF16)
    acc = jnp.dot(h, wb_ref[...], preferred_element_type=F32)
    for c in range(acc.shape[1] // LANES):
        sl = slice(c * LANES, (c + 1) * LANES)
        if c * LANES < norm_cols:
            o_ref[:, sl] = _rms(acc[:, sl], gs_ref[:, sl]).astype(o_ref.dtype)
        else:
            o_ref[:, sl] = acc[:, sl].astype(o_ref.dtype)


def _normed_proj(x, g, w, gs, norm_cols, name, tm=256):
    n, d = x.shape
    nc = w.shape[1]
    return pl.pallas_call(
        functools.partial(_normed_proj_kernel, norm_cols=norm_cols),
        grid=(n // tm,),
        in_specs=[pl.BlockSpec((tm, d), lambda i: (i, 0)),
                  pl.BlockSpec((1, d), lambda i: (0, 0)),
                  pl.BlockSpec((d, nc), lambda i: (0, 0)),
                  pl.BlockSpec((1, nc), lambda i: (0, 0))],
        out_specs=pl.BlockSpec((tm, nc), lambda i: (i, 0)),
        out_shape=jax.ShapeDtypeStruct((n, nc), BF16),
        scratch_shapes=[pltpu.VMEM((d, nc), BF16)],
        compiler_params=_cp(("arbitrary",)),
        name=name,
    )(x, g.reshape(1, d), w, gs.reshape(1, nc))


def _cross_kernel(x_ref, g_ref, wq_ref, gq_ref, k_ref, v_ref, wo_ref, o_ref):
    x = x_ref[...]
    q = jnp.dot(_rms(x, g_ref[...]).astype(BF16), wq_ref[...], preferred_element_type=F32)
    outs = []
    for hh in range(MEM_HEADS):
        sl = slice(hh * MEM_HEAD_DIM, (hh + 1) * MEM_HEAD_DIM)
        qh = _rms(q[:, sl], gq_ref[:, sl]).astype(BF16)
        s = lax.dot_general(qh, k_ref[:, sl], _NT, preferred_element_type=F32)
        p = jnp.exp(s - s.max(-1, keepdims=True))
        o = jnp.dot(p.astype(BF16), v_ref[:, sl], preferred_element_type=F32)
        outs.append((o / p.sum(-1, keepdims=True)).astype(BF16))
    acc = x
    for hh in range(MEM_HEADS):
        sl = slice(hh * MEM_HEAD_DIM, (hh + 1) * MEM_HEAD_DIM)
        acc = acc + jnp.dot(outs[hh], wo_ref[sl, :], preferred_element_type=F32)
    o_ref[...] = acc


def _cross_attn(x, g, w_xq, gq, kvx, w_xo, batch, seq, mem_tokens, tm=256):
    n, d = x.shape
    per_b = seq // tm
    hw = MEM_HEADS * MEM_HEAD_DIM
    const = lambda b, i: (0, 0)
    return pl.pallas_call(
        _cross_kernel,
        grid=(batch, per_b),
        in_specs=[pl.BlockSpec((tm, d), lambda b, i: (b * per_b + i, 0)),
                  pl.BlockSpec((1, d), const),
                  pl.BlockSpec((d, hw), const),
                  pl.BlockSpec((1, hw), const),
                  pl.BlockSpec((mem_tokens, hw), lambda b, i: (b, 0)),
                  pl.BlockSpec((mem_tokens, hw), lambda b, i: (b, 1)),
                  pl.BlockSpec((hw, d), const)],
        out_specs=pl.BlockSpec((tm, d), lambda b, i: (b * per_b + i, 0)),
        out_shape=jax.ShapeDtypeStruct((n, d), F32),
        compiler_params=_cp(("parallel", "arbitrary")),
        name="cross_attn",
    )(x, g.reshape(1, d), w_xq.astype(BF16), gq.reshape(1, hw), kvx, kvx, w_xo.astype(BF16))


def _router_kernel(x_ref, g_ref, wr_ref, b_ref, tri_ref, wg_ref, wu_ref, hp_ref, eid_ref, gate_ref, rank_ref,
                   cnt_ref, ts_ref, run_ref, scr_ref):
    t = pl.program_id(0)

    @pl.when(t == 0)
    def _():
        run_ref[...] = jnp.zeros_like(run_ref)

    h = _rms(x_ref[...], g_ref[...])
    _pack_store(hp_ref, 0, h, scr_ref)
    hb = h.astype(BF16)
    ts_ref[...] = _silu_mul(jnp.dot(hb, wg_ref[...], preferred_element_type=F32),
                            jnp.dot(hb, wu_ref[...], preferred_element_type=F32)).astype(ts_ref.dtype)
    tm = h.shape[0]
    per_g = N_EXPERTS // N_GROUPS

    wr = wr_ref[...]
    wr_hi = wr.astype(BF16)
    wr_lo = (wr - wr_hi.astype(F32)).astype(BF16)
    h_lo = (h - hb.astype(F32)).astype(BF16)
    logits = (lax.dot_general(wr_hi, hb, _NT, preferred_element_type=F32)
              + lax.dot_general(wr_lo, hb, _NT, preferred_element_type=F32)
              + lax.dot_general(wr_hi, h_lo, _NT, preferred_element_type=F32))
    scores = 1.0 / (1.0 + jnp.exp(-logits))
    choice = scores + b_ref[...]

    sub = lax.broadcasted_iota(I32, (per_g, tm), 0).astype(F32)
    rows = []
    for g in range(N_GROUPS):
        c = choice[g * per_g:(g + 1) * per_g, :]
        m1 = c.max(0, keepdims=True)
        first = jnp.where(c == m1, sub, float(per_g)).min(0, keepdims=True)
        m2 = jnp.where(sub == first, -jnp.inf, c).max(0, keepdims=True)
        rows.append(m1 + m2)
    gs = jnp.concatenate(rows, axis=0)

    gsub = lax.broadcasted_iota(I32, (N_GROUPS, tm), 0).astype(F32)
    beaten = jnp.zeros((N_GROUPS, tm), F32)
    for g2 in range(N_GROUPS):
        row = gs[g2:g2 + 1, :]
        wins = (row > gs) | ((row == gs) & (gsub > float(g2)))
        beaten = beaten + jnp.where(wins, 1.0, 0.0)
    g_ok = jnp.where(beaten < TOPK_GROUPS, 1.0, 0.0)
    e_ok = jnp.concatenate(
        [jnp.broadcast_to(g_ok[g:g + 1, :], (per_g, tm)) for g in range(N_GROUPS)], axis=0)

    eiota = lax.broadcasted_iota(I32, (N_EXPERTS, tm), 0).astype(F32)
    masked = jnp.where(e_ok > 0.5, choice, -jnp.inf)
    chosen = jnp.zeros((N_EXPERTS, tm), F32)
    eids, ws = [], []
    for _ in range(TOP_K):
        m = masked.max(0, keepdims=True)
        idx = jnp.where(masked == m, eiota, float(N_EXPERTS)).min(0, keepdims=True)
        sel = eiota == idx
        ws.append(jnp.where(sel, scores, 0.0).sum(0, keepdims=True))
        masked = jnp.where(sel, -jnp.inf, masked)
        chosen = jnp.where(sel, 1.0, chosen)
        eids.append(idx)
    wsum = ws[0]
    for w in ws[1:]:
        wsum = wsum + w
    denom = wsum + 1e-20

    pos = jnp.dot(chosen.astype(BF16), tri_ref[...], preferred_element_type=F32) + run_ref[:, 0:1]
    run_ref[...] = run_ref[...] + chosen.sum(1, keepdims=True)
    cnt_ref[...] = run_ref[...].astype(I32)

    for r in range(TOP_K):
        eid_ref[r:r + 1, :] = eids[r].astype(I32)
        gate_ref[r:r + 1, :] = ws[r] / denom * ROUTED_SCALE
        rank_ref[r:r + 1, :] = jnp.where(eiota == eids[r], pos, 0.0).sum(0, keepdims=True).astype(I32)


def _router(x, g, w_router, router_bias, w_sh_gate, w_sh_up, tm=256):
    n, d = x.shape
    ne = w_router.shape[1]
    ff = w_sh_gate.shape[1]
    tri = (jnp.arange(tm)[:, None] < jnp.arange(tm)[None, :]).astype(BF16)
    row8 = lambda i: (0, i)
    pitch = d // 2 // LANES
    return pl.pallas_call(
        _router_kernel,
        grid=(n // tm,),
        in_specs=[pl.BlockSpec((tm, d), lambda i: (i, 0)),
                  pl.BlockSpec((1, d), lambda i: (0, 0)),
                  pl.BlockSpec((ne, d), lambda i: (0, 0)),
                  pl.BlockSpec((ne, 1), lambda i: (0, 0)),
                  pl.BlockSpec((tm, tm), lambda i: (0, 0)),
                  pl.BlockSpec((d, ff), lambda i: (0, 0)), pl.BlockSpec((d, ff), lambda i: (0, 0))],
        out_specs=[pl.BlockSpec((tm * pitch, LANES), lambda i: (i, 0)),
                   pl.BlockSpec((TOP_K, tm), row8), pl.BlockSpec((TOP_K, tm), row8),
                   pl.BlockSpec((TOP_K, tm), row8),
                   pl.BlockSpec((ne, LANES), lambda i: (0, 0)),
                   pl.BlockSpec((tm, ff), lambda i: (i, 0))],
        out_shape=[jax.ShapeDtypeStruct((n * pitch, LANES), U32),
                   jax.ShapeDtypeStruct((TOP_K, n), I32), jax.ShapeDtypeStruct((TOP_K, n), F32),
                   jax.ShapeDtypeStruct((TOP_K, n), I32),
                   jax.ShapeDtypeStruct((ne, LANES), I32),
                   jax.ShapeDtypeStruct((n, ff), BF16)],
        scratch_shapes=[pltpu.VMEM((ne, LANES), F32), pltpu.VMEM((2 * tm * pitch, LANES), F32)],
        compiler_params=_cp(("arbitrary",)),
        name="router",
    )(x, g.reshape(1, d), w_router.T, router_bias.reshape(ne, 1), tri,
      w_sh_gate.astype(BF16), w_sh_up.astype(BF16))


def _dispatch_kernel(cnt_ref, start_ref, dest_ref, hp_ref, xs_ref, sem, *, tm, pitch):
    t = pl.program_id(0)
    pairs = TOP_K * tm

    def row_copy(r, slot):
        src = hp_ref.at[pl.ds(pl.multiple_of(r * pitch, pitch), pitch), :]
        dst = xs_ref.at[pl.ds(pl.multiple_of(slot * pitch, pitch), pitch), :]
        return pltpu.make_async_copy(src, dst, sem)

    def issue(g, c):
        for u in range(DMA_UNROLL):
            p = g * DMA_UNROLL + u
            row_copy(p & (tm - 1), dest_ref[0, 0, p]).start(priority=u % 2)
        return c

    lax.fori_loop(0, pairs // DMA_UNROLL, issue, 0)
    for _ in range(TOP_K):
        pltpu.make_async_copy(hp_ref, xs_ref.at[pl.ds(0, tm * pitch), :], sem).wait()

    @pl.when(t == pl.num_programs(0) - 1)
    def _():
        def run_copy(first, rows):
            dst = xs_ref.at[pl.ds(pl.multiple_of(first * pitch, pitch), rows * pitch), :]
            return pltpu.make_async_copy(hp_ref.at[pl.ds(0, rows * pitch), :], dst, sem)

        def per_expert(e, c):
            used = cnt_ref[e]
            npad = (-used) & (EXPERT_ROWS - 1)
            first = start_ref[e] * STEP_ROWS + used
            sizes = [1 << k for k in range(EXPERT_ROWS.bit_length() - 1)]
            offs = []
            off = first
            for size in sizes:
                offs.append(off)
                off = off + (npad & size)
            for size, o in zip(sizes, offs):
                pl.when((npad & size) != 0)(lambda size=size, o=o: run_copy(o, size).start())
            for size in sizes:
                pl.when((npad & size) != 0)(lambda size=size: run_copy(0, size).wait())
            return c

        lax.fori_loop(0, N_EXPERTS, per_expert, 0)


def _dispatch(hp, dest, counts, step_start, p_rows, pitch, tm=256):
    n = hp.shape[0] // pitch
    tiles = n // tm
    dest_t = dest.reshape(TOP_K, tiles, tm).transpose(1, 0, 2).reshape(tiles, 1, TOP_K * tm)
    grid_spec = pltpu.PrefetchScalarGridSpec(
        num_scalar_prefetch=2,
        grid=(tiles,),
        in_specs=[pl.BlockSpec((1, 1, TOP_K * tm), lambda i, *_: (i, 0, 0), memory_space=pltpu.SMEM),
                  pl.BlockSpec((tm * pitch, LANES), lambda i, *_: (i, 0))],
        out_specs=pl.BlockSpec(memory_space=pl.ANY),
        scratch_shapes=[pltpu.SemaphoreType.DMA(())],
    )
    return pl.pallas_call(
        functools.partial(_dispatch_kernel, tm=tm, pitch=pitch),
        grid_spec=grid_spec,
        out_shape=jax.ShapeDtypeStruct((p_rows * pitch, LANES), U32),
        compiler_params=_cp(("arbitrary",)),
        name="moe_dispatch",
    )(counts, step_start, dest_t, hp)


def _silu_mul(g, u):
    return g / (1.0 + jnp.exp(-g)) * u


def _expert_weights(sched_ref, w_hbm_refs, wbuf_ref, sem):
    b = pl.program_id(0)

    def copies(e, slot):
        return [pltpu.make_async_copy(w.at[e], wbuf_ref.at[slot, i], sem.at[slot])
                for i, w in enumerate(w_hbm_refs)]

    @pl.when(b == 0)
    def _():
        for c in copies(sched_ref[0, 0], 0):
            c.start(priority=1)

    slot = sched_ref[3, b]

    @pl.when(sched_ref[2, b] == 1)
    def _():
        for c in copies(0, slot):
            c.wait()
        nxt = sched_ref[4, b]

        @pl.when(nxt >= 0)
        def _():
            for c in copies(nxt, 1 - slot):
                c.start(priority=1)

    return slot


def _expert_up_kernel(sched_ref, nu_ref, xs_ref, wg_hbm, wu_hbm, o_ref, wbuf_ref, scr_ref, sem):
    b = pl.program_id(0)
    pitch = xs_ref.shape[0] // STEP_ROWS
    slot = _expert_weights(sched_ref, (wg_hbm, wu_hbm), wbuf_ref, sem)

    def sub_block(j):
        x = _load_unpacked(xs_ref, j * EXPERT_ROWS, EXPERT_ROWS, pitch, scr_ref)
        g = jnp.dot(x, wbuf_ref[slot, 0], preferred_element_type=F32)
        u = jnp.dot(x, wbuf_ref[slot, 1], preferred_element_type=F32)
        o_ref[j * EXPERT_ROWS:(j + 1) * EXPERT_ROWS, :] = _silu_mul(g, u).astype(o_ref.dtype)

    for j in range(STEP_ROWS // EXPERT_ROWS):
        pl.when(sched_ref[1, b] > j)(functools.partial(sub_block, j))


def _row_block(b, sched, nu):
    return (jnp.minimum(b, nu[0] - 1), 0)


def _expert_up(xs, w_gate, w_up, sched, n_used, n_steps, pitch):
    _, d, ff = w_gate.shape
    grid_spec = pltpu.PrefetchScalarGridSpec(
        num_scalar_prefetch=2,
        grid=(n_steps,),
        in_specs=[pl.BlockSpec((STEP_ROWS * pitch, LANES), _row_block),
                  pl.BlockSpec(memory_space=pl.ANY), pl.BlockSpec(memory_space=pl.ANY)],
        out_specs=pl.BlockSpec((STEP_ROWS, ff), _row_block),
        scratch_shapes=[pltpu.VMEM((2, 2, d, ff), F32),
                        pltpu.VMEM((2 * EXPERT_ROWS * pitch, LANES), F32),
                        pltpu.SemaphoreType.DMA((2,))],
    )
    return pl.pallas_call(
        _expert_up_kernel,
        grid_spec=grid_spec,
        out_shape=jax.ShapeDtypeStruct((n_steps * STEP_ROWS, ff), BF16),
        compiler_params=_cp(("arbitrary",)),
        name="expert_up",
    )(sched, n_used, xs, w_gate, w_up)


def _expert_down_kernel(sched_ref, nu_ref, h_ref, wd_hbm, o_ref, wbuf_ref, scr_ref, sem):
    b = pl.program_id(0)
    slot = _expert_weights(sched_ref, (wd_hbm,), wbuf_ref, sem)

    def sub_block(j):
        y = jnp.dot(h_ref[j * EXPERT_ROWS:(j + 1) * EXPERT_ROWS, :], wbuf_ref[slot, 0].astype(BF16),
                    preferred_element_type=F32)
        _pack_store(o_ref, j * EXPERT_ROWS, y, scr_ref)

    for j in range(STEP_ROWS // EXPERT_ROWS):
        pl.when(sched_ref[1, b] > j)(functools.partial(sub_block, j))


def _expert_down(hs, w_down, sched, n_used, n_steps, pitch):
    p_rows, ff = hs.shape
    d = w_down.shape[2]
    grid_spec = pltpu.PrefetchScalarGridSpec(
        num_scalar_prefetch=2,
        grid=(n_steps,),
        in_specs=[pl.BlockSpec((STEP_ROWS, ff), _row_block), pl.BlockSpec(memory_space=pl.ANY)],
        out_specs=pl.BlockSpec((STEP_ROWS * pitch, LANES), _row_block),
        scratch_shapes=[pltpu.VMEM((2, 1, ff, d), F32),
                        pltpu.VMEM((2 * EXPERT_ROWS * pitch, LANES), F32),
                        pltpu.SemaphoreType.DMA((2,))],
    )
    return pl.pallas_call(
        _expert_down_kernel,
        grid_spec=grid_spec,
        out_shape=jax.ShapeDtypeStruct((p_rows * pitch, LANES), U32),
        compiler_params=_cp(("arbitrary",)),
        name="expert_down",
    )(sched, n_used, hs, w_down)


def _combine_kernel(dfirst_ref, dnext_ref, gate_ref, x_ref, t_ref, wsd_ref, ys_ref, o_ref, buf_ref, scr_ref,
                    base_ref, sem, *, tm, pitch):
    t = pl.program_id(0)
    pairs = TOP_K * tm
    bpitch = buf_ref.shape[0] // (2 * pairs)

    def row_copy(dref, slot, p):
        src = ys_ref.at[pl.ds(pl.multiple_of(dref[0, 0, p] * pitch, pitch), pitch), :]
        dst = buf_ref.at[pl.ds(pl.multiple_of((slot * pairs + p) * bpitch, SUBLANES), pitch), :]
        return pltpu.make_async_copy(src, dst, sem.at[slot])

    def wait_slot(slot):
        for _ in range(TOP_K):
            pltpu.make_async_copy(ys_ref.at[pl.ds(0, tm * pitch), :], buf_ref.at[pl.ds(0, tm * pitch), :],
                                  sem.at[slot]).wait()

    def issue_tile(dref, slot):
        def body(g, c):
            for u in range(DMA_UNROLL):
                row_copy(dref, slot, g * DMA_UNROLL + u).start(priority=u % 2)
            return c

        lax.fori_loop(0, pairs // DMA_UNROLL, body, 0)

    @pl.when(t == 0)
    def _():
        issue_tile(dfirst_ref, 0)

    @pl.when(t + 1 < pl.num_programs(0))
    def _():
        issue_tile(dnext_ref, (t + 1) % 2)

    base_ref[...] = x_ref[...] + jnp.dot(t_ref[...], wsd_ref[...], preferred_element_type=F32)

    cur = t % 2
    wait_slot(cur)

    for grp in range(tm // COMBINE_GROUP):
        r0 = grp * COMBINE_GROUP
        rows = slice(r0, r0 + COMBINE_GROUP)
        gates = [jnp.broadcast_to(gate_ref[2 * r0:2 * (r0 + COMBINE_GROUP), k:k + 1], (2 * COMBINE_GROUP, LANES))
                 for k in range(TOP_K)]
        for s in range(pitch):
            acc = None
            for k in range(TOP_K):
                term = gates[k] * _load_pairs(buf_ref, cur * pairs + k * tm + r0, COMBINE_GROUP, bpitch, s)
                acc = term if acc is None else acc + term
            even, odd = _split_pairs(acc, scr_ref, grp * pitch + s)
            c0 = slice((2 * s) * LANES, (2 * s + 1) * LANES)
            c1 = slice((2 * s + 1) * LANES, (2 * s + 2) * LANES)
            o_ref[rows, c0] = base_ref[rows, c0] + even
            o_ref[rows, c1] = base_ref[rows, c1] + odd


def _combine(dest, gate_rows, x, t_shared, w_sh_down, ys, pitch, tm=128):
    n, d = x.shape
    ff = t_shared.shape[1]
    tiles = n // tm
    dest_t = dest.reshape(TOP_K, tiles, tm).transpose(1, 0, 2).reshape(tiles, 1, TOP_K * tm)
    dspec = lambda imap: pl.BlockSpec((1, 1, TOP_K * tm), imap, memory_space=pltpu.SMEM)
    bpitch = pitch if (pitch // SUBLANES) % 2 else pitch + SUBLANES
    return pl.pallas_call(
        functools.partial(_combine_kernel, tm=tm, pitch=pitch),
        grid=(tiles,),
        in_specs=[dspec(lambda i: (0, 0, 0)),
                  dspec(lambda i: (jnp.minimum(i + 1, tiles - 1), 0, 0)),
                  pl.BlockSpec((2 * tm, TOP_K), lambda i: (i, 0)),
                  pl.BlockSpec((tm, d), lambda i: (i, 0)),
                  pl.BlockSpec((tm, ff), lambda i: (i, 0)),
                  pl.BlockSpec((ff, d), lambda i: (0, 0)),
                  pl.BlockSpec(memory_space=pl.ANY)],
        out_specs=pl.BlockSpec((tm, d), lambda i: (i, 0)),
        out_shape=jax.ShapeDtypeStruct((n, d), F32),
        scratch_shapes=[pltpu.VMEM((2 * TOP_K * tm * bpitch, LANES), U32),
                        pltpu.VMEM((2 * tm * pitch, LANES), F32),
                        pltpu.VMEM((tm, d), F32),
                        pltpu.SemaphoreType.DMA((2,))],
        compiler_params=_cp(("arbitrary",)),
        name="moe_combine",
    )(dest_t, dest_t, gate_rows, x, t_shared, w_sh_down.astype(BF16), ys)


def _rope_tables(positions):
    half = ROPE_DIM // 2
    inv_freq = ROPE_THETA ** (-jnp.arange(0, ROPE_DIM, 2, dtype=F32) / ROPE_DIM)
    lane = np.arange(LANES)
    ang = positions.reshape(-1).astype(F32)[:, None] * jnp.tile(inv_freq, LANES // half)[None, :]
    cos, sin = jnp.cos(ang), jnp.sin(ang)
    cos_t = cos * (lane < ROPE_DIM).astype(np.float32)
    sin_a = sin * np.where(lane < half, -1.0, 0.0).astype(np.float32)
    sin_b = sin * ((lane >= half) & (lane < ROPE_DIM)).astype(np.float32)
    return cos_t, sin_a, sin_b


def _pad_lanes(v, width):
    return jnp.concatenate([v, jnp.zeros((width - v.shape[0],), v.dtype)])


def kernel(x, mem, positions, g_mix, w_in, g_qa, g_ka, rel_bias, g_cq, w_uq, g_ckv, w_ukv, g_qb, g_kb, w_o, g_cross, g_mem, w_xq, w_xkv, g_qx, g_kx, w_xo, g_ffn, w_router, router_bias, w_sh_gate, w_sh_up, w_sh_down, w_ex_gate, w_ex_up, w_ex_down):
    batch, seq, d = x.shape
    n = batch * seq
    mem_tokens = mem.shape[1]
    width_a = HEADS_A * HEAD_DIM
    x2d = x.reshape(n, d)

    h = _norm_bf16(x2d, g_mix)
    gs_qk = jnp.concatenate([jnp.tile(g_qa * HEAD_DIM ** -0.5, HEADS_A), jnp.tile(g_ka, HEADS_A)])
    w_in_t = w_in.T
    qk, v_a, c = _in_proj_all(h, w_in_t, gs_qk, 2 * width_a, width_a, Q_LORA + KV_LORA)
    w_kpe = jnp.pad(w_in_t[3 * width_a + Q_LORA + KV_LORA:], ((0, LANES - ROPE_DIM), (0, 0)))
    kpe = _proj_t(h, w_kpe, "in_proj_kpe")

    o_a = _attn_a(qk, v_a, _band_bias(rel_bias), batch, seq)

    cos_t, sin_a, sin_b = _rope_tables(positions)
    w_uq_pad = jnp.pad(w_uq.reshape(Q_LORA, HEADS_B, QK_B),
                       ((0, 0), (0, 0), (0, 2 * LANES - QK_B))).reshape(Q_LORA, HEADS_B * 2 * LANES)
    gq = _pad_lanes(g_qb, 2 * LANES).reshape(1, 2 * LANES)
    gk = _pad_lanes(g_kb, 2 * LANES).reshape(1, 2 * LANES)
    qf = _mla_q(c, g_cq, w_uq_pad, gq, cos_t, sin_a, sin_b)
    kf, v_b = _mla_kv(c, g_ckv, kpe, w_ukv, gk, cos_t, sin_a, sin_b)
    o_b = _attn_b(qf, kf, v_b, batch, seq)

    x1 = _out_proj(o_a, o_b, w_o, x2d)

    hw = MEM_HEADS * MEM_HEAD_DIM
    kvx = _normed_proj(mem.reshape(batch * mem_tokens, d), g_mem, w_xkv,
                       jnp.concatenate([jnp.tile(g_kx, MEM_HEADS), jnp.ones((hw,), F32)]), hw, "cross_kv")
    x2 = _cross_attn(x1, g_cross, w_xq, jnp.tile(g_qx * MEM_HEAD_DIM ** -0.5, MEM_HEADS), kvx, w_xo,
                     batch, seq, mem_tokens)

    hp, eid, gate, rank, cnt, t_shared = _router(x2, g_ffn, w_router, router_bias, w_sh_gate, w_sh_up)
    counts = cnt[:, 0]
    nstep = (counts + STEP_ROWS - 1) // STEP_ROWS
    step_end = jnp.cumsum(nstep).astype(I32)
    step_start = step_end - nstep
    n_steps = n * TOP_K // STEP_ROWS + N_EXPERTS
    experts = jnp.arange(N_EXPERTS, dtype=I32)
    start_of = jnp.sum(jnp.where(eid[:, :, None] == experts, step_start, 0), axis=-1)
    dest = start_of * STEP_ROWS + rank
    steps = jnp.arange(n_steps, dtype=I32)
    step_e = jnp.minimum(jnp.sum((step_end[None, :] <= steps[:, None]).astype(I32), axis=1), N_EXPERTS - 1)
    mine = step_e[:, None] == experts[None, :]
    rows_left = jnp.sum(jnp.where(mine, counts[None, :] - STEP_ROWS * (steps[:, None] - step_start[None, :]), 0),
                        axis=1)
    used = steps < step_end[-1]
    rows_here = jnp.where(used, jnp.clip(rows_left, 0, STEP_ROWS), 0)
    nsub = (rows_here + EXPERT_ROWS - 1) // EXPERT_ROWS
    nonempty = counts > 0
    ring_slot = (jnp.cumsum(nonempty.astype(I32)) - 1) % 2
    later = (experts[None, :] > experts[:, None]) & nonempty[None, :]
    succ = jnp.min(jnp.where(later, experts[None, :], N_EXPERTS), axis=1)
    succ = jnp.where(succ == N_EXPERTS, -1, succ)
    per_step = lambda v: jnp.sum(jnp.where(mine, v[None, :], 0), axis=1)
    first = (used & (steps == per_step(step_start))).astype(I32)
    sched = jnp.stack([step_e, nsub, first, per_step(ring_slot), per_step(succ)]).astype(I32)
    n_used = step_end[-1:]
    pitch = d // 2 // LANES

    xs = _dispatch(hp, dest, counts, step_start, n_steps * STEP_ROWS, pitch)
    hs = _expert_up(xs, w_ex_gate, w_ex_up, sched, n_used, n_steps, pitch)
    ys = _expert_down(hs, w_ex_down, sched, n_used, n_steps, pitch)

    out = _combine(dest, jnp.repeat(gate.T, 2, axis=0), x2, t_shared, w_sh_down, ys, pitch)
    return out.reshape(batch, seq, d)
```

```python
import functools

import jax
import jax.numpy as jnp
import numpy as np
from jax import lax
from jax.experimental import pallas as pl
from jax.experimental.pallas import tpu as pltpu

F32 = jnp.float32
BF16 = jnp.bfloat16
I32 = jnp.int32
U32 = jnp.uint32

CHUNK = 64
LEFT_CHUNKS = 8
REL_CLIP = 128
HEAD_DIM = 128
HEADS_A = 16
HEADS_B = 16
Q_LORA = 1024
KV_LORA = 512
NOPE_DIM = 128
ROPE_DIM = 64
V_DIM = 128
QK_B = NOPE_DIM + ROPE_DIM
ROPE_THETA = 10000.0
MEM_HEADS = 4
MEM_HEAD_DIM = 128
N_EXPERTS = 64
N_GROUPS = 8
TOPK_GROUPS = 4
TOP_K = 8
ROUTED_SCALE = 2.5
EPS = 1e-6

LANES = 128
SUBLANES = 8
ATT_BLOCK = 256
EXPERT_ROWS = 256
STEP_ROWS = 2 * EXPERT_ROWS
DMA_UNROLL = 8
COMBINE_GROUP = 16
NEG = -1e30
VMEM_LIMIT = 56 * 1024 * 1024

_NT = (((1,), (1,)), ((), ()))


def _cp(sem, vmem=VMEM_LIMIT):
    return pltpu.CompilerParams(dimension_semantics=sem, vmem_limit_bytes=vmem)


def _rms(x, g):
    return x * lax.rsqrt(jnp.mean(x * x, axis=-1, keepdims=True) + EPS) * g


def _pack_store(ref, first_token, x, scr):
    m, w = x.shape
    pitch = w // (2 * LANES)
    regions = scr.shape[0] // (2 * m)
    for s in range(pitch):
        base = (s % regions) * 2 * m
        scr[pl.ds(base, m, stride=2), :] = x[:, (2 * s) * LANES:(2 * s + 1) * LANES]
        scr[pl.ds(base + 1, m, stride=2), :] = x[:, (2 * s + 1) * LANES:(2 * s + 2) * LANES]
        z = scr[pl.ds(base, 2 * m), :].astype(BF16)
        ref[pl.ds(first_token * pitch + s, m, stride=pitch), :] = pltpu.bitcast(z, U32)


def _load_pairs(ref, first_token, m, pitch, s):
    w = ref[pl.ds(first_token * pitch + s, m, stride=pitch), :]
    return pltpu.bitcast(w, BF16).astype(F32)


def _split_pairs(z, scr, region):
    m = z.shape[0] // 2
    base = region * 2 * m
    scr[pl.ds(base, 2 * m), :] = z
    return scr[pl.ds(base, m, stride=2), :], scr[pl.ds(base + 1, m, stride=2), :]


def _load_unpacked(ref, first_token, m, pitch, scr, dtype=F32):
    regions = scr.shape[0] // (2 * m)
    cols = []
    for s in range(pitch):
        for piece in _split_pairs(_load_pairs(ref, first_token, m, pitch, s), scr, s % regions):
            cols.append(piece.astype(dtype))
    return jnp.concatenate(cols, axis=1)


def _norm_kernel(x_ref, g_ref, o_ref):
    o_ref[...] = _rms(x_ref[...], g_ref[...]).astype(o_ref.dtype)


def _norm_bf16(x, g, tm=256):
    n, d = x.shape
    return pl.pallas_call(
        _norm_kernel,
        grid=(n // tm,),
        in_specs=[pl.BlockSpec((tm, d), lambda i: (i, 0)), pl.BlockSpec((1, d), lambda i: (0, 0))],
        out_specs=pl.BlockSpec((tm, d), lambda i: (i, 0)),
        out_shape=jax.ShapeDtypeStruct((n, d), BF16),
        compiler_params=_cp(("parallel",)),
        name="norm_mix",
    )(x, g.reshape(1, d))


def _proj_t_kernel(h_ref, w_ref, o_ref):
    o_ref[...] = lax.dot_general(h_ref[...], w_ref[...].astype(BF16), _NT, preferred_element_type=F32)


def _proj_t(h, w_t, name, tm=1024):
    n, d = h.shape
    nc = w_t.shape[0]
    return pl.pallas_call(
        _proj_t_kernel,
        grid=(n // tm,),
        in_specs=[pl.BlockSpec((tm, d), lambda i: (i, 0)), pl.BlockSpec((nc, d), lambda i: (0, 0))],
        out_specs=pl.BlockSpec((tm, nc), lambda i: (i, 0)),
        out_shape=jax.ShapeDtypeStruct((n, nc), F32),
        compiler_params=_cp(("parallel",)),
        name=name,
    )(h, w_t)


def _in_proj_all_kernel(h_ref, w_ref, gs_ref, qk_ref, v_ref, c_ref, *, tn, n_qk, n_v):
    j = pl.program_id(1)
    acc = lax.dot_general(h_ref[...], w_ref[...].astype(BF16), _NT, preferred_element_type=F32)

    @pl.when(j < n_qk)
    def _():
        for c in range(tn // LANES):
            sl = slice(c * LANES, (c + 1) * LANES)
            qk_ref[:, sl] = _rms(acc[:, sl], gs_ref[:, sl]).astype(qk_ref.dtype)

    @pl.when((j >= n_qk) & (j < n_qk + n_v))
    def _():
        v_ref[...] = acc.astype(v_ref.dtype)

    @pl.when(j >= n_qk + n_v)
    def _():
        c_ref[...] = acc


def _in_proj_all(h, w_t, gs_qk, w_qk, w_v, w_c, tm=1024, tn=512):
    n, d = h.shape
    n_qk, n_v, n_c = w_qk // tn, w_v // tn, w_c // tn
    return pl.pallas_call(
        functools.partial(_in_proj_all_kernel, tn=tn, n_qk=n_qk, n_v=n_v),
        grid=(n // tm, n_qk + n_v + n_c),
        in_specs=[pl.BlockSpec((tm, d), lambda i, j: (i, 0)),
                  pl.BlockSpec((tn, d), lambda i, j: (j, 0)),
                  pl.BlockSpec((1, tn), lambda i, j: (0, jnp.minimum(j, n_qk - 1)))],
        out_specs=[pl.BlockSpec((tm, tn), lambda i, j: (i, jnp.minimum(j, n_qk - 1))),
                   pl.BlockSpec((tm, tn), lambda i, j: (i, jnp.clip(j - n_qk, 0, n_v - 1))),
                   pl.BlockSpec((tm, tn), lambda i, j: (i, jnp.clip(j - n_qk - n_v, 0, n_c - 1)))],
        out_shape=[jax.ShapeDtypeStruct((n, w_qk), BF16), jax.ShapeDtypeStruct((n, w_v), BF16),
                   jax.ShapeDtypeStruct((n, w_c), F32)],
        compiler_params=_cp(("parallel", "arbitrary")),
        name="in_proj",
    )(h, w_t, gs_qk.reshape(1, w_qk))


def _attn_a_kernel(q_ref, k0_ref, k1_ref, k2_ref, v0_ref, v1_ref, v2_ref, b_ref, o_ref, *, heads):
    i = pl.program_id(2)
    k_refs = (k0_ref, k1_ref, k2_ref)
    v_refs = (v0_ref, v1_ref, v2_ref)
    for hh in range(heads):
        sl = slice(hh * HEAD_DIM, (hh + 1) * HEAD_DIM)
        q = q_ref[:, sl]
        s = []
        for d in range(3):
            sd = lax.dot_general(q, k_refs[d][:, sl], _NT, preferred_element_type=F32)
            sd = sd + b_ref[hh, :, d * ATT_BLOCK:(d + 1) * ATT_BLOCK]
            if d > 0:
                sd = jnp.where(i >= d, sd, NEG)
            s.append(sd)
        m = jnp.maximum(jnp.maximum(s[0].max(-1, keepdims=True), s[1].max(-1, keepdims=True)),
                        s[2].max(-1, keepdims=True))
        l = jnp.zeros_like(m)
        o = jnp.zeros((q.shape[0], HEAD_DIM), F32)
        for d in range(3):
            p = jnp.exp(s[d] - m)
            l = l + p.sum(-1, keepdims=True)
            o = o + jnp.dot(p.astype(BF16), v_refs[d][:, sl], preferred_element_type=F32)
        o_ref[:, sl] = (o / l).astype(o_ref.dtype)


def _band_bias(rel_bias):
    blk = ATT_BLOCK
    period = 2 * blk + 1
    heads = rel_bias.shape[0]
    r = np.arange(blk)[:, None]
    c = np.arange(blk)[None, :]
    per_blk = blk // CHUNK
    k = np.arange(period)
    delta = np.where(k <= blk, k, k - period)
    tiles = []
    for d in range(3):
        idx = np.clip(blk * d - delta, -(CHUNK - 1), REL_CLIP) + CHUNK - 1
        w = rel_bias[:, idx].astype(F32)
        b = jnp.tile(w, (1, blk))[:, :blk * (period - 1)].reshape(heads, blk, period - 1)[:, :, :blk]
        cdiff = per_blk * d + r // CHUNK - c // CHUNK
        valid = (cdiff >= 0) & (cdiff <= LEFT_CHUNKS)
        tiles.append(jnp.where(valid[None], b, NEG))
    return jnp.concatenate(tiles, axis=-1)


def _attn_a(qk, v, bias, batch, seq, heads_per_step=8):
    n = qk.shape[0]
    nq = seq // ATT_BLOCK
    hw = heads_per_step * HEAD_DIM
    groups = HEADS_A // heads_per_step
    kcol0 = HEADS_A * HEAD_DIM // hw

    def kspec(d, col0):
        return pl.BlockSpec((ATT_BLOCK, hw), lambda g, b, i: (b * nq + jnp.maximum(i - d, 0), col0 + g))

    return pl.pallas_call(
        functools.partial(_attn_a_kernel, heads=heads_per_step),
        grid=(groups, batch, nq),
        in_specs=[pl.BlockSpec((ATT_BLOCK, hw), lambda g, b, i: (b * nq + i, g)),
                  kspec(0, kcol0), kspec(1, kcol0), kspec(2, kcol0),
                  kspec(0, 0), kspec(1, 0), kspec(2, 0),
                  pl.BlockSpec((heads_per_step, ATT_BLOCK, 3 * ATT_BLOCK), lambda g, b, i: (g, 0, 0))],
        out_specs=pl.BlockSpec((ATT_BLOCK, hw), lambda g, b, i: (b * nq + i, g)),
        out_shape=jax.ShapeDtypeStruct((n, HEADS_A * HEAD_DIM), BF16),
        compiler_params=_cp(("parallel", "parallel", "arbitrary")),
        name="attn_band",
    )(qk, qk, qk, qk, v, v, v, bias)


def _rope_pe(pe, g, cos_ref, sa_ref, sb_ref):
    ss = jnp.sum(pe * pe, axis=-1, keepdims=True) * (1.0 / ROPE_DIM)
    pn = pe * lax.rsqrt(ss + EPS) * g
    half = ROPE_DIM // 2
    return (pn * cos_ref[...] + pltpu.roll(pn, LANES - half, 1) * sa_ref[...]
            + pltpu.roll(pn, half, 1) * sb_ref[...])


def _mla_q_kernel(c_ref, gc_ref, w_ref, gq_ref, cos_ref, sa_ref, sb_ref, o_ref, xn_ref, *, heads, scale):
    @pl.when(pl.program_id(1) == 0)
    def _():
        xn_ref[...] = _rms(c_ref[...], gc_ref[...]).astype(BF16)

    acc = jnp.dot(xn_ref[...], w_ref[...].astype(BF16), preferred_element_type=F32)
    for hh in range(heads):
        base = hh * 2 * LANES
        nope = acc[:, base:base + LANES]
        pe = acc[:, base + LANES:base + 2 * LANES]
        o_ref[:, base:base + LANES] = (_rms(nope, gq_ref[:, :LANES]) * scale).astype(o_ref.dtype)
        o_ref[:, base + LANES:base + 2 * LANES] = (
            _rope_pe(pe, gq_ref[:, LANES:], cos_ref, sa_ref, sb_ref) * scale).astype(o_ref.dtype)


def _mla_q(c, g_cq, w_uq_pad, gq, cos_t, sin_a, sin_b, tm=512, heads_per_step=8):
    n = c.shape[0]
    tn = heads_per_step * 2 * LANES
    ncols = w_uq_pad.shape[1]
    row = lambda i, j: (i, 0)
    return pl.pallas_call(
        functools.partial(_mla_q_kernel, heads=heads_per_step, scale=QK_B ** -0.5),
        grid=(n // tm, ncols // tn),
        in_specs=[pl.BlockSpec((tm, Q_LORA), row),
                  pl.BlockSpec((1, Q_LORA), lambda i, j: (0, 0)),
                  pl.BlockSpec((Q_LORA, tn), lambda i, j: (0, j)),
                  pl.BlockSpec((1, 2 * LANES), lambda i, j: (0, 0)),
                  pl.BlockSpec((tm, LANES), row), pl.BlockSpec((tm, LANES), row), pl.BlockSpec((tm, LANES), row)],
        out_specs=pl.BlockSpec((tm, tn), lambda i, j: (i, j)),
        out_shape=jax.ShapeDtypeStruct((n, ncols), BF16),
        scratch_shapes=[pltpu.VMEM((tm, Q_LORA), BF16)],
        compiler_params=_cp(("parallel", "arbitrary")),
        name="mla_q_proj",
    )(c, g_cq.reshape(1, Q_LORA), w_uq_pad, gq, cos_t, sin_a, sin_b)


def _mla_kv_kernel(c_ref, gc_ref, kpe_ref, w_ref, gk_ref, cos_ref, sa_ref, sb_ref, k_ref, v_ref,
                   xn_ref, pe_ref, *, heads):
    @pl.when(pl.program_id(1) == 0)
    def _():
        xn_ref[...] = _rms(c_ref[...], gc_ref[...]).astype(BF16)
        pe_ref[...] = _rope_pe(kpe_ref[...], gk_ref[:, LANES:], cos_ref, sa_ref, sb_ref).astype(BF16)

    acc = jnp.dot(xn_ref[...], w_ref[...].astype(BF16), preferred_element_type=F32)
    for hh in range(heads):
        base = hh * 2 * LANES
        k_ref[:, base:base + LANES] = _rms(acc[:, base:base + LANES], gk_ref[:, :LANES]).astype(k_ref.dtype)
        k_ref[:, base + LANES:base + 2 * LANES] = pe_ref[...]
        v_ref[:, hh * LANES:(hh + 1) * LANES] = acc[:, base + LANES:base + 2 * LANES].astype(v_ref.dtype)


def _mla_kv(c, g_ckv, kpe, w_ukv, gk, cos_t, sin_a, sin_b, tm=512, heads_per_step=8):
    n = c.shape[0]
    tn = heads_per_step * 2 * LANES
    ncols = w_ukv.shape[1]
    cblk = Q_LORA // KV_LORA
    row = lambda i, j: (i, 0)
    return pl.pallas_call(
        functools.partial(_mla_kv_kernel, heads=heads_per_step),
        grid=(n // tm, ncols // tn),
        in_specs=[pl.BlockSpec((tm, KV_LORA), lambda i, j: (i, cblk)),
                  pl.BlockSpec((1, KV_LORA), lambda i, j: (0, 0)),
                  pl.BlockSpec((tm, LANES), row),
                  pl.BlockSpec((KV_LORA, tn), lambda i, j: (0, j)),
                  pl.BlockSpec((1, 2 * LANES), lambda i, j: (0, 0)),
                  pl.BlockSpec((tm, LANES), row), pl.BlockSpec((tm, LANES), row), pl.BlockSpec((tm, LANES), row)],
        out_specs=[pl.BlockSpec((tm, tn), lambda i, j: (i, j)),
                   pl.BlockSpec((tm, tn // 2), lambda i, j: (i, j))],
        out_shape=[jax.ShapeDtypeStruct((n, ncols), BF16),
                   jax.ShapeDtypeStruct((n, ncols // 2), BF16)],
        scratch_shapes=[pltpu.VMEM((tm, KV_LORA), BF16), pltpu.VMEM((tm, LANES), BF16)],
        compiler_params=_cp(("parallel", "arbitrary")),
        name="mla_kv_proj",
    )(c, g_ckv.reshape(1, KV_LORA), kpe, w_ukv, gk, cos_t, sin_a, sin_b)


def _attn_b_kernel(q_ref, k_ref, v_ref, o_ref, *, heads):
    i = pl.program_id(2)
    tq = q_ref.shape[0]
    qw = 2 * LANES
    qs = [q_ref[:, h * qw:(h + 1) * qw] for h in range(heads)]

    def step(first_blk, width, carry, masked):
        start = pl.multiple_of(first_blk * ATT_BLOCK, ATT_BLOCK)
        if masked:
            per_blk = ATT_BLOCK // CHUNK
            r = i * per_blk + lax.broadcasted_iota(I32, (tq, width), 0) // CHUNK
            c = first_blk * per_blk + lax.broadcasted_iota(I32, (tq, width), 1) // CHUNK
            keep = c <= r
        out = []
        for h in range(heads):
            m, l, acc = carry[h]
            s = lax.dot_general(qs[h], k_ref[pl.ds(start, width), h * qw:(h + 1) * qw], _NT,
                                preferred_element_type=F32)
            if masked:
                s = jnp.where(keep, s, NEG)
            m_new = jnp.maximum(m, s.max(-1, keepdims=True))
            alpha = jnp.exp(m - m_new)
            p = jnp.exp(s - m_new)
            l = alpha * l + p.sum(-1, keepdims=True)
            acc = alpha * acc + jnp.dot(p.astype(BF16), v_ref[pl.ds(start, width), h * V_DIM:(h + 1) * V_DIM],
                                        preferred_element_type=F32)
            out.append((m_new, l, acc))
        return tuple(out)

    def finish(final):
        for h in range(heads):
            _, l, acc = final[h]
            o_ref[:, h * V_DIM:(h + 1) * V_DIM] = (acc / l).astype(o_ref.dtype)

    init = tuple((jnp.full((tq, 1), NEG, F32), jnp.zeros((tq, 1), F32), jnp.zeros((tq, V_DIM), F32))
                 for _ in range(heads))
    carry = lax.fori_loop(0, i // 2, lambda j, c: step(2 * j, 2 * ATT_BLOCK, c, False), init)

    @pl.when(i % 2 == 1)
    def _():
        finish(step(i - 1, 2 * ATT_BLOCK, carry, True))

    @pl.when(i % 2 == 0)
    def _():
        finish(step(i, ATT_BLOCK, carry, True))


def _attn_b(qf, kf, vb, batch, seq, heads_per_step=8):
    n = qf.shape[0]
    nq = seq // ATT_BLOCK
    qw = heads_per_step * 2 * LANES
    vw = heads_per_step * V_DIM
    return pl.pallas_call(
        functools.partial(_attn_b_kernel, heads=heads_per_step),
        grid=(batch, HEADS_B // heads_per_step, nq),
        in_specs=[pl.BlockSpec((ATT_BLOCK, qw), lambda b, g, i: (b * nq + i, g)),
                  pl.BlockSpec((seq, qw), lambda b, g, i: (b, g)),
                  pl.BlockSpec((seq, vw), lambda b, g, i: (b, g))],
        out_specs=pl.BlockSpec((ATT_BLOCK, vw), lambda b, g, i: (b * nq + i, g)),
        out_shape=jax.ShapeDtypeStruct((n, HEADS_B * V_DIM), BF16),
        compiler_params=_cp(("parallel", "parallel", "arbitrary")),
        name="attn_latent",
    )(qf, kf, vb)


def _out_proj_kernel(oa_ref, ob_ref, w_ref, x_ref, o_ref, wb_ref):
    @pl.when(pl.program_id(1) == 0)
    def _():
        wb_ref[...] = w_ref[...].astype(BF16)

    ka = oa_ref.shape[1]
    acc = jnp.dot(oa_ref[...], wb_ref[:ka, :], preferred_element_type=F32)
    acc = acc + jnp.dot(ob_ref[...], wb_ref[ka:, :], preferred_element_type=F32)
    o_ref[...] = x_ref[...] + acc


def _out_proj(oa, ob, w_o, x, tm=256, tn=1024):
    n, ka = oa.shape
    kb = ob.shape[1]
    d = w_o.shape[1]
    return pl.pallas_call(
        _out_proj_kernel,
        grid=(d // tn, n // tm),
        in_specs=[pl.BlockSpec((tm, ka), lambda j, i: (i, 0)),
                  pl.BlockSpec((tm, kb), lambda j, i: (i, 0)),
                  pl.BlockSpec((ka + kb, tn), lambda j, i: (0, j)),
                  pl.BlockSpec((tm, tn), lambda j, i: (i, j))],
        out_specs=pl.BlockSpec((tm, tn), lambda j, i: (i, j)),
        out_shape=jax.ShapeDtypeStruct((n, d), F32),
        scratch_shapes=[pltpu.VMEM((ka + kb, tn), BF16)],
        compiler_params=_cp(("parallel", "arbitrary")),
        name="out_proj",
    )(oa, ob, w_o, x)


def _normed_proj_kernel(x_ref, g_ref, w_ref, gs_ref, o_ref, wb_ref, *, norm_cols):
    @pl.when(pl.program_id(0) == 0)
    def _():
        wb_ref[...] = w_ref[...].astype(BF16)

    h = _rms(x_ref[...], g_ref[...]).astype(BF16)
    acc = jnp.dot(h, wb_ref[...], preferred_element_type=F32)
    for c in range(acc.shape[1] // LANES):
        sl = slice(c * LANES, (c + 1) * LANES)
        if c * LANES < norm_cols:
            o_ref[:, sl] = _rms(acc[:, sl], gs_ref[:, sl]).astype(o_ref.dtype)
        else:
            o_ref[:, sl] = acc[:, sl].astype(o_ref.dtype)


def _normed_proj(x, g, w, gs, norm_cols, name):
    nb, tm, d = x.shape
    n = nb * tm
    nc = w.shape[1]
    return pl.pallas_call(
        functools.partial(_normed_proj_kernel, norm_cols=norm_cols),
        grid=(nb,),
        in_specs=[pl.BlockSpec((None, tm, d), lambda i: (i, 0, 0)),
                  pl.BlockSpec((1, d), lambda i: (0, 0)),
                  pl.BlockSpec((d, nc), lambda i: (0, 0)),
                  pl.BlockSpec((1, nc), lambda i: (0, 0))],
        out_specs=pl.BlockSpec((tm, nc), lambda i: (i, 0)),
        out_shape=jax.ShapeDtypeStruct((n, nc), BF16),
        scratch_shapes=[pltpu.VMEM((d, nc), BF16)],
        compiler_params=_cp(("arbitrary",)),
        name=name,
    )(x, g.reshape(1, d), w, gs.reshape(1, nc))


def _cross_kernel(x_ref, g_ref, wq_ref, gq_ref, k_ref, v_ref, wo_ref, o_ref):
    x = x_ref[...]
    q = jnp.dot(_rms(x, g_ref[...]).astype(BF16), wq_ref[...], preferred_element_type=F32)
    outs = []
    for hh in range(MEM_HEADS):
        sl = slice(hh * MEM_HEAD_DIM, (hh + 1) * MEM_HEAD_DIM)
        qh = _rms(q[:, sl], gq_ref[:, sl]).astype(BF16)
        s = lax.dot_general(qh, k_ref[:, sl], _NT, preferred_element_type=F32)
        p = jnp.exp(s - s.max(-1, keepdims=True))
        o = jnp.dot(p.astype(BF16), v_ref[:, sl], preferred_element_type=F32)
        outs.append((o / p.sum(-1, keepdims=True)).astype(BF16))
    acc = x
    for hh in range(MEM_HEADS):
        sl = slice(hh * MEM_HEAD_DIM, (hh + 1) * MEM_HEAD_DIM)
        acc = acc + jnp.dot(outs[hh], wo_ref[sl, :], preferred_element_type=F32)
    o_ref[...] = acc


def _cross_attn(x, g, w_xq, gq, kvx, w_xo, batch, seq, mem_tokens, tm=256):
    n, d = x.shape
    per_b = seq // tm
    hw = MEM_HEADS * MEM_HEAD_DIM
    const = lambda b, i: (0, 0)
    return pl.pallas_call(
        _cross_kernel,
        grid=(batch, per_b),
        in_specs=[pl.BlockSpec((tm, d), lambda b, i: (b * per_b + i, 0)),
                  pl.BlockSpec((1, d), const),
                  pl.BlockSpec((d, hw), const),
                  pl.BlockSpec((1, hw), const),
                  pl.BlockSpec((mem_tokens, hw), lambda b, i: (b, 0)),
                  pl.BlockSpec((mem_tokens, hw), lambda b, i: (b, 1)),
                  pl.BlockSpec((hw, d), const)],
        out_specs=pl.BlockSpec((tm, d), lambda b, i: (b * per_b + i, 0)),
        out_shape=jax.ShapeDtypeStruct((n, d), F32),
        compiler_params=_cp(("parallel", "arbitrary")),
        name="cross_attn",
    )(x, g.reshape(1, d), w_xq.astype(BF16), gq.reshape(1, hw), kvx, kvx, w_xo.astype(BF16))


def _router_kernel(x_ref, g_ref, wr_ref, b_ref, tri_ref, wg_ref, wu_ref, hp_ref, eid_ref, gate_ref, rank_ref,
                   cnt_ref, ts_ref, run_ref, scr_ref):
    t = pl.program_id(0)

    @pl.when(t == 0)
    def _():
        run_ref[...] = jnp.zeros_like(run_ref)

    h = _rms(x_ref[...], g_ref[...])
    _pack_store(hp_ref, 0, h, scr_ref)
    hb = h.astype(BF16)
    ts_ref[...] = _silu_mul(jnp.dot(hb, wg_ref[...], preferred_element_type=F32),
                            jnp.dot(hb, wu_ref[...], preferred_element_type=F32)).astype(ts_ref.dtype)
    tm = h.shape[0]
    per_g = N_EXPERTS // N_GROUPS

    wr = wr_ref[...]
    wr_hi = wr.astype(BF16)
    wr_lo = (wr - wr_hi.astype(F32)).astype(BF16)
    h_lo = (h - hb.astype(F32)).astype(BF16)
    logits = (lax.dot_general(wr_hi, hb, _NT, preferred_element_type=F32)
              + lax.dot_general(wr_lo, hb, _NT, preferred_element_type=F32)
              + lax.dot_general(wr_hi, h_lo, _NT, preferred_element_type=F32))
    scores = 1.0 / (1.0 + jnp.exp(-logits))
    choice = scores + b_ref[...]

    sub = lax.broadcasted_iota(I32, (per_g, tm), 0).astype(F32)
    rows = []
    for g in range(N_GROUPS):
        c = choice[g * per_g:(g + 1) * per_g, :]
        m1 = c.max(0, keepdims=True)
        first = jnp.where(c == m1, sub, float(per_g)).min(0, keepdims=True)
        m2 = jnp.where(sub == first, -jnp.inf, c).max(0, keepdims=True)
        rows.append(m1 + m2)
    gs = jnp.concatenate(rows, axis=0)

    gsub = lax.broadcasted_iota(I32, (N_GROUPS, tm), 0).astype(F32)
    beaten = jnp.zeros((N_GROUPS, tm), F32)
    for g2 in range(N_GROUPS):
        row = gs[g2:g2 + 1, :]
        wins = (row > gs) | ((row == gs) & (gsub > float(g2)))
        beaten = beaten + jnp.where(wins, 1.0, 0.0)
    g_ok = jnp.where(beaten < TOPK_GROUPS, 1.0, 0.0)
    e_ok = jnp.concatenate(
        [jnp.broadcast_to(g_ok[g:g + 1, :], (per_g, tm)) for g in range(N_GROUPS)], axis=0)

    eiota = lax.broadcasted_iota(I32, (N_EXPERTS, tm), 0).astype(F32)
    masked = jnp.where(e_ok > 0.5, choice, -jnp.inf)
    chosen = jnp.zeros((N_EXPERTS, tm), F32)
    eids, ws = [], []
    for _ in range(TOP_K):
        m = masked.max(0, keepdims=True)
        idx = jnp.where(masked == m, eiota, float(N_EXPERTS)).min(0, keepdims=True)
        sel = eiota == idx
        ws.append(jnp.where(sel, scores, 0.0).sum(0, keepdims=True))
        masked = jnp.where(sel, -jnp.inf, masked)
        chosen = jnp.where(sel, 1.0, chosen)
        eids.append(idx)
    wsum = ws[0]
    for w in ws[1:]:
        wsum = wsum + w
    denom = wsum + 1e-20

    pos = jnp.dot(chosen.astype(BF16), tri_ref[...], preferred_element_type=F32) + run_ref[:, 0:1]
    run_ref[...] = run_ref[...] + chosen.sum(1, keepdims=True)
    cnt_ref[...] = run_ref[...].astype(I32)

    for r in range(TOP_K):
        eid_ref[r:r + 1, :] = eids[r].astype(I32)
        gate_ref[r:r + 1, :] = ws[r] / denom * ROUTED_SCALE
        rank_ref[r:r + 1, :] = jnp.where(eiota == eids[r], pos, 0.0).sum(0, keepdims=True).astype(I32)


def _router(x, g, w_router, router_bias, w_sh_gate, w_sh_up, tm=256):
    n, d = x.shape
    ne = w_router.shape[1]
    ff = w_sh_gate.shape[1]
    tri = (jnp.arange(tm)[:, None] < jnp.arange(tm)[None, :]).astype(BF16)
    row8 = lambda i: (0, i)
    pitch = d // 2 // LANES
    return pl.pallas_call(
        _router_kernel,
        grid=(n // tm,),
        in_specs=[pl.BlockSpec((tm, d), lambda i: (i, 0)),
                  pl.BlockSpec((1, d), lambda i: (0, 0)),
                  pl.BlockSpec((ne, d), lambda i: (0, 0)),
                  pl.BlockSpec((ne, 1), lambda i: (0, 0)),
                  pl.BlockSpec((tm, tm), lambda i: (0, 0)),
                  pl.BlockSpec((d, ff), lambda i: (0, 0)), pl.BlockSpec((d, ff), lambda i: (0, 0))],
        out_specs=[pl.BlockSpec((tm * pitch, LANES), lambda i: (i, 0)),
                   pl.BlockSpec((TOP_K, tm), row8), pl.BlockSpec((TOP_K, tm), row8),
                   pl.BlockSpec((TOP_K, tm), row8),
                   pl.BlockSpec((ne, LANES), lambda i: (0, 0)),
                   pl.BlockSpec((tm, ff), lambda i: (i, 0))],
        out_shape=[jax.ShapeDtypeStruct((n * pitch, LANES), U32),
                   jax.ShapeDtypeStruct((TOP_K, n), I32), jax.ShapeDtypeStruct((TOP_K, n), F32),
                   jax.ShapeDtypeStruct((TOP_K, n), I32),
                   jax.ShapeDtypeStruct((ne, LANES), I32),
                   jax.ShapeDtypeStruct((n, ff), BF16)],
        scratch_shapes=[pltpu.VMEM((ne, LANES), F32), pltpu.VMEM((2 * tm * pitch, LANES), F32)],
        compiler_params=_cp(("arbitrary",)),
        name="router",
    )(x, g.reshape(1, d), w_router.T, router_bias.reshape(ne, 1), tri,
      w_sh_gate.astype(BF16), w_sh_up.astype(BF16))


def _dispatch_kernel(cnt_ref, start_ref, dest_ref, hp_ref, xs_ref, sem, *, tm, pitch):
    t = pl.program_id(0)
    pairs = TOP_K * tm

    def row_copy(r, slot):
        src = hp_ref.at[pl.ds(pl.multiple_of(r * pitch, pitch), pitch), :]
        dst = xs_ref.at[pl.ds(pl.multiple_of(slot * pitch, pitch), pitch), :]
        return pltpu.make_async_copy(src, dst, sem)

    def issue(g, c):
        for u in range(DMA_UNROLL):
            p = g * DMA_UNROLL + u
            row_copy(p & (tm - 1), dest_ref[0, 0, p]).start(priority=u % 2)
        return c

    lax.fori_loop(0, pairs // DMA_UNROLL, issue, 0)
    for _ in range(TOP_K):
        pltpu.make_async_copy(hp_ref, xs_ref.at[pl.ds(0, tm * pitch), :], sem).wait()

    @pl.when(t == pl.num_programs(0) - 1)
    def _():
        def run_copy(first, rows):
            dst = xs_ref.at[pl.ds(pl.multiple_of(first * pitch, pitch), rows * pitch), :]
            return pltpu.make_async_copy(hp_ref.at[pl.ds(0, rows * pitch), :], dst, sem)

        def per_expert(e, c):
            used = cnt_ref[e]
            npad = (-used) & (EXPERT_ROWS - 1)
            first = start_ref[e] * STEP_ROWS + used
            sizes = [1 << k for k in range(EXPERT_ROWS.bit_length() - 1)]
            offs = []
            off = first
            for size in sizes:
                offs.append(off)
                off = off + (npad & size)
            for size, o in zip(sizes, offs):
                pl.when((npad & size) != 0)(lambda size=size, o=o: run_copy(o, size).start())
            for size in sizes:
                pl.when((npad & size) != 0)(lambda size=size: run_copy(0, size).wait())
            return c

        lax.fori_loop(0, N_EXPERTS, per_expert, 0)


def _dispatch(hp, dest, counts, step_start, p_rows, pitch, tm=512):
    n = hp.shape[0] // pitch
    tiles = n // tm
    dest_t = dest.reshape(TOP_K, tiles, tm).transpose(1, 0, 2).reshape(tiles, 1, TOP_K * tm)
    grid_spec = pltpu.PrefetchScalarGridSpec(
        num_scalar_prefetch=2,
        grid=(tiles,),
        in_specs=[pl.BlockSpec((1, 1, TOP_K * tm), lambda i, *_: (i, 0, 0), memory_space=pltpu.SMEM),
                  pl.BlockSpec((tm * pitch, LANES), lambda i, *_: (i, 0))],
        out_specs=pl.BlockSpec(memory_space=pl.ANY),
        scratch_shapes=[pltpu.SemaphoreType.DMA(())],
    )
    return pl.pallas_call(
        functools.partial(_dispatch_kernel, tm=tm, pitch=pitch),
        grid_spec=grid_spec,
        out_shape=jax.ShapeDtypeStruct((p_rows * pitch, LANES), U32),
        compiler_params=_cp(("arbitrary",)),
        name="moe_dispatch",
    )(counts, step_start, dest_t, hp)


def _silu_mul(g, u):
    return g / (1.0 + jnp.exp(-g)) * u


def _expert_weights(sched_ref, w_hbm_refs, wbuf_ref, sem):
    b = pl.program_id(0)

    def copies(e, slot):
        return [pltpu.make_async_copy(w.at[e], wbuf_ref.at[slot, i], sem.at[slot])
                for i, w in enumerate(w_hbm_refs)]

    @pl.when(b == 0)
    def _():
        for c in copies(sched_ref[0, 0], 0):
            c.start(priority=1)

    slot = sched_ref[3, b]

    @pl.when(sched_ref[2, b] == 1)
    def _():
        for c in copies(0, slot):
            c.wait()
        nxt = sched_ref[4, b]

        @pl.when(nxt >= 0)
        def _():
            for c in copies(nxt, 1 - slot):
                c.start(priority=1)

    return slot


def _expert_up_kernel(sched_ref, nu_ref, xs_ref, wg_hbm, wu_hbm, o_ref, wbuf_ref, scr_ref, sem):
    b = pl.program_id(0)
    pitch = xs_ref.shape[0] // STEP_ROWS
    slot = _expert_weights(sched_ref, (wg_hbm, wu_hbm), wbuf_ref, sem)

    def sub_block(j):
        x = _load_unpacked(xs_ref, j * EXPERT_ROWS, EXPERT_ROWS, pitch, scr_ref)
        g = jnp.dot(x, wbuf_ref[slot, 0], preferred_element_type=F32)
        u = jnp.dot(x, wbuf_ref[slot, 1], preferred_element_type=F32)
        o_ref[j * EXPERT_ROWS:(j + 1) * EXPERT_ROWS, :] = _silu_mul(g, u).astype(o_ref.dtype)

    for j in range(STEP_ROWS // EXPERT_ROWS):
        pl.when(sched_ref[1, b] > j)(functools.partial(sub_block, j))


def _row_block(b, sched, nu):
    return (jnp.minimum(b, nu[0] - 1), 0)


def _expert_up(xs, w_gate, w_up, sched, n_used, n_steps, pitch):
    _, d, ff = w_gate.shape
    grid_spec = pltpu.PrefetchScalarGridSpec(
        num_scalar_prefetch=2,
        grid=(n_steps,),
        in_specs=[pl.BlockSpec((STEP_ROWS * pitch, LANES), _row_block),
                  pl.BlockSpec(memory_space=pl.ANY), pl.BlockSpec(memory_space=pl.ANY)],
        out_specs=pl.BlockSpec((STEP_ROWS, ff), _row_block),
        scratch_shapes=[pltpu.VMEM((2, 2, d, ff), F32),
                        pltpu.VMEM((2 * EXPERT_ROWS * pitch, LANES), F32),
                        pltpu.SemaphoreType.DMA((2,))],
    )
    return pl.pallas_call(
        _expert_up_kernel,
        grid_spec=grid_spec,
        out_shape=jax.ShapeDtypeStruct((n_steps * STEP_ROWS, ff), BF16),
        compiler_params=_cp(("arbitrary",)),
        name="expert_up",
    )(sched, n_used, xs, w_gate, w_up)


def _expert_down_kernel(sched_ref, nu_ref, h_ref, wd_hbm, o_ref, wbuf_ref, scr_ref, sem):
    b = pl.program_id(0)
    slot = _expert_weights(sched_ref, (wd_hbm,), wbuf_ref, sem)

    def sub_block(j):
        y = jnp.dot(h_ref[j * EXPERT_ROWS:(j + 1) * EXPERT_ROWS, :], wbuf_ref[slot, 0].astype(BF16),
                    preferred_element_type=F32)
        _pack_store(o_ref, j * EXPERT_ROWS, y, scr_ref)

    for j in range(STEP_ROWS // EXPERT_ROWS):
        pl.when(sched_ref[1, b] > j)(functools.partial(sub_block, j))


def _expert_down(hs, w_down, sched, n_used, n_steps, pitch):
    p_rows, ff = hs.shape
    d = w_down.shape[2]
    grid_spec = pltpu.PrefetchScalarGridSpec(
        num_scalar_prefetch=2,
        grid=(n_steps,),
        in_specs=[pl.BlockSpec((STEP_ROWS, ff), _row_block), pl.BlockSpec(memory_space=pl.ANY)],
        out_specs=pl.BlockSpec((STEP_ROWS * pitch, LANES), _row_block),
        scratch_shapes=[pltpu.VMEM((2, 1, ff, d), F32),
                        pltpu.VMEM((2 * EXPERT_ROWS * pitch, LANES), F32),
                        pltpu.SemaphoreType.DMA((2,))],
    )
    return pl.pallas_call(
        _expert_down_kernel,
        grid_spec=grid_spec,
        out_shape=jax.ShapeDtypeStruct((p_rows * pitch, LANES), U32),
        compiler_params=_cp(("arbitrary",)),
        name="expert_down",
    )(sched, n_used, hs, w_down)


def _combine_kernel(dfirst_ref, dnext_ref, gate_ref, x_ref, t_ref, wsd_ref, ys_ref, o_ref, buf_ref, scr_ref,
                    base_ref, sem, *, tm, pitch):
    t = pl.program_id(0)
    pairs = TOP_K * tm
    bpitch = buf_ref.shape[0] // (2 * pairs)

    def row_copy(dref, slot, p):
        src = ys_ref.at[pl.ds(pl.multiple_of(dref[0, 0, p] * pitch, pitch), pitch), :]
        dst = buf_ref.at[pl.ds(pl.multiple_of((slot * pairs + p) * bpitch, SUBLANES), pitch), :]
        return pltpu.make_async_copy(src, dst, sem.at[slot])

    def wait_slot(slot):
        for _ in range(TOP_K):
            pltpu.make_async_copy(ys_ref.at[pl.ds(0, tm * pitch), :], buf_ref.at[pl.ds(0, tm * pitch), :],
                                  sem.at[slot]).wait()

    def issue_tile(dref, slot):
        def body(g, c):
            for u in range(DMA_UNROLL):
                row_copy(dref, slot, g * DMA_UNROLL + u).start(priority=u % 2)
            return c

        lax.fori_loop(0, pairs // DMA_UNROLL, body, 0)

    @pl.when(t == 0)
    def _():
        issue_tile(dfirst_ref, 0)

    @pl.when(t + 1 < pl.num_programs(0))
    def _():
        issue_tile(dnext_ref, (t + 1) % 2)

    base_ref[...] = x_ref[...] + jnp.dot(t_ref[...], wsd_ref[...], preferred_element_type=F32)

    cur = t % 2
    wait_slot(cur)

    for grp in range(tm // COMBINE_GROUP):
        r0 = grp * COMBINE_GROUP
        rows = slice(r0, r0 + COMBINE_GROUP)
        gates = [jnp.broadcast_to(gate_ref[2 * r0:2 * (r0 + COMBINE_GROUP), k:k + 1], (2 * COMBINE_GROUP, LANES))
                 for k in range(TOP_K)]
        for s in range(pitch):
            acc = None
            for k in range(TOP_K):
                term = gates[k] * _load_pairs(buf_ref, cur * pairs + k * tm + r0, COMBINE_GROUP, bpitch, s)
                acc = term if acc is None else acc + term
            even, odd = _split_pairs(acc, scr_ref, grp * pitch + s)
            c0 = slice((2 * s) * LANES, (2 * s + 1) * LANES)
            c1 = slice((2 * s + 1) * LANES, (2 * s + 2) * LANES)
            o_ref[rows, c0] = base_ref[rows, c0] + even
            o_ref[rows, c1] = base_ref[rows, c1] + odd


def _combine(dest, gate_rows, x, t_shared, w_sh_down, ys, pitch, tm=128):
    n, d = x.shape
    ff = t_shared.shape[1]
    tiles = n // tm
    dest_t = dest.reshape(TOP_K, tiles, tm).transpose(1, 0, 2).reshape(tiles, 1, TOP_K * tm)
    dspec = lambda imap: pl.BlockSpec((1, 1, TOP_K * tm), imap, memory_space=pltpu.SMEM)
    bpitch = pitch if (pitch // SUBLANES) % 2 else pitch + SUBLANES
    return pl.pallas_call(
        functools.partial(_combine_kernel, tm=tm, pitch=pitch),
        grid=(tiles,),
        in_specs=[dspec(lambda i: (0, 0, 0)),
                  dspec(lambda i: (jnp.minimum(i + 1, tiles - 1), 0, 0)),
                  pl.BlockSpec((2 * tm, TOP_K), lambda i: (i, 0)),
                  pl.BlockSpec((tm, d), lambda i: (i, 0)),
                  pl.BlockSpec((tm, ff), lambda i: (i, 0)),
                  pl.BlockSpec((ff, d), lambda i: (0, 0)),
                  pl.BlockSpec(memory_space=pl.ANY)],
        out_specs=pl.BlockSpec((tm, d), lambda i: (i, 0)),
        out_shape=jax.ShapeDtypeStruct((n, d), F32),
        scratch_shapes=[pltpu.VMEM((2 * TOP_K * tm * bpitch, LANES), U32),
                        pltpu.VMEM((2 * tm * pitch, LANES), F32),
                        pltpu.VMEM((tm, d), F32),
                        pltpu.SemaphoreType.DMA((2,))],
        compiler_params=_cp(("arbitrary",)),
        name="moe_combine",
    )(dest_t, dest_t, gate_rows, x, t_shared, w_sh_down.astype(BF16), ys)


def _rope_tables(positions):
    half = ROPE_DIM // 2
    inv_freq = ROPE_THETA ** (-jnp.arange(0, ROPE_DIM, 2, dtype=F32) / ROPE_DIM)
    lane = np.arange(LANES)
    ang = positions.reshape(-1).astype(F32)[:, None] * jnp.tile(inv_freq, LANES // half)[None, :]
    cos, sin = jnp.cos(ang), jnp.sin(ang)
    cos_t = cos * (lane < ROPE_DIM).astype(np.float32)
    sin_a = sin * np.where(lane < half, -1.0, 0.0).astype(np.float32)
    sin_b = sin * ((lane >= half) & (lane < ROPE_DIM)).astype(np.float32)
    return cos_t, sin_a, sin_b


def _pad_lanes(v, width):
    return jnp.concatenate([v, jnp.zeros((width - v.shape[0],), v.dtype)])


def kernel(x, mem, positions, g_mix, w_in, g_qa, g_ka, rel_bias, g_cq, w_uq, g_ckv, w_ukv, g_qb, g_kb, w_o, g_cross, g_mem, w_xq, w_xkv, g_qx, g_kx, w_xo, g_ffn, w_router, router_bias, w_sh_gate, w_sh_up, w_sh_down, w_ex_gate, w_ex_up, w_ex_down):
    batch, seq, d = x.shape
    n = batch * seq
    mem_tokens = mem.shape[1]
    width_a = HEADS_A * HEAD_DIM
    x2d = x.reshape(n, d)

    h = _norm_bf16(x2d, g_mix)
    gs_qk = jnp.concatenate([jnp.tile(g_qa * HEAD_DIM ** -0.5, HEADS_A), jnp.tile(g_ka, HEADS_A)])
    w_in_t = w_in.T
    qk, v_a, c = _in_proj_all(h, w_in_t, gs_qk, 2 * width_a, width_a, Q_LORA + KV_LORA)
    w_kpe = jnp.pad(w_in_t[3 * width_a + Q_LORA + KV_LORA:], ((0, LANES - ROPE_DIM), (0, 0)))
    kpe = _proj_t(h, w_kpe, "in_proj_kpe")

    o_a = _attn_a(qk, v_a, _band_bias(rel_bias), batch, seq)

    cos_t, sin_a, sin_b = _rope_tables(positions)
    w_uq_pad = jnp.pad(w_uq.reshape(Q_LORA, HEADS_B, QK_B),
                       ((0, 0), (0, 0), (0, 2 * LANES - QK_B))).reshape(Q_LORA, HEADS_B * 2 * LANES)
    gq = _pad_lanes(g_qb, 2 * LANES).reshape(1, 2 * LANES)
    gk = _pad_lanes(g_kb, 2 * LANES).reshape(1, 2 * LANES)
    qf = _mla_q(c, g_cq, w_uq_pad, gq, cos_t, sin_a, sin_b)
    kf, v_b = _mla_kv(c, g_ckv, kpe, w_ukv, gk, cos_t, sin_a, sin_b)
    o_b = _attn_b(qf, kf, v_b, batch, seq)

    x1 = _out_proj(o_a, o_b, w_o, x2d)

    hw = MEM_HEADS * MEM_HEAD_DIM
    kvx = _normed_proj(mem, g_mem, w_xkv,
                       jnp.concatenate([jnp.tile(g_kx, MEM_HEADS), jnp.ones((hw,), F32)]), hw, "cross_kv")
    x2 = _cross_attn(x1, g_cross, w_xq, jnp.tile(g_qx * MEM_HEAD_DIM ** -0.5, MEM_HEADS), kvx, w_xo,
                     batch, seq, mem_tokens)

    hp, eid, gate, rank, cnt, t_shared = _router(x2, g_ffn, w_router, router_bias, w_sh_gate, w_sh_up)
    counts = cnt[:, 0]
    nstep = (counts + STEP_ROWS - 1) // STEP_ROWS
    step_end = jnp.cumsum(nstep).astype(I32)
    step_start = step_end - nstep
    n_steps = n * TOP_K // STEP_ROWS + N_EXPERTS
    experts = jnp.arange(N_EXPERTS, dtype=I32)
    start_of = jnp.sum(jnp.where(eid[:, :, None] == experts, step_start, 0), axis=-1)
    dest = start_of * STEP_ROWS + rank
    steps = jnp.arange(n_steps, dtype=I32)
    step_e = jnp.minimum(jnp.sum((step_end[None, :] <= steps[:, None]).astype(I32), axis=1), N_EXPERTS - 1)
    mine = step_e[:, None] == experts[None, :]
    rows_left = jnp.sum(jnp.where(mine, counts[None, :] - STEP_ROWS * (steps[:, None] - step_start[None, :]), 0),
                        axis=1)
    used = steps < step_end[-1]
    rows_here = jnp.where(used, jnp.clip(rows_left, 0, STEP_ROWS), 0)
    nsub = (rows_here + EXPERT_ROWS - 1) // EXPERT_ROWS
    nonempty = counts > 0
    ring_slot = (jnp.cumsum(nonempty.astype(I32)) - 1) % 2
    later = (experts[None, :] > experts[:, None]) & nonempty[None, :]
    succ = jnp.min(jnp.where(later, experts[None, :], N_EXPERTS), axis=1)
    succ = jnp.where(succ == N_EXPERTS, -1, succ)
    per_step = lambda v: jnp.sum(jnp.where(mine, v[None, :], 0), axis=1)
    first = (used & (steps == per_step(step_start))).astype(I32)
    sched = jnp.stack([step_e, nsub, first, per_step(ring_slot), per_step(succ)]).astype(I32)
    n_used = step_end[-1:]
    pitch = d // 2 // LANES

    xs = _dispatch(hp, dest, counts, step_start, n_steps * STEP_ROWS, pitch)
    hs = _expert_up(xs, w_ex_gate, w_ex_up, sched, n_used, n_steps, pitch)
    ys = _expert_down(hs, w_ex_down, sched, n_used, n_steps, pitch)

    out = _combine(dest, jnp.repeat(gate.T, 2, axis=0), x2, t_shared, w_sh_down, ys, pitch)
    return out.reshape(batch, seq, d)
```

```python
import functools

import jax
import jax.numpy as jnp
import numpy as np
from jax import lax
from jax.experimental import pallas as pl
from jax.experimental.pallas import tpu as pltpu

F32 = jnp.float32
BF16 = jnp.bfloat16
I32 = jnp.int32
U32 = jnp.uint32

CHUNK = 64
LEFT_CHUNKS = 8
REL_CLIP = 128
HEAD_DIM = 128
HEADS_A = 16
HEADS_B = 16
Q_LORA = 1024
KV_LORA = 512
NOPE_DIM = 128
ROPE_DIM = 64
V_DIM = 128
QK_B = NOPE_DIM + ROPE_DIM
ROPE_THETA = 10000.0
MEM_HEADS = 4
MEM_HEAD_DIM = 128
N_EXPERTS = 64
N_GROUPS = 8
TOPK_GROUPS = 4
TOP_K = 8
ROUTED_SCALE = 2.5
EPS = 1e-6

LANES = 128
SUBLANES = 8
ATT_BLOCK = 256
EXPERT_ROWS = 256
STEP_ROWS = 2 * EXPERT_ROWS
DMA_UNROLL = 16
COMBINE_GROUP = 16
NEG = -1e30
VMEM_LIMIT = 56 * 1024 * 1024

_NT = (((1,), (1,)), ((), ()))


def _cp(sem, vmem=VMEM_LIMIT):
    return pltpu.CompilerParams(dimension_semantics=sem, vmem_limit_bytes=vmem)


def _rms(x, g):
    return x * lax.rsqrt(jnp.mean(x * x, axis=-1, keepdims=True) + EPS) * g


def _pack_store(ref, first_token, x, scr):
    m, w = x.shape
    pitch = w // (2 * LANES)
    regions = scr.shape[0] // (2 * m)
    for s in range(pitch):
        base = (s % regions) * 2 * m
        scr[pl.ds(base, m, stride=2), :] = x[:, (2 * s) * LANES:(2 * s + 1) * LANES]
        scr[pl.ds(base + 1, m, stride=2), :] = x[:, (2 * s + 1) * LANES:(2 * s + 2) * LANES]
        z = scr[pl.ds(base, 2 * m), :].astype(BF16)
        ref[pl.ds(first_token * pitch + s, m, stride=pitch), :] = pltpu.bitcast(z, U32)


def _load_pairs(ref, first_token, m, pitch, s):
    w = ref[pl.ds(first_token * pitch + s, m, stride=pitch), :]
    return pltpu.bitcast(w, BF16).astype(F32)


def _split_pairs(z, scr, region):
    m = z.shape[0] // 2
    base = region * 2 * m
    scr[pl.ds(base, 2 * m), :] = z
    return scr[pl.ds(base, m, stride=2), :], scr[pl.ds(base + 1, m, stride=2), :]


def _load_unpacked(ref, first_token, m, pitch, scr, dtype=F32):
    regions = scr.shape[0] // (2 * m)
    cols = []
    for s in range(pitch):
        for piece in _split_pairs(_load_pairs(ref, first_token, m, pitch, s), scr, s % regions):
            cols.append(piece.astype(dtype))
    return jnp.concatenate(cols, axis=1)


def _norm_kernel(x_ref, g_ref, o_ref):
    o_ref[...] = _rms(x_ref[...], g_ref[...]).astype(o_ref.dtype)


def _norm_bf16(x, g, tm=256):
    n, d = x.shape
    return pl.pallas_call(
        _norm_kernel,
        grid=(n // tm,),
        in_specs=[pl.BlockSpec((tm, d), lambda i: (i, 0)), pl.BlockSpec((1, d), lambda i: (0, 0))],
        out_specs=pl.BlockSpec((tm, d), lambda i: (i, 0)),
        out_shape=jax.ShapeDtypeStruct((n, d), BF16),
        compiler_params=_cp(("parallel",)),
        name="norm_mix",
    )(x, g.reshape(1, d))


def _proj_t_kernel(h_ref, w_ref, o_ref):
    o_ref[...] = lax.dot_general(h_ref[...], w_ref[...].astype(BF16), _NT, preferred_element_type=F32)


def _proj_t(h, w_t, name, tm=1024):
    n, d = h.shape
    nc = w_t.shape[0]
    return pl.pallas_call(
        _proj_t_kernel,
        grid=(n // tm,),
        in_specs=[pl.BlockSpec((tm, d), lambda i: (i, 0)), pl.BlockSpec((nc, d), lambda i: (0, 0))],
        out_specs=pl.BlockSpec((tm, nc), lambda i: (i, 0)),
        out_shape=jax.ShapeDtypeStruct((n, nc), F32),
        compiler_params=_cp(("parallel",)),
        name=name,
    )(h, w_t)


def _in_proj_all_kernel(h_ref, w_ref, gs_ref, qk_ref, v_ref, c_ref, *, tn, n_qk, n_v):
    j = pl.program_id(1)
    acc = lax.dot_general(h_ref[...], w_ref[...].astype(BF16), _NT, preferred_element_type=F32)

    @pl.when(j < n_qk)
    def _():
        for c in range(tn // LANES):
            sl = slice(c * LANES, (c + 1) * LANES)
            qk_ref[:, sl] = _rms(acc[:, sl], gs_ref[:, sl]).astype(qk_ref.dtype)

    @pl.when((j >= n_qk) & (j < n_qk + n_v))
    def _():
        v_ref[...] = acc.astype(v_ref.dtype)

    @pl.when(j >= n_qk + n_v)
    def _():
        c_ref[...] = acc


def _in_proj_all(h, w_t, gs_qk, w_qk, w_v, w_c, tm=1024, tn=512):
    n, d = h.shape
    n_qk, n_v, n_c = w_qk // tn, w_v // tn, w_c // tn
    return pl.pallas_call(
        functools.partial(_in_proj_all_kernel, tn=tn, n_qk=n_qk, n_v=n_v),
        grid=(n // tm, n_qk + n_v + n_c),
        in_specs=[pl.BlockSpec((tm, d), lambda i, j: (i, 0)),
                  pl.BlockSpec((tn, d), lambda i, j: (j, 0)),
                  pl.BlockSpec((1, tn), lambda i, j: (0, jnp.minimum(j, n_qk - 1)))],
        out_specs=[pl.BlockSpec((tm, tn), lambda i, j: (i, jnp.minimum(j, n_qk - 1))),
                   pl.BlockSpec((tm, tn), lambda i, j: (i, jnp.clip(j - n_qk, 0, n_v - 1))),
                   pl.BlockSpec((tm, tn), lambda i, j: (i, jnp.clip(j - n_qk - n_v, 0, n_c - 1)))],
        out_shape=[jax.ShapeDtypeStruct((n, w_qk), BF16), jax.ShapeDtypeStruct((n, w_v), BF16),
                   jax.ShapeDtypeStruct((n, w_c), F32)],
        compiler_params=_cp(("parallel", "arbitrary")),
        name="in_proj",
    )(h, w_t, gs_qk.reshape(1, w_qk))


def _attn_a_kernel(q_ref, k0_ref, k1_ref, k2_ref, v0_ref, v1_ref, v2_ref, b_ref, o_ref, *, heads):
    i = pl.program_id(2)
    k_refs = (k0_ref, k1_ref, k2_ref)
    v_refs = (v0_ref, v1_ref, v2_ref)
    for hh in range(heads):
        sl = slice(hh * HEAD_DIM, (hh + 1) * HEAD_DIM)
        q = q_ref[:, sl]
        s = []
        for d in range(3):
            sd = lax.dot_general(q, k_refs[d][:, sl], _NT, preferred_element_type=F32)
            sd = sd + b_ref[hh, :, d * ATT_BLOCK:(d + 1) * ATT_BLOCK]
            if d > 0:
                sd = jnp.where(i >= d, sd, NEG)
            s.append(sd)
        m = jnp.maximum(jnp.maximum(s[0].max(-1, keepdims=True), s[1].max(-1, keepdims=True)),
                        s[2].max(-1, keepdims=True))
        l = jnp.zeros_like(m)
        o = jnp.zeros((q.shape[0], HEAD_DIM), F32)
        for d in range(3):
            p = jnp.exp(s[d] - m)
            l = l + p.sum(-1, keepdims=True)
            o = o + jnp.dot(p.astype(BF16), v_refs[d][:, sl], preferred_element_type=F32)
        o_ref[:, sl] = (o / l).astype(o_ref.dtype)


def _band_bias(rel_bias):
    blk = ATT_BLOCK
    period = 2 * blk + 1
    heads = rel_bias.shape[0]
    r = np.arange(blk)[:, None]
    c = np.arange(blk)[None, :]
    per_blk = blk // CHUNK
    k = np.arange(period)
    delta = np.where(k <= blk, k, k - period)
    tiles = []
    for d in range(3):
        idx = np.clip(blk * d - delta, -(CHUNK - 1), REL_CLIP) + CHUNK - 1
        w = rel_bias[:, idx].astype(F32)
        b = jnp.tile(w, (1, blk))[:, :blk * (period - 1)].reshape(heads, blk, period - 1)[:, :, :blk]
        cdiff = per_blk * d + r // CHUNK - c // CHUNK
        valid = (cdiff >= 0) & (cdiff <= LEFT_CHUNKS)
        tiles.append(jnp.where(valid[None], b, NEG))
    return jnp.concatenate(tiles, axis=-1)


def _attn_a(qk, v, bias, batch, seq, heads_per_step=8):
    n = qk.shape[0]
    nq = seq // ATT_BLOCK
    hw = heads_per_step * HEAD_DIM
    groups = HEADS_A // heads_per_step
    kcol0 = HEADS_A * HEAD_DIM // hw

    def kspec(d, col0):
        return pl.BlockSpec((ATT_BLOCK, hw), lambda g, b, i: (b * nq + jnp.maximum(i - d, 0), col0 + g))

    return pl.pallas_call(
        functools.partial(_attn_a_kernel, heads=heads_per_step),
        grid=(groups, batch, nq),
        in_specs=[pl.BlockSpec((ATT_BLOCK, hw), lambda g, b, i: (b * nq + i, g)),
                  kspec(0, kcol0), kspec(1, kcol0), kspec(2, kcol0),
                  kspec(0, 0), kspec(1, 0), kspec(2, 0),
                  pl.BlockSpec((heads_per_step, ATT_BLOCK, 3 * ATT_BLOCK), lambda g, b, i: (g, 0, 0))],
        out_specs=pl.BlockSpec((ATT_BLOCK, hw), lambda g, b, i: (b * nq + i, g)),
        out_shape=jax.ShapeDtypeStruct((n, HEADS_A * HEAD_DIM), BF16),
        compiler_params=_cp(("parallel", "parallel", "arbitrary")),
        name="attn_band",
    )(qk, qk, qk, qk, v, v, v, bias)


def _rope_pe(pe, g, cos_ref, sa_ref, sb_ref):
    ss = jnp.sum(pe * pe, axis=-1, keepdims=True) * (1.0 / ROPE_DIM)
    pn = pe * lax.rsqrt(ss + EPS) * g
    half = ROPE_DIM // 2
    return (pn * cos_ref[...] + pltpu.roll(pn, LANES - half, 1) * sa_ref[...]
            + pltpu.roll(pn, half, 1) * sb_ref[...])


def _mla_q_kernel(c_ref, gc_ref, w_ref, gq_ref, cos_ref, sa_ref, sb_ref, o_ref, xn_ref, *, heads, scale):
    @pl.when(pl.program_id(1) == 0)
    def _():
        xn_ref[...] = _rms(c_ref[...], gc_ref[...]).astype(BF16)

    acc = jnp.dot(xn_ref[...], w_ref[...].astype(BF16), preferred_element_type=F32)
    for hh in range(heads):
        base = hh * 2 * LANES
        nope = acc[:, base:base + LANES]
        pe = acc[:, base + LANES:base + 2 * LANES]
        o_ref[:, base:base + LANES] = (_rms(nope, gq_ref[:, :LANES]) * scale).astype(o_ref.dtype)
        o_ref[:, base + LANES:base + 2 * LANES] = (
            _rope_pe(pe, gq_ref[:, LANES:], cos_ref, sa_ref, sb_ref) * scale).astype(o_ref.dtype)


def _mla_q(c, g_cq, w_uq_pad, gq, cos_t, sin_a, sin_b, tm=512, heads_per_step=8):
    n = c.shape[0]
    tn = heads_per_step * 2 * LANES
    ncols = w_uq_pad.shape[1]
    row = lambda i, j: (i, 0)
    return pl.pallas_call(
        functools.partial(_mla_q_kernel, heads=heads_per_step, scale=QK_B ** -0.5),
        grid=(n // tm, ncols // tn),
        in_specs=[pl.BlockSpec((tm, Q_LORA), row),
                  pl.BlockSpec((1, Q_LORA), lambda i, j: (0, 0)),
                  pl.BlockSpec((Q_LORA, tn), lambda i, j: (0, j)),
                  pl.BlockSpec((1, 2 * LANES), lambda i, j: (0, 0)),
                  pl.BlockSpec((tm, LANES), row), pl.BlockSpec((tm, LANES), row), pl.BlockSpec((tm, LANES), row)],
        out_specs=pl.BlockSpec((tm, tn), lambda i, j: (i, j)),
        out_shape=jax.ShapeDtypeStruct((n, ncols), BF16),
        scratch_shapes=[pltpu.VMEM((tm, Q_LORA), BF16)],
        compiler_params=_cp(("parallel", "arbitrary")),
        name="mla_q_proj",
    )(c, g_cq.reshape(1, Q_LORA), w_uq_pad, gq, cos_t, sin_a, sin_b)


def _mla_kv_kernel(c_ref, gc_ref, kpe_ref, w_ref, gk_ref, cos_ref, sa_ref, sb_ref, k_ref, v_ref,
                   xn_ref, pe_ref, *, heads):
    @pl.when(pl.program_id(1) == 0)
    def _():
        xn_ref[...] = _rms(c_ref[...], gc_ref[...]).astype(BF16)
        pe_ref[...] = _rope_pe(kpe_ref[...], gk_ref[:, LANES:], cos_ref, sa_ref, sb_ref).astype(BF16)

    acc = jnp.dot(xn_ref[...], w_ref[...].astype(BF16), preferred_element_type=F32)
    for hh in range(heads):
        base = hh * 2 * LANES
        k_ref[:, base:base + LANES] = _rms(acc[:, base:base + LANES], gk_ref[:, :LANES]).astype(k_ref.dtype)
        k_ref[:, base + LANES:base + 2 * LANES] = pe_ref[...]
        v_ref[:, hh * LANES:(hh + 1) * LANES] = acc[:, base + LANES:base + 2 * LANES].astype(v_ref.dtype)


def _mla_kv(c, g_ckv, kpe, w_ukv, gk, cos_t, sin_a, sin_b, tm=512, heads_per_step=8):
    n = c.shape[0]
    tn = heads_per_step * 2 * LANES
    ncols = w_ukv.shape[1]
    cblk = Q_LORA // KV_LORA
    row = lambda i, j: (i, 0)
    return pl.pallas_call(
        functools.partial(_mla_kv_kernel, heads=heads_per_step),
        grid=(n // tm, ncols // tn),
        in_specs=[pl.BlockSpec((tm, KV_LORA), lambda i, j: (i, cblk)),
                  pl.BlockSpec((1, KV_LORA), lambda i, j: (0, 0)),
                  pl.BlockSpec((tm, LANES), row),
                  pl.BlockSpec((KV_LORA, tn), lambda i, j: (0, j)),
                  pl.BlockSpec((1, 2 * LANES), lambda i, j: (0, 0)),
                  pl.BlockSpec((tm, LANES), row), pl.BlockSpec((tm, LANES), row), pl.BlockSpec((tm, LANES), row)],
        out_specs=[pl.BlockSpec((tm, tn), lambda i, j: (i, j)),
                   pl.BlockSpec((tm, tn // 2), lambda i, j: (i, j))],
        out_shape=[jax.ShapeDtypeStruct((n, ncols), BF16),
                   jax.ShapeDtypeStruct((n, ncols // 2), BF16)],
        scratch_shapes=[pltpu.VMEM((tm, KV_LORA), BF16), pltpu.VMEM((tm, LANES), BF16)],
        compiler_params=_cp(("parallel", "arbitrary")),
        name="mla_kv_proj",
    )(c, g_ckv.reshape(1, KV_LORA), kpe, w_ukv, gk, cos_t, sin_a, sin_b)


def _attn_b_kernel(q_ref, k_ref, v_ref, o_ref, *, heads):
    i = pl.program_id(2)
    tq = q_ref.shape[0]
    qw = 2 * LANES
    qs = [q_ref[:, h * qw:(h + 1) * qw] for h in range(heads)]

    def step(first_blk, width, carry, masked):
        start = pl.multiple_of(first_blk * ATT_BLOCK, ATT_BLOCK)
        if masked:
            per_blk = ATT_BLOCK // CHUNK
            r = i * per_blk + lax.broadcasted_iota(I32, (tq, width), 0) // CHUNK
            c = first_blk * per_blk + lax.broadcasted_iota(I32, (tq, width), 1) // CHUNK
            keep = c <= r
        out = []
        for h in range(heads):
            m, l, acc = carry[h]
            s = lax.dot_general(qs[h], k_ref[pl.ds(start, width), h * qw:(h + 1) * qw], _NT,
                                preferred_element_type=F32)
            if masked:
                s = jnp.where(keep, s, NEG)
            m_new = jnp.maximum(m, s.max(-1, keepdims=True))
            alpha = jnp.exp(m - m_new)
            p = jnp.exp(s - m_new)
            l = alpha * l + p.sum(-1, keepdims=True)
            acc = alpha * acc + jnp.dot(p.astype(BF16), v_ref[pl.ds(start, width), h * V_DIM:(h + 1) * V_DIM],
                                        preferred_element_type=F32)
            out.append((m_new, l, acc))
        return tuple(out)

    def finish(final):
        for h in range(heads):
            _, l, acc = final[h]
            o_ref[:, h * V_DIM:(h + 1) * V_DIM] = (acc / l).astype(o_ref.dtype)

    init = tuple((jnp.full((tq, 1), NEG, F32), jnp.zeros((tq, 1), F32), jnp.zeros((tq, V_DIM), F32))
                 for _ in range(heads))
    carry = lax.fori_loop(0, i // 2, lambda j, c: step(2 * j, 2 * ATT_BLOCK, c, False), init)

    @pl.when(i % 2 == 1)
    def _():
        finish(step(i - 1, 2 * ATT_BLOCK, carry, True))

    @pl.when(i % 2 == 0)
    def _():
        finish(step(i, ATT_BLOCK, carry, True))


def _attn_b(qf, kf, vb, batch, seq, heads_per_step=8):
    n = qf.shape[0]
    nq = seq // ATT_BLOCK
    qw = heads_per_step * 2 * LANES
    vw = heads_per_step * V_DIM
    return pl.pallas_call(
        functools.partial(_attn_b_kernel, heads=heads_per_step),
        grid=(batch, HEADS_B // heads_per_step, nq),
        in_specs=[pl.BlockSpec((ATT_BLOCK, qw), lambda b, g, i: (b * nq + i, g)),
                  pl.BlockSpec((seq, qw), lambda b, g, i: (b, g)),
                  pl.BlockSpec((seq, vw), lambda b, g, i: (b, g))],
        out_specs=pl.BlockSpec((ATT_BLOCK, vw), lambda b, g, i: (b * nq + i, g)),
        out_shape=jax.ShapeDtypeStruct((n, HEADS_B * V_DIM), BF16),
        compiler_params=_cp(("parallel", "parallel", "arbitrary")),
        name="attn_latent",
    )(qf, kf, vb)


def _out_proj_kernel(oa_ref, ob_ref, w_ref, x_ref, o_ref, wb_ref):
    @pl.when(pl.program_id(1) == 0)
    def _():
        wb_ref[...] = w_ref[...].astype(BF16)

    ka = oa_ref.shape[1]
    acc = jnp.dot(oa_ref[...], wb_ref[:ka, :], preferred_element_type=F32)
    acc = acc + jnp.dot(ob_ref[...], wb_ref[ka:, :], preferred_element_type=F32)
    o_ref[...] = x_ref[...] + acc


def _out_proj(oa, ob, w_o, x, tm=256, tn=1024):
    n, ka = oa.shape
    kb = ob.shape[1]
    d = w_o.shape[1]
    return pl.pallas_call(
        _out_proj_kernel,
        grid=(d // tn, n // tm),
        in_specs=[pl.BlockSpec((tm, ka), lambda j, i: (i, 0)),
                  pl.BlockSpec((tm, kb), lambda j, i: (i, 0)),
                  pl.BlockSpec((ka + kb, tn), lambda j, i: (0, j)),
                  pl.BlockSpec((tm, tn), lambda j, i: (i, j))],
        out_specs=pl.BlockSpec((tm, tn), lambda j, i: (i, j)),
        out_shape=jax.ShapeDtypeStruct((n, d), F32),
        scratch_shapes=[pltpu.VMEM((ka + kb, tn), BF16)],
        compiler_params=_cp(("parallel", "arbitrary")),
        name="out_proj",
    )(oa, ob, w_o, x)


def _normed_proj_kernel(x_ref, g_ref, w_ref, gs_ref, o_ref, wb_ref, *, norm_cols):
    @pl.when(pl.program_id(0) == 0)
    def _():
        wb_ref[...] = w_ref[...].astype(BF16)

    h = _rms(x_ref[...], g_ref[...]).astype(BF16)
    acc = jnp.dot(h, wb_ref[...], preferred_element_type=F32)
    for c in range(acc.shape[1] // LANES):
        sl = slice(c * LANES, (c + 1) * LANES)
        if c * LANES < norm_cols:
            o_ref[:, sl] = _rms(acc[:, sl], gs_ref[:, sl]).astype(o_ref.dtype)
        else:
            o_ref[:, sl] = acc[:, sl].astype(o_ref.dtype)


def _normed_proj(x, g, w, gs, norm_cols, name):
    nb, tm, d = x.shape
    n = nb * tm
    nc = w.shape[1]
    return pl.pallas_call(
        functools.partial(_normed_proj_kernel, norm_cols=norm_cols),
        grid=(nb,),
        in_specs=[pl.BlockSpec((None, tm, d), lambda i: (i, 0, 0)),
                  pl.BlockSpec((1, d), lambda i: (0, 0)),
                  pl.BlockSpec((d, nc), lambda i: (0, 0)),
                  pl.BlockSpec((1, nc), lambda i: (0, 0))],
        out_specs=pl.BlockSpec((tm, nc), lambda i: (i, 0)),
        out_shape=jax.ShapeDtypeStruct((n, nc), BF16),
        scratch_shapes=[pltpu.VMEM((d, nc), BF16)],
        compiler_params=_cp(("arbitrary",)),
        name=name,
    )(x, g.reshape(1, d), w, gs.reshape(1, nc))


def _cross_kernel(x_ref, g_ref, wq_ref, gq_ref, k_ref, v_ref, wo_ref, o_ref):
    x = x_ref[...]
    q = jnp.dot(_rms(x, g_ref[...]).astype(BF16), wq_ref[...], preferred_element_type=F32)
    outs = []
    for hh in range(MEM_HEADS):
        sl = slice(hh * MEM_HEAD_DIM, (hh + 1) * MEM_HEAD_DIM)
        qh = _rms(q[:, sl], gq_ref[:, sl]).astype(BF16)
        s = lax.dot_general(qh, k_ref[:, sl], _NT, preferred_element_type=F32)
        p = jnp.exp(s - s.max(-1, keepdims=True))
        o = jnp.dot(p.astype(BF16), v_ref[:, sl], preferred_element_type=F32)
        outs.append((o / p.sum(-1, keepdims=True)).astype(BF16))
    acc = x
    for hh in range(MEM_HEADS):
        sl = slice(hh * MEM_HEAD_DIM, (hh + 1) * MEM_HEAD_DIM)
        acc = acc + jnp.dot(outs[hh], wo_ref[sl, :], preferred_element_type=F32)
    o_ref[...] = acc


def _cross_attn(x, g, w_xq, gq, kvx, w_xo, batch, seq, mem_tokens, tm=256):
    n, d = x.shape
    per_b = seq // tm
    hw = MEM_HEADS * MEM_HEAD_DIM
    const = lambda b, i: (0, 0)
    return pl.pallas_call(
        _cross_kernel,
        grid=(batch, per_b),
        in_specs=[pl.BlockSpec((tm, d), lambda b, i: (b * per_b + i, 0)),
                  pl.BlockSpec((1, d), const),
                  pl.BlockSpec((d, hw), const),
                  pl.BlockSpec((1, hw), const),
                  pl.BlockSpec((mem_tokens, hw), lambda b, i: (b, 0)),
                  pl.BlockSpec((mem_tokens, hw), lambda b, i: (b, 1)),
                  pl.BlockSpec((hw, d), const)],
        out_specs=pl.BlockSpec((tm, d), lambda b, i: (b * per_b + i, 0)),
        out_shape=jax.ShapeDtypeStruct((n, d), F32),
        compiler_params=_cp(("parallel", "arbitrary")),
        name="cross_attn",
    )(x, g.reshape(1, d), w_xq.astype(BF16), gq.reshape(1, hw), kvx, kvx, w_xo.astype(BF16))


def _router_kernel(x_ref, g_ref, wr_ref, b_ref, tri_ref, wg_ref, wu_ref, hp_ref, eid_ref, gate_ref, rank_ref,
                   cnt_ref, ts_ref, run_ref, scr_ref):
    t = pl.program_id(0)

    @pl.when(t == 0)
    def _():
        run_ref[...] = jnp.zeros_like(run_ref)

    h = _rms(x_ref[...], g_ref[...])
    _pack_store(hp_ref, 0, h, scr_ref)
    hb = h.astype(BF16)
    ts_ref[...] = _silu_mul(jnp.dot(hb, wg_ref[...], preferred_element_type=F32),
                            jnp.dot(hb, wu_ref[...], preferred_element_type=F32)).astype(ts_ref.dtype)
    tm = h.shape[0]
    per_g = N_EXPERTS // N_GROUPS

    wr = wr_ref[...]
    wr_hi = wr.astype(BF16)
    wr_lo = (wr - wr_hi.astype(F32)).astype(BF16)
    h_lo = (h - hb.astype(F32)).astype(BF16)
    logits = (lax.dot_general(wr_hi, hb, _NT, preferred_element_type=F32)
              + lax.dot_general(wr_lo, hb, _NT, preferred_element_type=F32)
              + lax.dot_general(wr_hi, h_lo, _NT, preferred_element_type=F32))
    scores = 1.0 / (1.0 + jnp.exp(-logits))
    choice = scores + b_ref[...]

    sub = lax.broadcasted_iota(I32, (per_g, tm), 0).astype(F32)
    rows = []
    for g in range(N_GROUPS):
        c = choice[g * per_g:(g + 1) * per_g, :]
        m1 = c.max(0, keepdims=True)
        first = jnp.where(c == m1, sub, float(per_g)).min(0, keepdims=True)
        m2 = jnp.where(sub == first, -jnp.inf, c).max(0, keepdims=True)
        rows.append(m1 + m2)
    gs = jnp.concatenate(rows, axis=0)

    gsub = lax.broadcasted_iota(I32, (N_GROUPS, tm), 0).astype(F32)
    beaten = jnp.zeros((N_GROUPS, tm), F32)
    for g2 in range(N_GROUPS):
        row = gs[g2:g2 + 1, :]
        wins = (row > gs) | ((row == gs) & (gsub > float(g2)))
        beaten = beaten + jnp.where(wins, 1.0, 0.0)
    g_ok = jnp.where(beaten < TOPK_GROUPS, 1.0, 0.0)
    e_ok = jnp.concatenate(
        [jnp.broadcast_to(g_ok[g:g + 1, :], (per_g, tm)) for g in range(N_GROUPS)], axis=0)

    eiota = lax.broadcasted_iota(I32, (N_EXPERTS, tm), 0).astype(F32)
    masked = jnp.where(e_ok > 0.5, choice, -jnp.inf)
    chosen = jnp.zeros((N_EXPERTS, tm), F32)
    eids, ws = [], []
    for _ in range(TOP_K):
        m = masked.max(0, keepdims=True)
        idx = jnp.where(masked == m, eiota, float(N_EXPERTS)).min(0, keepdims=True)
        sel = eiota == idx
        ws.append(jnp.where(sel, scores, 0.0).sum(0, keepdims=True))
        masked = jnp.where(sel, -jnp.inf, masked)
        chosen = jnp.where(sel, 1.0, chosen)
        eids.append(idx)
    wsum = ws[0]
    for w in ws[1:]:
        wsum = wsum + w
    denom = wsum + 1e-20

    pos = jnp.dot(chosen.astype(BF16), tri_ref[...], preferred_element_type=F32) + run_ref[:, 0:1]
    run_ref[...] = run_ref[...] + chosen.sum(1, keepdims=True)
    cnt_ref[...] = run_ref[...].astype(I32)

    for r in range(TOP_K):
        eid_ref[r:r + 1, :] = eids[r].astype(I32)
        gate_ref[r:r + 1, :] = ws[r] / denom * ROUTED_SCALE
        rank_ref[r:r + 1, :] = jnp.where(eiota == eids[r], pos, 0.0).sum(0, keepdims=True).astype(I32)


def _router(x, g, w_router, router_bias, w_sh_gate, w_sh_up, tm=256):
    n, d = x.shape
    ne = w_router.shape[1]
    ff = w_sh_gate.shape[1]
    tri = (jnp.arange(tm)[:, None] < jnp.arange(tm)[None, :]).astype(BF16)
    row8 = lambda i: (0, i)
    pitch = d // 2 // LANES
    return pl.pallas_call(
        _router_kernel,
        grid=(n // tm,),
        in_specs=[pl.BlockSpec((tm, d), lambda i: (i, 0)),
                  pl.BlockSpec((1, d), lambda i: (0, 0)),
                  pl.BlockSpec((ne, d), lambda i: (0, 0)),
                  pl.BlockSpec((ne, 1), lambda i: (0, 0)),
                  pl.BlockSpec((tm, tm), lambda i: (0, 0)),
                  pl.BlockSpec((d, ff), lambda i: (0, 0)), pl.BlockSpec((d, ff), lambda i: (0, 0))],
        out_specs=[pl.BlockSpec((tm * pitch, LANES), lambda i: (i, 0)),
                   pl.BlockSpec((TOP_K, tm), row8), pl.BlockSpec((TOP_K, tm), row8),
                   pl.BlockSpec((TOP_K, tm), row8),
                   pl.BlockSpec((ne, LANES), lambda i: (0, 0)),
                   pl.BlockSpec((tm, ff), lambda i: (i, 0))],
        out_shape=[jax.ShapeDtypeStruct((n * pitch, LANES), U32),
                   jax.ShapeDtypeStruct((TOP_K, n), I32), jax.ShapeDtypeStruct((TOP_K, n), F32),
                   jax.ShapeDtypeStruct((TOP_K, n), I32),
                   jax.ShapeDtypeStruct((ne, LANES), I32),
                   jax.ShapeDtypeStruct((n, ff), BF16)],
        scratch_shapes=[pltpu.VMEM((ne, LANES), F32), pltpu.VMEM((2 * tm * pitch, LANES), F32)],
        compiler_params=_cp(("arbitrary",)),
        name="router",
    )(x, g.reshape(1, d), w_router.T, router_bias.reshape(ne, 1), tri,
      w_sh_gate.astype(BF16), w_sh_up.astype(BF16))


def _dispatch_kernel(cnt_ref, start_ref, dest_ref, hp_ref, xs_ref, sem, *, tm, pitch):
    t = pl.program_id(0)
    pairs = TOP_K * tm

    def row_copy(r, slot):
        src = hp_ref.at[pl.ds(pl.multiple_of(r * pitch, pitch), pitch), :]
        dst = xs_ref.at[pl.ds(pl.multiple_of(slot * pitch, pitch), pitch), :]
        return pltpu.make_async_copy(src, dst, sem)

    def issue(g, c):
        for u in range(DMA_UNROLL):
            p = g * DMA_UNROLL + u
            row_copy(p & (tm - 1), dest_ref[0, 0, p]).start(priority=u % 2)
        return c

    lax.fori_loop(0, pairs // DMA_UNROLL, issue, 0)
    for _ in range(TOP_K):
        pltpu.make_async_copy(hp_ref, xs_ref.at[pl.ds(0, tm * pitch), :], sem).wait()

    @pl.when(t == pl.num_programs(0) - 1)
    def _():
        def run_copy(first, rows):
            dst = xs_ref.at[pl.ds(pl.multiple_of(first * pitch, pitch), rows * pitch), :]
            return pltpu.make_async_copy(hp_ref.at[pl.ds(0, rows * pitch), :], dst, sem)

        def per_expert(e, c):
            used = cnt_ref[e]
            npad = (-used) & (EXPERT_ROWS - 1)
            first = start_ref[e] * STEP_ROWS + used
            sizes = [1 << k for k in range(EXPERT_ROWS.bit_length() - 1)]
            offs = []
            off = first
            for size in sizes:
                offs.append(off)
                off = off + (npad & size)
            for size, o in zip(sizes, offs):
                pl.when((npad & size) != 0)(lambda size=size, o=o: run_copy(o, size).start())
            for size in sizes:
                pl.when((npad & size) != 0)(lambda size=size: run_copy(0, size).wait())
            return c

        lax.fori_loop(0, N_EXPERTS, per_expert, 0)


def _dispatch(hp, dest, counts, step_start, p_rows, pitch, tm=512):
    n = hp.shape[0] // pitch
    tiles = n // tm
    dest_t = dest.reshape(TOP_K, tiles, tm).transpose(1, 0, 2).reshape(tiles, 1, TOP_K * tm)
    grid_spec = pltpu.PrefetchScalarGridSpec(
        num_scalar_prefetch=2,
        grid=(tiles,),
        in_specs=[pl.BlockSpec((1, 1, TOP_K * tm), lambda i, *_: (i, 0, 0), memory_space=pltpu.SMEM),
                  pl.BlockSpec((tm * pitch, LANES), lambda i, *_: (i, 0))],
        out_specs=pl.BlockSpec(memory_space=pl.ANY),
        scratch_shapes=[pltpu.SemaphoreType.DMA(())],
    )
    return pl.pallas_call(
        functools.partial(_dispatch_kernel, tm=tm, pitch=pitch),
        grid_spec=grid_spec,
        out_shape=jax.ShapeDtypeStruct((p_rows * pitch, LANES), U32),
        compiler_params=_cp(("arbitrary",)),
        name="moe_dispatch",
    )(counts, step_start, dest_t, hp)


def _silu_mul(g, u):
    return g / (1.0 + jnp.exp(-g)) * u


def _expert_weights(sched_ref, w_hbm_refs, wbuf_ref, sem):
    b = pl.program_id(0)

    def copies(e, slot):
        return [pltpu.make_async_copy(w.at[e], wbuf_ref.at[slot, i], sem.at[slot])
                for i, w in enumerate(w_hbm_refs)]

    @pl.when(b == 0)
    def _():
        for c in copies(sched_ref[0, 0], 0):
            c.start(priority=1)

    slot = sched_ref[3, b]

    @pl.when(sched_ref[2, b] == 1)
    def _():
        for c in copies(0, slot):
            c.wait()
        nxt = sched_ref[4, b]

        @pl.when(nxt >= 0)
        def _():
            for c in copies(nxt, 1 - slot):
                c.start(priority=1)

    return slot


def _expert_up_kernel(sched_ref, nu_ref, xs_ref, wg_hbm, wu_hbm, o_ref, wbuf_ref, scr_ref, sem):
    b = pl.program_id(0)
    pitch = xs_ref.shape[0] // STEP_ROWS
    slot = _expert_weights(sched_ref, (wg_hbm, wu_hbm), wbuf_ref, sem)

    def sub_block(j):
        x = _load_unpacked(xs_ref, j * EXPERT_ROWS, EXPERT_ROWS, pitch, scr_ref)
        g = jnp.dot(x, wbuf_ref[slot, 0], preferred_element_type=F32)
        u = jnp.dot(x, wbuf_ref[slot, 1], preferred_element_type=F32)
        o_ref[j * EXPERT_ROWS:(j + 1) * EXPERT_ROWS, :] = _silu_mul(g, u).astype(o_ref.dtype)

    for j in range(STEP_ROWS // EXPERT_ROWS):
        pl.when(sched_ref[1, b] > j)(functools.partial(sub_block, j))


def _row_block(b, sched, nu):
    return (jnp.minimum(b, nu[0] - 1), 0)


def _expert_up(xs, w_gate, w_up, sched, n_used, n_steps, pitch):
    _, d, ff = w_gate.shape
    grid_spec = pltpu.PrefetchScalarGridSpec(
        num_scalar_prefetch=2,
        grid=(n_steps,),
        in_specs=[pl.BlockSpec((STEP_ROWS * pitch, LANES), _row_block),
                  pl.BlockSpec(memory_space=pl.ANY), pl.BlockSpec(memory_space=pl.ANY)],
        out_specs=pl.BlockSpec((STEP_ROWS, ff), _row_block),
        scratch_shapes=[pltpu.VMEM((2, 2, d, ff), F32),
                        pltpu.VMEM((2 * EXPERT_ROWS * pitch, LANES), F32),
                        pltpu.SemaphoreType.DMA((2,))],
    )
    return pl.pallas_call(
        _expert_up_kernel,
        grid_spec=grid_spec,
        out_shape=jax.ShapeDtypeStruct((n_steps * STEP_ROWS, ff), BF16),
        compiler_params=_cp(("arbitrary",)),
        name="expert_up",
    )(sched, n_used, xs, w_gate, w_up)


def _expert_down_kernel(sched_ref, nu_ref, h_ref, wd_hbm, o_ref, wbuf_ref, scr_ref, sem):
    b = pl.program_id(0)
    slot = _expert_weights(sched_ref, (wd_hbm,), wbuf_ref, sem)

    def sub_block(j):
        y = jnp.dot(h_ref[j * EXPERT_ROWS:(j + 1) * EXPERT_ROWS, :], wbuf_ref[slot, 0].astype(BF16),
                    preferred_element_type=F32)
        _pack_store(o_ref, j * EXPERT_ROWS, y, scr_ref)

    for j in range(STEP_ROWS // EXPERT_ROWS):
        pl.when(sched_ref[1, b] > j)(functools.partial(sub_block, j))


def _expert_down(hs, w_down, sched, n_used, n_steps, pitch):
    p_rows, ff = hs.shape
    d = w_down.shape[2]
    grid_spec = pltpu.PrefetchScalarGridSpec(
        num_scalar_prefetch=2,
        grid=(n_steps,),
        in_specs=[pl.BlockSpec((STEP_ROWS, ff), _row_block), pl.BlockSpec(memory_space=pl.ANY)],
        out_specs=pl.BlockSpec((STEP_ROWS * pitch, LANES), _row_block),
        scratch_shapes=[pltpu.VMEM((2, 1, ff, d), F32),
                        pltpu.VMEM((2 * EXPERT_ROWS * pitch, LANES), F32),
                        pltpu.SemaphoreType.DMA((2,))],
    )
    return pl.pallas_call(
        _expert_down_kernel,
        grid_spec=grid_spec,
        out_shape=jax.ShapeDtypeStruct((p_rows * pitch, LANES), U32),
        compiler_params=_cp(("arbitrary",)),
        name="expert_down",
    )(sched, n_used, hs, w_down)


def _combine_kernel(dfirst_ref, dnext_ref, gate_ref, x_ref, t_ref, wsd_ref, ys_ref, o_ref, buf_ref, scr_ref,
                    base_ref, sem, *, tm, pitch):
    t = pl.program_id(0)
    pairs = TOP_K * tm
    bpitch = buf_ref.shape[0] // (2 * pairs)

    def row_copy(dref, slot, p):
        src = ys_ref.at[pl.ds(pl.multiple_of(dref[0, 0, p] * pitch, pitch), pitch), :]
        dst = buf_ref.at[pl.ds(pl.multiple_of((slot * pairs + p) * bpitch, SUBLANES), pitch), :]
        return pltpu.make_async_copy(src, dst, sem.at[slot])

    def wait_slot(slot):
        for _ in range(TOP_K):
            pltpu.make_async_copy(ys_ref.at[pl.ds(0, tm * pitch), :], buf_ref.at[pl.ds(0, tm * pitch), :],
                                  sem.at[slot]).wait()

    def issue_tile(dref, slot):
        def body(g, c):
            for u in range(DMA_UNROLL):
                row_copy(dref, slot, g * DMA_UNROLL + u).start(priority=u % 2)
            return c

        lax.fori_loop(0, pairs // DMA_UNROLL, body, 0)

    @pl.when(t == 0)
    def _():
        issue_tile(dfirst_ref, 0)

    @pl.when(t + 1 < pl.num_programs(0))
    def _():
        issue_tile(dnext_ref, (t + 1) % 2)

    base_ref[...] = x_ref[...] + jnp.dot(t_ref[...], wsd_ref[...], preferred_element_type=F32)

    cur = t % 2
    wait_slot(cur)

    for grp in range(tm // COMBINE_GROUP):
        r0 = grp * COMBINE_GROUP
        rows = slice(r0, r0 + COMBINE_GROUP)
        gates = [jnp.broadcast_to(gate_ref[2 * r0:2 * (r0 + COMBINE_GROUP), k:k + 1], (2 * COMBINE_GROUP, LANES))
                 for k in range(TOP_K)]
        for s in range(pitch):
            acc = None
            for k in range(TOP_K):
                term = gates[k] * _load_pairs(buf_ref, cur * pairs + k * tm + r0, COMBINE_GROUP, bpitch, s)
                acc = term if acc is None else acc + term
            even, odd = _split_pairs(acc, scr_ref, grp * pitch + s)
            c0 = slice((2 * s) * LANES, (2 * s + 1) * LANES)
            c1 = slice((2 * s + 1) * LANES, (2 * s + 2) * LANES)
            o_ref[rows, c0] = base_ref[rows, c0] + even
            o_ref[rows, c1] = base_ref[rows, c1] + odd


def _combine(dest, gate_rows, x, t_shared, w_sh_down, ys, pitch, tm=128):
    n, d = x.shape
    ff = t_shared.shape[1]
    tiles = n // tm
    dest_t = dest.reshape(TOP_K, tiles, tm).transpose(1, 0, 2).reshape(tiles, 1, TOP_K * tm)
    dspec = lambda imap: pl.BlockSpec((1, 1, TOP_K * tm), imap, memory_space=pltpu.SMEM)
    bpitch = pitch if (pitch // SUBLANES) % 2 else pitch + SUBLANES
    return pl.pallas_call(
        functools.partial(_combine_kernel, tm=tm, pitch=pitch),
        grid=(tiles,),
        in_specs=[dspec(lambda i: (0, 0, 0)),
                  dspec(lambda i: (jnp.minimum(i + 1, tiles - 1), 0, 0)),
                  pl.BlockSpec((2 * tm, TOP_K), lambda i: (i, 0)),
                  pl.BlockSpec((tm, d), lambda i: (i, 0)),
                  pl.BlockSpec((tm, ff), lambda i: (i, 0)),
                  pl.BlockSpec((ff, d), lambda i: (0, 0)),
                  pl.BlockSpec(memory_space=pl.ANY)],
        out_specs=pl.BlockSpec((tm, d), lambda i: (i, 0)),
        out_shape=jax.ShapeDtypeStruct((n, d), F32),
        scratch_shapes=[pltpu.VMEM((2 * TOP_K * tm * bpitch, LANES), U32),
                        pltpu.VMEM((2 * tm * pitch, LANES), F32),
                        pltpu.VMEM((tm, d), F32),
                        pltpu.SemaphoreType.DMA((2,))],
        compiler_params=_cp(("arbitrary",)),
        name="moe_combine",
    )(dest_t, dest_t, gate_rows, x, t_shared, w_sh_down.astype(BF16), ys)


def _rope_tables(positions):
    half = ROPE_DIM // 2
    inv_freq = ROPE_THETA ** (-jnp.arange(0, ROPE_DIM, 2, dtype=F32) / ROPE_DIM)
    lane = np.arange(LANES)
    ang = positions.reshape(-1).astype(F32)[:, None] * jnp.tile(inv_freq, LANES // half)[None, :]
    cos, sin = jnp.cos(ang), jnp.sin(ang)
    cos_t = cos * (lane < ROPE_DIM).astype(np.float32)
    sin_a = sin * np.where(lane < half, -1.0, 0.0).astype(np.float32)
    sin_b = sin * ((lane >= half) & (lane < ROPE_DIM)).astype(np.float32)
    return cos_t, sin_a, sin_b


def _pad_lanes(v, width):
    return jnp.concatenate([v, jnp.zeros((width - v.shape[0],), v.dtype)])


def kernel(x, mem, positions, g_mix, w_in, g_qa, g_ka, rel_bias, g_cq, w_uq, g_ckv, w_ukv, g_qb, g_kb, w_o, g_cross, g_mem, w_xq, w_xkv, g_qx, g_kx, w_xo, g_ffn, w_router, router_bias, w_sh_gate, w_sh_up, w_sh_down, w_ex_gate, w_ex_up, w_ex_down):
    batch, seq, d = x.shape
    n = batch * seq
    mem_tokens = mem.shape[1]
    width_a = HEADS_A * HEAD_DIM
    x2d = x.reshape(n, d)

    h = _norm_bf16(x2d, g_mix)
    gs_qk = jnp.concatenate([jnp.tile(g_qa * HEAD_DIM ** -0.5, HEADS_A), jnp.tile(g_ka, HEADS_A)])
    w_in_t = w_in.T
    qk, v_a, c = _in_proj_all(h, w_in_t, gs_qk, 2 * width_a, width_a, Q_LORA + KV_LORA)
    w_kpe = jnp.pad(w_in_t[3 * width_a + Q_LORA + KV_LORA:], ((0, LANES - ROPE_DIM), (0, 0)))
    kpe = _proj_t(h, w_kpe, "in_proj_kpe")

    o_a = _attn_a(qk, v_a, _band_bias(rel_bias), batch, seq)

    cos_t, sin_a, sin_b = _rope_tables(positions)
    w_uq_pad = jnp.pad(w_uq.reshape(Q_LORA, HEADS_B, QK_B),
                       ((0, 0), (0, 0), (0, 2 * LANES - QK_B))).reshape(Q_LORA, HEADS_B * 2 * LANES)
    gq = _pad_lanes(g_qb, 2 * LANES).reshape(1, 2 * LANES)
    gk = _pad_lanes(g_kb, 2 * LANES).reshape(1, 2 * LANES)
    qf = _mla_q(c, g_cq, w_uq_pad, gq, cos_t, sin_a, sin_b)
    kf, v_b = _mla_kv(c, g_ckv, kpe, w_ukv, gk, cos_t, sin_a, sin_b)
    o_b = _attn_b(qf, kf, v_b, batch, seq)

    x1 = _out_proj(o_a, o_b, w_o, x2d)

    hw = MEM_HEADS * MEM_HEAD_DIM
    kvx = _normed_proj(mem, g_mem, w_xkv,
                       jnp.concatenate([jnp.tile(g_kx, MEM_HEADS), jnp.ones((hw,), F32)]), hw, "cross_kv")
    x2 = _cross_attn(x1, g_cross, w_xq, jnp.tile(g_qx * MEM_HEAD_DIM ** -0.5, MEM_HEADS), kvx, w_xo,
                     batch, seq, mem_tokens)

    hp, eid, gate, rank, cnt, t_shared = _router(x2, g_ffn, w_router, router_bias, w_sh_gate, w_sh_up)
    counts = cnt[:, 0]
    nstep = (counts + STEP_ROWS - 1) // STEP_ROWS
    step_end = jnp.cumsum(nstep).astype(I32)
    step_start = step_end - nstep
    n_steps = n * TOP_K // STEP_ROWS + N_EXPERTS
    experts = jnp.arange(N_EXPERTS, dtype=I32)
    start_of = jnp.sum(jnp.where(eid[:, :, None] == experts, step_start, 0), axis=-1)
    dest = start_of * STEP_ROWS + rank
    steps = jnp.arange(n_steps, dtype=I32)
    step_e = jnp.minimum(jnp.sum((step_end[None, :] <= steps[:, None]).astype(I32), axis=1), N_EXPERTS - 1)
    mine = step_e[:, None] == experts[None, :]
    rows_left = jnp.sum(jnp.where(mine, counts[None, :] - STEP_ROWS * (steps[:, None] - step_start[None, :]), 0),
                        axis=1)
    used = steps < step_end[-1]
    rows_here = jnp.where(used, jnp.clip(rows_left, 0, STEP_ROWS), 0)
    nsub = (rows_here + EXPERT_ROWS - 1) // EXPERT_ROWS
    nonempty = counts > 0
    ring_slot = (jnp.cumsum(nonempty.astype(I32)) - 1) % 2
    later = (experts[None, :] > experts[:, None]) & nonempty[None, :]
    succ = jnp.min(jnp.where(later, experts[None, :], N_EXPERTS), axis=1)
    succ = jnp.where(succ == N_EXPERTS, -1, succ)
    per_step = lambda v: jnp.sum(jnp.where(mine, v[None, :], 0), axis=1)
    first = (used & (steps == per_step(step_start))).astype(I32)
    sched = jnp.stack([step_e, nsub, first, per_step(ring_slot), per_step(succ)]).astype(I32)
    n_used = step_end[-1:]
    pitch = d // 2 // LANES

    xs = _dispatch(hp, dest, counts, step_start, n_steps * STEP_ROWS, pitch)
    hs = _expert_up(xs, w_ex_gate, w_ex_up, sched, n_used, n_steps, pitch)
    ys = _expert_down(hs, w_ex_down, sched, n_used, n_steps, pitch)

    out = _combine(dest, jnp.repeat(gate.T, 2, axis=0), x2, t_shared, w_sh_down, ys, pitch)
    return out.reshape(batch, seq, d)
```
